```python
import math
import jax, jax.numpy as jnp
from jax import lax
import numpy as np

D_MODEL = 1024
BATCH = 4
SEQ = 4096
DEPTH = 2

GRID_W = 64
CTX_LEN = 256
HEAD_DIM = 64
A_HEADS = 8
A_KV_HEADS = 2
A_GROUP = A_HEADS // A_KV_HEADS
B_HEADS = 8
NA_ROWS = 8
NA_COLS = 16
Q_BLOCK = 128
ROPE_THETA = 10000.0
MLA_HEADS = 16
MLA_Q_LORA = 768
MLA_KV_LORA = 256
MLA_NOPE_DIM = 64
MLA_ROPE_DIM = 32
MLA_V_DIM = 64
D_FF = 2816
CONV_WIDTH = 3
N_MOD = 6
EPS = 1e-6
DEEPNORM_ALPHA = (2 * DEPTH) ** 0.25
DEEPNORM_BETA = (8 * DEPTH) ** -0.25
HEAD_SCALE = HEAD_DIM ** -0.5
MLA_SCALE = (MLA_NOPE_DIM + MLA_ROPE_DIM) ** -0.5
_A_Q = A_HEADS * HEAD_DIM
_A_KV = A_KV_HEADS * HEAD_DIM
_B_W = B_HEADS * HEAD_DIM
EVEN_IN_DIM = _A_Q + 2 * _A_KV + 3 * _B_W
EVEN_SPLITS = (_A_Q, _A_Q + _A_KV, _A_Q + 2 * _A_KV, _A_Q + 2 * _A_KV + _B_W, _A_Q + 2 * _A_KV + 2 * _B_W)
EVEN_OUT_DIM = (A_HEADS + B_HEADS) * HEAD_DIM
ODD_IN_DIM = MLA_Q_LORA + MLA_KV_LORA + MLA_ROPE_DIM
ODD_OUT_DIM = MLA_HEADS * MLA_V_DIM

kernel_name = 'hybrid_gqa_natten_mla_convglu_dit'


def _layer_norm(x, g, b):
    xf = x.astype(jnp.float32)
    mu = jnp.mean(xf, axis=-1, keepdims=True)
    var = jnp.mean(jnp.square(xf - mu), axis=-1, keepdims=True)
    return ((xf - mu) * lax.rsqrt(var + EPS) * g + b).astype(x.dtype)


def _rms_norm(x, g):
    xf = x.astype(jnp.float32)
    return (xf * lax.rsqrt(jnp.mean(jnp.square(xf), axis=-1, keepdims=True) + EPS) * g).astype(x.dtype)


def _axial_rope(length, rot_dim, dtype):
    pos = jnp.arange(length, dtype=jnp.int32)
    rows = (pos // GRID_W).astype(jnp.float32)
    cols = (pos % GRID_W).astype(jnp.float32)
    axis_dim = rot_dim // 2
    inv = ROPE_THETA ** (-jnp.arange(0, axis_dim, 2, dtype=jnp.float32) / axis_dim)
    ang = jnp.concatenate([rows[:, None] * inv, cols[:, None] * inv], axis=-1)
    return jnp.cos(ang).astype(dtype), jnp.sin(ang).astype(dtype)


def _rope(x, cos, sin):
    xp = x.reshape(x.shape[:-1] + (x.shape[-1] // 2, 2))
    x1, x2 = xp[..., 0], xp[..., 1]
    return jnp.stack([x1 * cos - x2 * sin, x1 * sin + x2 * cos], axis=-1).reshape(x.shape)


def _heads(t, n, d):
    b, length, _ = t.shape
    return t.reshape(b, length, n, d).transpose(0, 2, 1, 3)


def _merge_heads(t):
    b, h, length, d = t.shape
    return t.transpose(0, 2, 1, 3).reshape(b, length, h * d)


def _dense_attention(q, k, v, kc, vc, scale):
    b, g, r, length, dk = q.shape
    nb = length // Q_BLOCK
    qb = jnp.moveaxis(q.reshape(b, g, r, nb, Q_BLOCK, dk), 3, 0)

    def one_block(qblk):
        s = jnp.concatenate([jnp.einsum('bgrqd,bgkd->bgrqk', qblk, k),
                             jnp.einsum('bgrqd,bgcd->bgrqc', qblk, kc)], axis=-1)
        p = jax.nn.softmax(s.astype(jnp.float32) * scale, axis=-1).astype(v.dtype)
        return (jnp.einsum('bgrqk,bgkd->bgrqd', p[..., :length], v)
                + jnp.einsum('bgrqc,bgcd->bgrqd', p[..., length:], vc))

    o = lax.map(one_block, qb)
    return jnp.moveaxis(o, 0, 3).reshape(b, g, r, length, -1)


def _context_attention(q, k, v, scale):
    s = jnp.einsum('bgrqd,bgkd->bgrqk', q, k).astype(jnp.float32) * scale
    p = jax.nn.softmax(s, axis=-1).astype(v.dtype)
    return jnp.einsum('bgrqk,bgkd->bgrqd', p, v)


def _neighbourhood_attention(q, k, v, kc, vc, rpb, scale):
    b, h, length, d = q.shape
    rows = length // GRID_W
    wr = min(NA_ROWS, rows)
    qg = q.reshape(b, h, rows, GRID_W, d)
    kg = k.reshape(b, h, rows, GRID_W, d)
    vg = v.reshape(b, h, rows, GRID_W, d)
    col = jnp.arange(GRID_W, dtype=jnp.int32)
    col_start = jnp.clip(col - NA_COLS // 2, 0, GRID_W - NA_COLS)
    col_idx = col_start[:, None] + jnp.arange(NA_COLS, dtype=jnp.int32)[None, :]
    col_bias = rpb[:, :, col_idx - col[:, None] + (NA_COLS - 1)].astype(jnp.float32)
    n_nb = wr * NA_COLS

    def one_row(args):
        r, q_row = args
        rs = jnp.clip(r - wr // 2, 0, rows - wr)
        k_win = lax.dynamic_slice_in_dim(kg, rs, wr, axis=2)[:, :, :, col_idx]
        v_win = lax.dynamic_slice_in_dim(vg, rs, wr, axis=2)[:, :, :, col_idx]
        row_off = rs + jnp.arange(wr, dtype=jnp.int32) - r + (NA_ROWS - 1)
        bias = col_bias[:, row_off].transpose(0, 2, 1, 3)
        s_nb = jnp.einsum('bhqd,bhrqkd->bhqrk', q_row, k_win).astype(jnp.float32) * scale + bias[None]
        s_ctx = jnp.einsum('bhqd,bhcd->bhqc', q_row, kc).astype(jnp.float32) * scale
        p = jax.nn.softmax(jnp.concatenate([s_nb.reshape(b, h, GRID_W, n_nb), s_ctx], axis=-1), axis=-1).astype(v.dtype)
        p_nb = p[..., :n_nb].reshape(b, h, GRID_W, wr, NA_COLS)
        return (jnp.einsum('bhqrk,bhrqkd->bhqd', p_nb, v_win)
                + jnp.einsum('bhqc,bhcd->bhqd', p[..., n_nb:], vc))

    out = lax.map(one_row, (jnp.arange(rows, dtype=jnp.int32), jnp.moveaxis(qg, 2, 0)))
    return jnp.moveaxis(out, 0, 2).reshape(b, h, length, d)


def _even_mixer(h, hc, w_in, q_gain, k_gain, rpb, w_out, need_ctx):
    b, length, _ = h.shape
    cos, sin = _axial_rope(length, HEAD_DIM, h.dtype)

    def project(t):
        qa, ka, va, qb, kb, vb = jnp.split(t @ w_in, EVEN_SPLITS, axis=-1)
        return (_rms_norm(_heads(qa, A_HEADS, HEAD_DIM), q_gain),
                _rms_norm(_heads(ka, A_KV_HEADS, HEAD_DIM), k_gain),
                _heads(va, A_KV_HEADS, HEAD_DIM),
                _heads(qb, B_HEADS, HEAD_DIM), _heads(kb, B_HEADS, HEAD_DIM), _heads(vb, B_HEADS, HEAD_DIM))

    qa, ka, va, qb, kb, vb = project(h)
    qac, kac, vac, qbc, kbc, vbc = project(hc)
    qa = _rope(qa, cos, sin)
    ka = _rope(ka, cos, sin)
    ya = _dense_attention(qa.reshape(b, A_KV_HEADS, A_GROUP, length, HEAD_DIM), ka, va, kac, vac,
                          HEAD_SCALE).reshape(b, A_HEADS, length, HEAD_DIM)
    yb = _neighbourhood_attention(qb, kb, vb, kbc, vbc, rpb, HEAD_SCALE)
    y = _merge_heads(jnp.concatenate([ya, yb], axis=1)) @ w_out
    yc = None
    if need_ctx:
        lc = hc.shape[1]
        yac = _context_attention(qac.reshape(b, A_KV_HEADS, A_GROUP, lc, HEAD_DIM), kac, vac,
                                 HEAD_SCALE).reshape(b, A_HEADS, lc, HEAD_DIM)
        ybc = _context_attention(qbc[:, :, None], kbc, vbc, HEAD_SCALE)[:, :, 0]
        yc = _merge_heads(jnp.concatenate([yac, ybc], axis=1)) @ w_out
    return y, yc


def _odd_mixer(h, hc, w_in, cq_gain, ckv_gain, w_uq, w_ukv, w_out, need_ctx):
    b, length, _ = h.shape
    cos, sin = _axial_rope(length, MLA_ROPE_DIM, h.dtype)

    def compress(t):
        c_q, c_kv, k_r = jnp.split(t @ w_in, (MLA_Q_LORA, MLA_Q_LORA + MLA_KV_LORA), axis=-1)
        return _rms_norm(c_q, cq_gain), _rms_norm(c_kv, ckv_gain), k_r

    def up_q(c_q):
        return jnp.split(_heads(c_q @ w_uq, MLA_HEADS, MLA_NOPE_DIM + MLA_ROPE_DIM), (MLA_NOPE_DIM,), axis=-1)

    def up_kv(c_kv, k_r):
        k_nope, v = jnp.split(_heads(c_kv @ w_ukv, MLA_HEADS, MLA_NOPE_DIM + MLA_V_DIM), (MLA_NOPE_DIM,), axis=-1)
        k_shared = jnp.broadcast_to(k_r[:, None], k_nope.shape[:-1] + (MLA_ROPE_DIM,))
        return jnp.concatenate([k_nope, k_shared], axis=-1), v

    c_q, c_kv, k_r = compress(h)
    cc_q, cc_kv, kc_r = compress(hc)
    q_nope, q_rope = up_q(c_q)
    q = jnp.concatenate([q_nope, _rope(q_rope, cos, sin)], axis=-1)
    k, v = up_kv(c_kv, _rope(k_r, cos, sin))
    kc, vc = up_kv(cc_kv, kc_r)
    y = _merge_heads(_dense_attention(q[:, :, None], k, v, kc, vc, MLA_SCALE)[:, :, 0]) @ w_out
    yc = None
    if need_ctx:
        qc = jnp.concatenate(up_q(cc_q), axis=-1)
        yc = _merge_heads(_context_attention(qc[:, :, None], kc, vc, MLA_SCALE)[:, :, 0]) @ w_out
    return y, yc


def _conv_ffn(h, w_up, conv_w, conv_b, w_down):
    gate, val = jnp.split(h @ w_up, 2, axis=-1)
    gate = lax.conv_general_dilated(gate, conv_w[:, None, :], (1,), ((CONV_WIDTH // 2, CONV_WIDTH // 2),),
                                    dimension_numbers=('NWC', 'WIO', 'NWC'), feature_group_count=D_FF) + conv_b
    return (jax.nn.silu(gate) * val) @ w_down


def _adaln(cond, w, b):
    return jnp.split(jax.nn.silu(cond) @ w + b, N_MOD, axis=-1)


def setup_inputs(seed: int = 0) -> dict:
    key = jax.random.key(seed)
    ks = iter(jax.random.split(key, 40))

    def nrm(shape, scale):
        return jax.random.normal(next(ks), shape, jnp.float32) * scale

    def gain(n):
        return 1.0 + nrm((n,), 0.1)

    d = D_MODEL
    return {
        'x': nrm((BATCH, SEQ, d), 1.0),
        'c': nrm((BATCH, d), 1.0),
        'ctx': nrm((BATCH, CTX_LEN, d), 1.0),
        'c_ctx': nrm((d,), 1.0),
        'l0_w_ada': nrm((d, N_MOD * d), 0.5 * d ** -0.5),
        'l0_b_ada': nrm((N_MOD * d,), 0.1),
        'l0_w_in': nrm((d, EVEN_IN_DIM), d ** -0.5),
        'l0_q_gain': gain(HEAD_DIM),
        'l0_k_gain': gain(HEAD_DIM),
        'l0_rpb': nrm((B_HEADS, 2 * NA_ROWS - 1, 2 * NA_COLS - 1), 0.5),
        'l0_w_out': nrm((EVEN_OUT_DIM, d), DEEPNORM_BETA * EVEN_OUT_DIM ** -0.5),
        'l0_ln1_g': gain(d),
        'l0_ln1_b': nrm((d,), 0.02),
        'l0_w_up': nrm((d, 2 * D_FF), d ** -0.5),
        'l0_conv_w': nrm((CONV_WIDTH, D_FF), CONV_WIDTH ** -0.5),
        'l0_conv_b': nrm((D_FF,), 0.02),
        'l0_w_down': nrm((D_FF, d), DEEPNORM_BETA * D_FF ** -0.5),
        'l0_ln2_g': gain(d),
        'l0_ln2_b': nrm((d,), 0.02),
        'l1_w_ada': nrm((d, N_MOD * d), 0.5 * d ** -0.5),
        'l1_b_ada': nrm((N_MOD * d,), 0.1),
        'l1_w_in': nrm((d, ODD_IN_DIM), d ** -0.5),
        'l1_cq_gain': gain(MLA_Q_LORA),
        'l1_ckv_gain': gain(MLA_KV_LORA),
        'l1_w_uq': nrm((MLA_Q_LORA, MLA_HEADS * (MLA_NOPE_DIM + MLA_ROPE_DIM)), MLA_Q_LORA ** -0.5),
        'l1_w_ukv': nrm((MLA_KV_LORA, MLA_HEADS * (MLA_NOPE_DIM + MLA_V_DIM)), MLA_KV_LORA ** -0.5),
        'l1_w_out': nrm((ODD_OUT_DIM, d), DEEPNORM_BETA * ODD_OUT_DIM ** -0.5),
        'l1_ln1_g': gain(d),
        'l1_ln1_b': nrm((d,), 0.02),
        'l1_w_up': nrm((d, 2 * D_FF), d ** -0.5),
        'l1_conv_w': nrm((CONV_WIDTH, D_FF), CONV_WIDTH ** -0.5),
        'l1_conv_b': nrm((D_FF,), 0.02),
        'l1_w_down': nrm((D_FF, d), DEEPNORM_BETA * D_FF ** -0.5),
        'l1_ln2_g': gain(d),
        'l1_ln2_b': nrm((d,), 0.02),
    }


def reference(x, c, ctx, c_ctx,
              l0_w_ada, l0_b_ada, l0_w_in, l0_q_gain, l0_k_gain, l0_rpb, l0_w_out, l0_ln1_g, l0_ln1_b,
              l0_w_up, l0_conv_w, l0_conv_b, l0_w_down, l0_ln2_g, l0_ln2_b,
              l1_w_ada, l1_b_ada, l1_w_in, l1_cq_gain, l1_ckv_gain, l1_w_uq, l1_w_ukv, l1_w_out, l1_ln1_g, l1_ln1_b,
              l1_w_up, l1_conv_w, l1_conv_b, l1_w_down, l1_ln2_g, l1_ln2_b):
    layers = (
        dict(w_ada=l0_w_ada, b_ada=l0_b_ada, w_in=l0_w_in, q_gain=l0_q_gain, k_gain=l0_k_gain, rpb=l0_rpb,
             w_out=l0_w_out, ln1_g=l0_ln1_g, ln1_b=l0_ln1_b, w_up=l0_w_up, conv_w=l0_conv_w, conv_b=l0_conv_b,
             w_down=l0_w_down, ln2_g=l0_ln2_g, ln2_b=l0_ln2_b),
        dict(w_ada=l1_w_ada, b_ada=l1_b_ada, w_in=l1_w_in, cq_gain=l1_cq_gain, ckv_gain=l1_ckv_gain,
             w_uq=l1_w_uq, w_ukv=l1_w_ukv, w_out=l1_w_out, ln1_g=l1_ln1_g, ln1_b=l1_ln1_b, w_up=l1_w_up,
             conv_w=l1_conv_w, conv_b=l1_conv_b, w_down=l1_w_down, ln2_g=l1_ln2_g, ln2_b=l1_ln2_b),
    )
    xc = ctx
    for i in range(DEPTH):
        p = layers[i]
        need_ctx = i < DEPTH - 1
        sh1, sc1, g1, sh2, sc2, g2 = [m[:, None, :] for m in _adaln(c, p['w_ada'], p['b_ada'])]
        csh1, csc1, cg1, csh2, csc2, cg2 = _adaln(c_ctx, p['w_ada'], p['b_ada'])
        h = x * (1.0 + sc1) + sh1
        hc = xc * (1.0 + csc1) + csh1
        if i % 2 == 0:
            y, yc = _even_mixer(h, hc, p['w_in'], p['q_gain'], p['k_gain'], p['rpb'], p['w_out'], need_ctx)
        else:
            y, yc = _odd_mixer(h, hc, p['w_in'], p['cq_gain'], p['ckv_gain'], p['w_uq'], p['w_ukv'],
                               p['w_out'], need_ctx)
        x = _layer_norm(DEEPNORM_ALPHA * x + g1 * y, p['ln1_g'], p['ln1_b'])
        h = x * (1.0 + sc2) + sh2
        x = _layer_norm(DEEPNORM_ALPHA * x + g2 * _conv_ffn(h, p['w_up'], p['conv_w'], p['conv_b'], p['w_down']),
                        p['ln2_g'], p['ln2_b'])
        if need_ctx:
            xc = _layer_norm(DEEPNORM_ALPHA * xc + cg1 * yc, p['ln1_g'], p['ln1_b'])
            hc = xc * (1.0 + csc2) + csh2
            xc = _layer_norm(DEEPNORM_ALPHA * xc + cg2 * _conv_ffn(hc, p['w_up'], p['conv_w'], p['conv_b'], p['w_down']),
                             p['ln2_g'], p['ln2_b'])
    return x
```

```python
import functools
import math

import numpy as np
import jax
import jax.numpy as jnp
from jax import lax
from jax.experimental import pallas as pl
from jax.experimental.pallas import tpu as pltpu

D_MODEL = 1024
BATCH = 4
SEQ = 4096
DEPTH = 2
GRID_W = 64
GRID_H = SEQ // GRID_W
CTX_LEN = 256
HEAD_DIM = 64
A_HEADS = 8
A_KV_HEADS = 2
A_GROUP = A_HEADS // A_KV_HEADS
B_HEADS = 8
NA_ROWS = 8
NA_COLS = 16
ROPE_THETA = 10000.0
MLA_HEADS = 16
MLA_Q_LORA = 768
MLA_KV_LORA = 256
MLA_NOPE_DIM = 64
MLA_ROPE_DIM = 32
MLA_V_DIM = 64
D_FF = 2816
N_MOD = 6
EPS = 1e-6
DEEPNORM_ALPHA = (2 * DEPTH) ** 0.25
HEAD_SCALE = HEAD_DIM ** -0.5
MLA_SCALE = (MLA_NOPE_DIM + MLA_ROPE_DIM) ** -0.5
LOG2E = math.log2(math.e)

LANES = 128
HALF = LANES // 2
MASK_VALUE = -1e30
VMEM_LIMIT = 56 * 1024 * 1024

SH1, SC1, G1, SH2, SC2, G2 = range(N_MOD)

F32 = jnp.float32
BF16 = jnp.bfloat16


def _cparams(sem):
    return pltpu.CompilerParams(dimension_semantics=sem, vmem_limit_bytes=VMEM_LIMIT)


def _pair_swap(y):
    n = y.shape[-1]
    lane = lax.broadcasted_iota(jnp.int32, y.shape, y.ndim - 1)
    from_right = pltpu.roll(y, n - 1, axis=y.ndim - 1)
    from_left = pltpu.roll(y, 1, axis=y.ndim - 1)
    return jnp.where(lane % 2 == 0, from_right, from_left)


def _tile_lanes(t, n):
    reps = n // t.shape[-1]
    return t if reps == 1 else jnp.concatenate([t] * reps, axis=-1)


def _proj_body(*refs, pro, mod_rows, seg_norm, rope, scale, add_kr, bias, resid_ln, transpose_out,
               cache_x):
    it = iter(refs)
    x_ref = next(it)
    mod_ref = next(it) if (pro == "mod" or resid_ln) else None
    pgain_ref = next(it) if pro == "rms" else None
    w_ref = next(it)
    bias_ref = next(it) if bias else None
    sgain_ref = next(it) if seg_norm else None
    c_ref = next(it) if rope else None
    s_ref = next(it) if rope else None
    kr_ref = next(it) if add_kr else None
    xres_ref = next(it) if resid_ln else None
    lng_ref = next(it) if resid_ln else None
    lnb_ref = next(it) if resid_ln else None
    out_ref = next(it)
    xs_ref = next(it) if cache_x else None

    def prologue():
        x = x_ref[0].astype(F32)
        if pro == "mod":
            sc = mod_ref[0, mod_rows[0]:mod_rows[0] + 1, :]
            sh = mod_ref[0, mod_rows[1]:mod_rows[1] + 1, :]
            x = x * (1.0 + sc) + sh
        elif pro == "rms":
            ms = jnp.mean(x * x, axis=-1, keepdims=True)
            x = x * lax.rsqrt(ms + EPS) * pgain_ref[...]
        elif pro == "silu":
            x = x * jax.nn.sigmoid(x)
        return x.astype(BF16)

    if cache_x:
        @pl.when(pl.program_id(2) == 0)
        def _():
            xs_ref[...] = prologue()
        xb = xs_ref[...]
    else:
        xb = x_ref[0]

    y = jnp.dot(xb, w_ref[...], preferred_element_type=F32)
    tn = y.shape[-1]
    if bias:
        y = y + bias_ref[...]
    if seg_norm:
        lane = lax.broadcasted_iota(jnp.int32, (y.shape[0], LANES), 1)
        lo = lane < HALF
        gain = sgain_ref[...]
        parts = []
        for s in range(tn // LANES):
            ys = y[:, s * LANES:(s + 1) * LANES]
            sq = ys * ys
            ms_lo = jnp.sum(jnp.where(lo, sq, 0.0), axis=-1, keepdims=True) * (1.0 / HEAD_DIM)
            ms_hi = jnp.sum(jnp.where(lo, 0.0, sq), axis=-1, keepdims=True) * (1.0 / HEAD_DIM)
            r = jnp.where(lo, lax.rsqrt(ms_lo + EPS), lax.rsqrt(ms_hi + EPS))
            parts.append(ys * r * gain)
        y = parts[0] if len(parts) == 1 else jnp.concatenate(parts, axis=-1)
    if add_kr:
        y = y + _tile_lanes(kr_ref[0], tn)
    if rope:
        y = y * _tile_lanes(c_ref[...], tn) + _pair_swap(y) * _tile_lanes(s_ref[...], tn)
    if scale is not None:
        y = y * scale
    if resid_ln:
        g = mod_ref[0, mod_rows[2]:mod_rows[2] + 1, :]
        z = DEEPNORM_ALPHA * xres_ref[0] + g * y
        mu = jnp.mean(z, axis=-1, keepdims=True)
        zc = z - mu
        var = jnp.mean(zc * zc, axis=-1, keepdims=True)
        y = zc * lax.rsqrt(var + EPS) * lng_ref[...] + lnb_ref[...]
    if transpose_out:
        out_ref[0] = y.T.astype(out_ref.dtype)
    else:
        out_ref[0] = y.astype(out_ref.dtype)


def _proj(x, w, *, name, tm, tn, x_col=0, k=None, mod=None, mod_row=None, pro=None, mod_rows=None,
          pgain=None, bias=None, seg_gain=None, rope_cs=None, scale=None, kr=None, kr_col=0,
          resid=None, ln=None, transpose_out=False, out_dtype=BF16):
    b, t, _ = x.shape
    k = w.shape[0] if k is None else k
    n = w.shape[1]
    assert t % tm == 0 and n % tn == 0
    resid_ln = resid is not None
    cache_x = pro is not None
    grid = (b, t // tm, n // tn)

    in_specs = [pl.BlockSpec((1, tm, k), lambda bi, i, j: (bi, i, x_col))]
    args = [x]
    if pro == "mod" or resid_ln:
        if mod_row is None:
            in_specs.append(pl.BlockSpec((1, N_MOD, D_MODEL), lambda bi, i, j: (bi, 0, 0)))
        else:
            in_specs.append(pl.BlockSpec((1, N_MOD, D_MODEL), lambda bi, i, j: (mod_row, 0, 0)))
        args.append(mod)
    if pro == "rms":
        in_specs.append(pl.BlockSpec((1, k), lambda bi, i, j: (0, 0)))
        args.append(pgain.reshape(1, k).astype(F32))
    in_specs.append(pl.BlockSpec((k, tn), lambda bi, i, j: (0, j)))
    args.append(w)
    if bias is not None:
        in_specs.append(pl.BlockSpec((1, tn), lambda bi, i, j: (0, j)))
        args.append(bias.reshape(1, n).astype(F32))
    if seg_gain is not None:
        in_specs.append(pl.BlockSpec((1, LANES), lambda bi, i, j: (0, 0)))
        args.append(seg_gain.reshape(1, LANES).astype(F32))
    if rope_cs is not None:
        for tab in rope_cs:
            in_specs.append(pl.BlockSpec((tm, LANES), lambda bi, i, j: (i, 0)))
            args.append(tab)
    if kr is not None:
        in_specs.append(pl.BlockSpec((1, tm, LANES), lambda bi, i, j: (bi, i, kr_col)))
        args.append(kr)
    if resid_ln:
        in_specs.append(pl.BlockSpec((1, tm, n), lambda bi, i, j: (bi, i, 0)))
        args.append(resid)
        for v in ln:
            in_specs.append(pl.BlockSpec((1, n), lambda bi, i, j: (0, 0)))
            args.append(v.reshape(1, n).astype(F32))

    if transpose_out:
        out_shape = jax.ShapeDtypeStruct((b, n, t), out_dtype)
        out_spec = pl.BlockSpec((1, tn, tm), lambda bi, i, j: (bi, j, i))
    else:
        out_shape = jax.ShapeDtypeStruct((b, t, n), out_dtype)
        out_spec = pl.BlockSpec((1, tm, tn), lambda bi, i, j: (bi, i, j))

    body = functools.partial(
        _proj_body, pro=pro, mod_rows=mod_rows, seg_norm=seg_gain is not None,
        rope=rope_cs is not None, scale=scale, add_kr=kr is not None, bias=bias is not None,
        resid_ln=resid_ln, transpose_out=transpose_out, cache_x=cache_x)
    scratch = [pltpu.VMEM((tm, k), BF16)] if cache_x else []
    return pl.pallas_call(
        body, grid=grid, in_specs=in_specs, out_specs=out_spec, out_shape=out_shape,
        scratch_shapes=scratch, name=name,
        compiler_params=_cparams(("parallel", "parallel", "arbitrary")),
    )(*args)


def _softmax_step(kk, qq, vv, m_prev, l_prev, acc_prev):
    s = jnp.dot(kk, qq, preferred_element_type=F32)
    m_new = jnp.maximum(m_prev, jnp.max(s, axis=0, keepdims=True))
    alpha = jnp.exp2(m_prev - m_new)
    p = jnp.exp2(s - m_new)
    l_new = alpha * l_prev + jnp.sum(p, axis=0, keepdims=True)
    acc_new = alpha * acc_prev + jnp.dot(vv, p.astype(BF16), preferred_element_type=F32)
    return m_new, l_new, acc_new


def _dense_body(*refs, heads, k_slot, v_row, nk, has_ctx):
    if has_ctx:
        qt_ref, k_ref, vt_ref, kc_ref, vct_ref, out_ref, m_ref, l_ref, acc_ref = refs
    else:
        qt_ref, k_ref, vt_ref, out_ref, m_ref, l_ref, acc_ref = refs
    ki = pl.program_id(2)
    last = nk if has_ctx else nk - 1

    @pl.when(ki == 0)
    def _():
        m_ref[...] = jnp.full(m_ref.shape, MASK_VALUE, F32)
        l_ref[...] = jnp.zeros(l_ref.shape, F32)
        acc_ref[...] = jnp.zeros(acc_ref.shape, F32)

    def step(kr, vr):
        for h in range(heads):
            ks, vo = k_slot[h], v_row[h]
            kk = kr[0, :, ks * LANES:(ks + 1) * LANES]
            qq = qt_ref[0, h * LANES:(h + 1) * LANES, :]
            vv = vr[0, vo:vo + HEAD_DIM, :]
            rows = slice(h * HEAD_DIM, (h + 1) * HEAD_DIM)
            m_new, l_new, acc_new = _softmax_step(
                kk, qq, vv, m_ref[h:h + 1, :], l_ref[h:h + 1, :], acc_ref[rows, :])
            m_ref[h:h + 1, :] = m_new
            l_ref[h:h + 1, :] = l_new
            acc_ref[rows, :] = acc_new

    if has_ctx:
        @pl.when(ki < nk)
        def _():
            step(k_ref, vt_ref)

        @pl.when(ki == nk)
        def _():
            step(kc_ref, vct_ref)
    else:
        step(k_ref, vt_ref)

    @pl.when(ki == last)
    def _():
        for h in range(heads):
            rows = slice(h * HEAD_DIM, (h + 1) * HEAD_DIM)
            acc_ref[rows, :] = acc_ref[rows, :] * (1.0 / l_ref[h:h + 1, :])
        out_ref[0] = acc_ref[...].T.astype(out_ref.dtype)


def _dense_attn(qt, k, vt, kc=None, vct=None, *, heads, k_slot, v_row, tq, tk, name):
    b, nq, lq = qt.shape
    _, lk, nkw = k.shape
    nv = vt.shape[1]
    assert nq == heads * LANES and lq % tq == 0 and lk % tk == 0
    nk = lk // tk
    has_ctx = kc is not None
    steps = nk + (1 if has_ctx else 0)
    in_specs = [
        pl.BlockSpec((1, nq, tq), lambda bi, qi, ki: (bi, 0, qi)),
        pl.BlockSpec((1, tk, nkw), lambda bi, qi, ki: (bi, jnp.minimum(ki, nk - 1), 0)),
        pl.BlockSpec((1, nv, tk), lambda bi, qi, ki: (bi, 0, jnp.minimum(ki, nk - 1))),
    ]
    args = [qt, k, vt]
    if has_ctx:
        lc = kc.shape[1]
        in_specs += [pl.BlockSpec((1, lc, nkw), lambda bi, qi, ki: (bi, 0, 0)),
                     pl.BlockSpec((1, nv, lc), lambda bi, qi, ki: (bi, 0, 0))]
        args += [kc, vct]
    body = functools.partial(_dense_body, heads=heads, k_slot=tuple(k_slot), v_row=tuple(v_row),
                             nk=nk, has_ctx=has_ctx)
    return pl.pallas_call(
        body, grid=(b, lq // tq, steps), in_specs=in_specs,
        out_specs=pl.BlockSpec((1, tq, heads * HEAD_DIM), lambda bi, qi, ki: (bi, qi, 0)),
        out_shape=jax.ShapeDtypeStruct((b, lq, heads * HEAD_DIM), BF16),
        scratch_shapes=[pltpu.VMEM((heads, tq), F32), pltpu.VMEM((heads, tq), F32),
                        pltpu.VMEM((heads * HEAD_DIM, tq), F32)],
        name=name, compiler_params=_cparams(("parallel", "parallel", "arbitrary")),
    )(*args)


NA_QROWS = 4
NA_TQ = NA_QROWS * GRID_W
NA_WROWS = 12
NA_WBLOCKS = NA_WROWS // NA_QROWS
NA_STEPS = GRID_H // NA_QROWS


def _natten_body(qt_ref, k0_ref, k1_ref, k2_ref, v0_ref, v1_ref, v2_ref, kc_ref, vct_ref, bias_ref,
                 out_ref, acc_ref):
    k_refs = (k0_ref, k1_ref, k2_ref)
    v_refs = (v0_ref, v1_ref, v2_ref)
    for h in range(B_HEADS):
        pair = h // 2
        lanes = slice(pair * LANES, (pair + 1) * LANES)
        rows = slice(h * HEAD_DIM, (h + 1) * HEAD_DIM)
        qq = qt_ref[0, h * LANES:(h + 1) * LANES, :]
        scores = []
        for blk in range(NA_WBLOCKS):
            s = jnp.dot(k_refs[blk][0, :, lanes], qq, preferred_element_type=F32)
            scores.append(s + bias_ref[0, h, blk * NA_TQ:(blk + 1) * NA_TQ, :])
        scores.append(jnp.dot(kc_ref[0, :, lanes], qq, preferred_element_type=F32))
        m = scores[0].max(axis=0, keepdims=True)
        for s in scores[1:]:
            m = jnp.maximum(m, s.max(axis=0, keepdims=True))
        l = jnp.zeros_like(m)
        o = jnp.zeros((HEAD_DIM, NA_TQ), F32)
        for idx, s in enumerate(scores):
            p = jnp.exp2(s - m)
            l = l + jnp.sum(p, axis=0, keepdims=True)
            vv = vct_ref[0, rows, :] if idx == NA_WBLOCKS else v_refs[idx][0, rows, :]
            o = o + jnp.dot(vv, p.astype(BF16), preferred_element_type=F32)
        acc_ref[rows, :] = o * (1.0 / l)
    out_ref[0] = acc_ref[...].T.astype(out_ref.dtype)


def _natten(qt, k, vt, kc, vct, bias):
    b = qt.shape[0]
    nw = B_HEADS * HEAD_DIM

    def wstart(i):
        return jnp.clip(i - 1, 0, NA_STEPS - NA_WBLOCKS)

    def cls(i):
        return jnp.where(i == 0, 0, jnp.where(i == NA_STEPS - 1, 2, 1))

    in_specs = [pl.BlockSpec((1, B_HEADS * LANES, NA_TQ), lambda bi, i: (bi, 0, i))]
    in_specs += [pl.BlockSpec((1, NA_TQ, nw), functools.partial(lambda bi, i, j: (bi, wstart(i) + j, 0), j=j))
                 for j in range(NA_WBLOCKS)]
    in_specs += [pl.BlockSpec((1, nw, NA_TQ), functools.partial(lambda bi, i, j: (bi, 0, wstart(i) + j), j=j))
                 for j in range(NA_WBLOCKS)]
    in_specs += [pl.BlockSpec((1, CTX_LEN, nw), lambda bi, i: (bi, 0, 0)),
                 pl.BlockSpec((1, nw, CTX_LEN), lambda bi, i: (bi, 0, 0)),
                 pl.BlockSpec((1, B_HEADS, NA_WROWS * GRID_W, NA_TQ), lambda bi, i: (cls(i), 0, 0, 0))]
    return pl.pallas_call(
        _natten_body, grid=(b, NA_STEPS), in_specs=in_specs,
        out_specs=pl.BlockSpec((1, NA_TQ, nw), lambda bi, i: (bi, i, 0)),
        out_shape=jax.ShapeDtypeStruct((b, SEQ, nw), BF16),
        scratch_shapes=[pltpu.VMEM((nw, NA_TQ), F32)],
        name="natten", compiler_params=_cparams(("parallel", "arbitrary")),
    )(qt, k, k, k, vt, vt, vt, kc, vct, bias)


def _natten_bias(rpb):
    tables = []
    for r0 in (0, 2 * NA_QROWS, GRID_H - NA_QROWS):
        ws = min(max(r0 - NA_QROWS, 0), GRID_H - NA_WROWS)
        kk = np.arange(NA_WROWS * GRID_W)
        qq = np.arange(NA_TQ)
        k_row, k_col = (ws + kk // GRID_W)[:, None], (kk % GRID_W)[:, None]
        q_row, q_col = (r0 + qq // GRID_W)[None, :], (qq % GRID_W)[None, :]
        rs = np.clip(q_row - NA_ROWS // 2, 0, GRID_H - NA_ROWS)
        cs = np.clip(q_col - NA_COLS // 2, 0, GRID_W - NA_COLS)
        ok = (k_row >= rs) & (k_row < rs + NA_ROWS) & (k_col >= cs) & (k_col < cs + NA_COLS)
        ri = np.clip(k_row - q_row + NA_ROWS - 1, 0, 2 * NA_ROWS - 2)
        ci = np.clip(k_col - q_col + NA_COLS - 1, 0, 2 * NA_COLS - 2)
        vals = rpb[:, ri, ci].astype(F32) * LOG2E
        tables.append(jnp.where(jnp.asarray(ok)[None], vals, MASK_VALUE))
    return jnp.stack(tables)


FFN_HALO = 16


def _ffn_body(x_ref, xp_ref, xn_ref, mod_ref, wg_ref, wv_ref, wd_ref, cw_ref, cb_ref, lng_ref, lnb_ref,
              out_ref, hs_ref, g_ref, acc_ref, *, tm, n_tiles):
    i = pl.program_id(1)
    f = pl.program_id(2)
    nf = pl.num_programs(2)

    @pl.when(f == 0)
    def _():
        sc = mod_ref[0, SC2:SC2 + 1, :]
        sh = mod_ref[0, SH2:SH2 + 1, :]

        def modulate(v):
            return v * (1.0 + sc) + sh

        hp = jnp.where(i > 0, modulate(xp_ref[0]), 0.0)
        hn = jnp.where(i < n_tiles - 1, modulate(xn_ref[0]), 0.0)
        hs_ref[0:FFN_HALO, :] = hp.astype(BF16)
        hs_ref[FFN_HALO:FFN_HALO + tm, :] = modulate(x_ref[0]).astype(BF16)
        hs_ref[FFN_HALO + tm:, :] = hn.astype(BF16)
        acc_ref[...] = jnp.zeros(acc_ref.shape, F32)

    g_ref[...] = jnp.dot(hs_ref[...], wg_ref[...], preferred_element_type=F32)
    val = jnp.dot(hs_ref[FFN_HALO:FFN_HALO + tm, :], wv_ref[...], preferred_element_type=F32)
    gate = (cw_ref[0:1, :] * g_ref[FFN_HALO - 1:FFN_HALO - 1 + tm, :]
            + cw_ref[1:2, :] * g_ref[FFN_HALO:FFN_HALO + tm, :]
            + cw_ref[2:3, :] * g_ref[FFN_HALO + 1:FFN_HALO + 1 + tm, :]
            + cb_ref[...])
    act = gate * jax.nn.sigmoid(gate) * val
    acc_ref[...] += jnp.dot(act.astype(BF16), wd_ref[...], preferred_element_type=F32)

    @pl.when(f == nf - 1)
    def _():
        g = mod_ref[0, G2:G2 + 1, :]
        z = DEEPNORM_ALPHA * x_ref[0] + g * acc_ref[...]
        mu = jnp.mean(z, axis=-1, keepdims=True)
        zc = z - mu
        var = jnp.mean(zc * zc, axis=-1, keepdims=True)
        out_ref[0] = zc * lax.rsqrt(var + EPS) * lng_ref[...] + lnb_ref[...]


def _ffn(x, mod, mod_row, wg, wv, wd, conv_w, conv_b, ln_g, ln_b, *, tm, tf, name):
    b, t, d = x.shape
    n_tiles = t // tm
    hb = tm // FFN_HALO
    n_hblocks = t // FFN_HALO
    mod_map = (lambda bi, i, f: (bi, 0, 0)) if mod_row is None else (lambda bi, i, f: (mod_row, 0, 0))
    in_specs = [
        pl.BlockSpec((1, tm, d), lambda bi, i, f: (bi, i, 0)),
        pl.BlockSpec((1, FFN_HALO, d), lambda bi, i, f: (bi, jnp.maximum(i * hb - 1, 0), 0)),
        pl.BlockSpec((1, FFN_HALO, d), lambda bi, i, f: (bi, jnp.minimum((i + 1) * hb, n_hblocks - 1), 0)),
        pl.BlockSpec((1, N_MOD, d), mod_map),
        pl.BlockSpec((d, tf), lambda bi, i, f: (0, f)),
        pl.BlockSpec((d, tf), lambda bi, i, f: (0, f)),
        pl.BlockSpec((tf, d), lambda bi, i, f: (f, 0)),
        pl.BlockSpec((3, tf), lambda bi, i, f: (0, f)),
        pl.BlockSpec((1, tf), lambda bi, i, f: (0, f)),
        pl.BlockSpec((1, d), lambda bi, i, f: (0, 0)),
        pl.BlockSpec((1, d), lambda bi, i, f: (0, 0)),
    ]
    body = functools.partial(_ffn_body, tm=tm, n_tiles=n_tiles)
    return pl.pallas_call(
        body, grid=(b, n_tiles, D_FF // tf), in_specs=in_specs,
        out_specs=pl.BlockSpec((1, tm, d), lambda bi, i, f: (bi, i, 0)),
        out_shape=jax.ShapeDtypeStruct((b, t, d), F32),
        scratch_shapes=[pltpu.VMEM((tm + 2 * FFN_HALO, d), BF16),
                        pltpu.VMEM((tm + 2 * FFN_HALO, tf), F32),
                        pltpu.VMEM((tm, d), F32)],
        name=name, compiler_params=_cparams(("parallel", "parallel", "arbitrary")),
    )(x, x, x, mod, wg, wv, wd, conv_w.astype(F32), conv_b.reshape(1, D_FF).astype(F32),
      ln_g.reshape(1, d).astype(F32), ln_b.reshape(1, d).astype(F32))


def _rope_tables(rot_dim, lane_offset, scale):
    pos = jnp.arange(SEQ, dtype=jnp.int32)
    rows = (pos // GRID_W).astype(F32)
    cols = (pos % GRID_W).astype(F32)
    axis_dim = rot_dim // 2
    inv = ROPE_THETA ** (-jnp.arange(0, axis_dim, 2, dtype=F32) / axis_dim)
    ang = jnp.concatenate([rows[:, None] * inv, cols[:, None] * inv], axis=-1)
    cos = jnp.repeat(jnp.cos(ang), 2, axis=-1)
    sin = jnp.repeat(jnp.sin(ang), 2, axis=-1)
    sign = jnp.tile(jnp.array([-1.0, 1.0], F32), rot_dim // 2)
    sin = sin * sign
    if rot_dim == HEAD_DIM:
        c = jnp.concatenate([cos, cos], axis=-1)
        s = jnp.concatenate([sin, sin], axis=-1)
    else:
        c = jnp.ones((SEQ, LANES), F32).at[:, lane_offset:lane_offset + rot_dim].set(cos)
        s = jnp.zeros((SEQ, LANES), F32).at[:, lane_offset:lane_offset + rot_dim].set(sin)
    return c * scale, s * scale


def _slot_columns(w, heads, width, half_of):
    k = w.shape[0]
    out = jnp.zeros((k, heads * LANES), w.dtype)
    for h in range(heads):
        off = h * LANES + half_of(h) * HALF
        out = out.at[:, off:off + width].set(w[:, h * width:(h + 1) * width])
    return out


MOD_ROWS = 16


def _modvec(c, c_ctx, w_ada, b_ada):
    cond = jnp.zeros((1, MOD_ROWS, D_MODEL), F32).at[0, :BATCH].set(c).at[0, BATCH].set(c_ctx)
    m = _proj(cond, w_ada.astype(BF16), name="adaln", tm=MOD_ROWS, tn=1024, pro="silu", bias=b_ada,
              out_dtype=F32)
    return m.reshape(MOD_ROWS, N_MOD, D_MODEL)


CTX_MOD_ROW = BATCH
TM = 512
TMC = CTX_LEN
TQ = 512
TK = 512
TF = 256


def _layer0_mixer(x, xc, mod, w_in, q_gain, k_gain, rpb):
    w = w_in.astype(BF16)
    aq, akv, bw = A_HEADS * HEAD_DIM, A_KV_HEADS * HEAD_DIM, B_HEADS * HEAD_DIM
    o = 0
    w_qa = _slot_columns(w[:, o:o + aq], A_HEADS, HEAD_DIM, lambda h: h // A_GROUP); o += aq
    w_ka = w[:, o:o + akv]; o += akv
    w_va = w[:, o:o + akv]; o += akv
    w_qb = _slot_columns(w[:, o:o + bw], B_HEADS, HEAD_DIM, lambda h: h % 2); o += bw
    w_kb = w[:, o:o + bw]; o += bw
    w_vb = w[:, o:o + bw]
    qg = jnp.tile(q_gain, 2)
    kg = jnp.tile(k_gain, 2)
    qscale = HEAD_SCALE * LOG2E
    cq, sq = _rope_tables(HEAD_DIM, 0, qscale)
    ck, sk = _rope_tables(HEAD_DIM, 0, 1.0)
    pm = dict(pro="mod", mod=mod, mod_rows=(SC1, SH1))

    def project(t, tm, mod_row, rope, tag):
        kw = dict(pm, tm=tm, mod_row=mod_row)
        qa_t = _proj(t, w_qa, name="qa" + tag, tn=512, seg_gain=qg, transpose_out=True,
                     **(dict(rope_cs=(cq, sq)) if rope else dict(scale=qscale)), **kw)
        ka = _proj(t, w_ka, name="ka" + tag, tn=128, seg_gain=kg,
                   **(dict(rope_cs=(ck, sk)) if rope else {}), **kw)
        va_t = _proj(t, w_va, name="va" + tag, tn=128, transpose_out=True, **kw)
        qb_t = _proj(t, w_qb, name="qb" + tag, tn=512, scale=qscale, transpose_out=True, **kw)
        kb = _proj(t, w_kb, name="kb" + tag, tn=512, **kw)
        vb_t = _proj(t, w_vb, name="vb" + tag, tn=512, transpose_out=True, **kw)
        return qa_t, ka, va_t, qb_t, kb, vb_t

    qa_t, ka, va_t, qb_t, kb, vb_t = project(x, TM, None, True, "")
    qac_t, kac, vac_t, qbc_t, kbc, vbc_t = project(xc, TMC, CTX_MOD_ROW, False, "_ctx")

    a_kslot = [0] * A_HEADS
    a_vrow = [(h // A_GROUP) * HEAD_DIM for h in range(A_HEADS)]
    b_kslot = [h // 2 for h in range(B_HEADS)]
    b_vrow = [h * HEAD_DIM for h in range(B_HEADS)]
    ya = _dense_attn(qa_t, ka, va_t, kac, vac_t, heads=A_HEADS, k_slot=a_kslot, v_row=a_vrow,
                     tq=TQ, tk=TK, name="gqa")
    yb = _natten(qb_t, kb, vb_t, kbc, vbc_t, _natten_bias(rpb))
    yac = _dense_attn(qac_t, kac, vac_t, heads=A_HEADS, k_slot=a_kslot, v_row=a_vrow,
                      tq=CTX_LEN, tk=CTX_LEN, name="gqa_ctx")
    ybc = _dense_attn(qbc_t, kbc, vbc_t, heads=B_HEADS, k_slot=b_kslot, v_row=b_vrow,
                      tq=CTX_LEN, tk=CTX_LEN, name="nbr_ctx")
    return jnp.concatenate([ya, yb], axis=-1), jnp.concatenate([yac, ybc], axis=-1)


def _layer1_mixer(x, xc, mod, w_in, cq_gain, ckv_gain, w_uq, w_ukv):
    qk_dim = MLA_NOPE_DIM + MLA_ROPE_DIM
    kv_dim = MLA_NOPE_DIM + MLA_V_DIM
    n_in = MLA_Q_LORA + MLA_KV_LORA
    w_dn = jnp.zeros((D_MODEL, n_in + LANES), BF16)
    w_dn = w_dn.at[:, :n_in].set(w_in[:, :n_in].astype(BF16))
    w_dn = w_dn.at[:, n_in + MLA_NOPE_DIM:n_in + qk_dim].set(w_in[:, n_in:].astype(BF16))
    w_q = _slot_columns(w_uq.astype(BF16), MLA_HEADS, qk_dim, lambda h: 0)
    wkv = w_ukv.astype(BF16).reshape(MLA_KV_LORA, MLA_HEADS, kv_dim)
    w_k = _slot_columns(wkv[:, :, :MLA_NOPE_DIM].reshape(MLA_KV_LORA, -1), MLA_HEADS, MLA_NOPE_DIM, lambda h: 0)
    w_v = wkv[:, :, MLA_NOPE_DIM:].reshape(MLA_KV_LORA, MLA_HEADS * MLA_V_DIM)

    qscale = MLA_SCALE * LOG2E
    cq, sq = _rope_tables(MLA_ROPE_DIM, MLA_NOPE_DIM, qscale)
    ck, sk = _rope_tables(MLA_ROPE_DIM, MLA_NOPE_DIM, 1.0)
    kr_col = n_in // LANES

    def down(t, tm, mod_row, tag):
        return _proj(t, w_dn, name="mla_down" + tag, tm=tm, tn=384, pro="mod", mod=mod, mod_row=mod_row,
                     mod_rows=(SC1, SH1), out_dtype=F32)

    def up_kv(cmp, tm, rope, tag):
        common = dict(tm=tm, x_col=MLA_Q_LORA // MLA_KV_LORA, k=MLA_KV_LORA, pro="rms", pgain=ckv_gain)
        kk = _proj(cmp, w_k, name="mla_k" + tag, tn=512, kr=cmp, kr_col=kr_col,
                   **(dict(rope_cs=(ck, sk)) if rope else {}), **common)
        vt = _proj(cmp, w_v, name="mla_v" + tag, tn=512, transpose_out=True, **common)
        return kk, vt

    cmp = down(x, TM, None, "")
    cmpc = down(xc, TMC, CTX_MOD_ROW, "_ctx")
    qt = _proj(cmp, w_q, name="mla_q", tm=TM, tn=512, x_col=0, k=MLA_Q_LORA, pro="rms", pgain=cq_gain,
               rope_cs=(cq, sq), transpose_out=True)
    kk, vt = up_kv(cmp, TM, True, "")
    kkc, vtc = up_kv(cmpc, TMC, False, "_ctx")
    return _dense_attn(qt, kk, vt, kkc, vtc, heads=MLA_HEADS, k_slot=list(range(MLA_HEADS)),
                       v_row=[h * MLA_V_DIM for h in range(MLA_HEADS)], tq=TQ, tk=TK, name="mla")


def _post_mixer(x, y, mod, mod_row, w_out, ln1, w_up, conv_w, conv_b, w_down, ln2, tm, tag):
    x = _proj(y, w_out.astype(BF16), name="out_ln" + tag, tm=tm, tn=D_MODEL, mod=mod, mod_row=mod_row,
              mod_rows=(None, None, G1), resid=x, ln=ln1, out_dtype=F32)
    wup = w_up.astype(BF16)
    return _ffn(x, mod, mod_row, wup[:, :D_FF], wup[:, D_FF:], w_down.astype(BF16), conv_w, conv_b,
                ln2[0], ln2[1], tm=tm, tf=TF, name="ffn" + tag)


def kernel(x, c, ctx, c_ctx, l0_w_ada, l0_b_ada, l0_w_in, l0_q_gain, l0_k_gain, l0_rpb, l0_w_out, l0_ln1_g, l0_ln1_b, l0_w_up, l0_conv_w, l0_conv_b, l0_w_down, l0_ln2_g, l0_ln2_b, l1_w_ada, l1_b_ada, l1_w_in, l1_cq_gain, l1_ckv_gain, l1_w_uq, l1_w_ukv, l1_w_out, l1_ln1_g, l1_ln1_b, l1_w_up, l1_conv_w, l1_conv_b, l1_w_down, l1_ln2_g, l1_ln2_b):
    xc = ctx
    mod = _modvec(c, c_ctx, l0_w_ada, l0_b_ada)
    y, yc = _layer0_mixer(x, xc, mod, l0_w_in, l0_q_gain, l0_k_gain, l0_rpb)
    post0 = (l0_w_out, (l0_ln1_g, l0_ln1_b), l0_w_up, l0_conv_w, l0_conv_b, l0_w_down, (l0_ln2_g, l0_ln2_b))
    x = _post_mixer(x, y, mod, None, *post0, tm=TM, tag="0")
    xc = _post_mixer(xc, yc, mod, CTX_MOD_ROW, *post0, tm=TMC, tag="0_ctx")
    mod = _modvec(c, c_ctx, l1_w_ada, l1_b_ada)
    y = _layer1_mixer(x, xc, mod, l1_w_in, l1_cq_gain, l1_ckv_gain, l1_w_uq, l1_w_ukv)
    post1 = (l1_w_out, (l1_ln1_g, l1_ln1_b), l1_w_up, l1_conv_w, l1_conv_b, l1_w_down, (l1_ln2_g, l1_ln2_b))
    return _post_mixer(x, y, mod, None, *post1, tm=TM, tag="1")
```

```python
import functools
import math

import numpy as np
import jax
import jax.numpy as jnp
from jax import lax
from jax.experimental import pallas as pl
from jax.experimental.pallas import tpu as pltpu

D_MODEL = 1024
BATCH = 4
SEQ = 4096
DEPTH = 2
GRID_W = 64
GRID_H = SEQ // GRID_W
CTX_LEN = 256
HEAD_DIM = 64
A_HEADS = 8
A_KV_HEADS = 2
A_GROUP = A_HEADS // A_KV_HEADS
B_HEADS = 8
NA_ROWS = 8
NA_COLS = 16
ROPE_THETA = 10000.0
MLA_HEADS = 16
MLA_Q_LORA = 768
MLA_KV_LORA = 256
MLA_NOPE_DIM = 64
MLA_ROPE_DIM = 32
MLA_V_DIM = 64
D_FF = 2816
N_MOD = 6
EPS = 1e-6
DEEPNORM_ALPHA = (2 * DEPTH) ** 0.25
HEAD_SCALE = HEAD_DIM ** -0.5
MLA_SCALE = (MLA_NOPE_DIM + MLA_ROPE_DIM) ** -0.5
LOG2E = math.log2(math.e)

LANES = 128
HALF = LANES // 2
BF16_ROWS = 16
V_ROWS = HEAD_DIM + BF16_ROWS
MASK_VALUE = -1e30
VMEM_LIMIT = 56 * 1024 * 1024

SH1, SC1, G1, SH2, SC2, G2 = range(N_MOD)

F32 = jnp.float32
BF16 = jnp.bfloat16

MOD_ROWS = 16
CTX_MOD_ROW = BATCH
TM = 512
TMC = CTX_LEN
TQ = 512
TK = 512
TF = D_FF // 2
FFN_ROWS = 256
FFN_HALO = BF16_ROWS
ROW_CHUNK = 512


def _cparams(sem):
    return pltpu.CompilerParams(dimension_semantics=sem, vmem_limit_bytes=VMEM_LIMIT)


def _dot(a, b):
    return jnp.dot(a, b, preferred_element_type=F32)


def _dot_nt(a, b):
    return lax.dot_general(a, b, (((1,), (1,)), ((), ())), preferred_element_type=F32)


def _pair_swap(y, axis):
    n = y.shape[axis]
    idx = lax.broadcasted_iota(jnp.int32, y.shape, axis)
    nxt = pltpu.roll(y, n - 1, axis=axis)
    prv = pltpu.roll(y, 1, axis=axis)
    return jnp.where(idx % 2 == 0, nxt, prv)


def _rotate(y, c, s, axis):
    return y * c + _pair_swap(y, axis) * s


def _layer_norm(z, g, b):
    mu = jnp.mean(z, axis=-1, keepdims=True)
    zc = z - mu
    var = jnp.mean(zc * zc, axis=-1, keepdims=True)
    return zc * lax.rsqrt(var + EPS) * g + b


def _store_vt(out_ref, yt, heads):
    ones = jnp.ones((BF16_ROWS, yt.shape[1]), out_ref.dtype)
    for h in range(heads):
        out_ref[0, h * V_ROWS:h * V_ROWS + HEAD_DIM, :] = yt[h * HEAD_DIM:(h + 1) * HEAD_DIM].astype(out_ref.dtype)
        out_ref[0, h * V_ROWS + HEAD_DIM:(h + 1) * V_ROWS, :] = ones


def _proj_body(*refs, silu, bias, resid_ln, gate_row):
    it = iter(refs)
    x_ref = next(it)
    w_ref = next(it)
    bias_ref = next(it) if bias else None
    mod_ref = next(it) if resid_ln else None
    xres_ref = next(it) if resid_ln else None
    lng_ref = next(it) if resid_ln else None
    lnb_ref = next(it) if resid_ln else None
    out_ref = next(it)
    x = x_ref[0]
    if silu:
        x = x * jax.nn.sigmoid(x)
    y = _dot(x.astype(BF16), w_ref[...])
    if bias:
        y = y + bias_ref[...]
    if resid_ln:
        g = mod_ref[0, gate_row:gate_row + 1, :]
        y = _layer_norm(DEEPNORM_ALPHA * xres_ref[0] + g * y, lng_ref[...], lnb_ref[...])
    out_ref[0] = y.astype(out_ref.dtype)


def _proj(x, w, *, name, tm, tn, silu=False, bias=None, mod=None, mod_row=None, gate_row=None,
          resid=None, ln=None, out_dtype=F32):
    b, t, k = x.shape
    n = w.shape[1]
    assert t % tm == 0 and n % tn == 0
    resid_ln = resid is not None
    in_specs = [pl.BlockSpec((1, tm, k), lambda bi, i, j: (bi, i, 0)),
                pl.BlockSpec((k, tn), lambda bi, i, j: (0, j))]
    args = [x, w]
    if bias is not None:
        in_specs.append(pl.BlockSpec((1, tn), lambda bi, i, j: (0, j)))
        args.append(bias.reshape(1, n).astype(F32))
    if resid_ln:
        mod_map = (lambda bi, i, j: (bi, 0, 0)) if mod_row is None else (lambda bi, i, j: (mod_row, 0, 0))
        in_specs += [pl.BlockSpec((1, N_MOD, D_MODEL), mod_map),
                     pl.BlockSpec((1, tm, n), lambda bi, i, j: (bi, i, 0)),
                     pl.BlockSpec((1, n), lambda bi, i, j: (0, 0)),
                     pl.BlockSpec((1, n), lambda bi, i, j: (0, 0))]
        args += [mod, resid, ln[0].reshape(1, n).astype(F32), ln[1].reshape(1, n).astype(F32)]
    body = functools.partial(_proj_body, silu=silu, bias=bias is not None, resid_ln=resid_ln, gate_row=gate_row)
    return pl.pallas_call(
        body, grid=(b, t // tm, n // tn), in_specs=in_specs,
        out_specs=pl.BlockSpec((1, tm, tn), lambda bi, i, j: (bi, i, j)),
        out_shape=jax.ShapeDtypeStruct((b, t, n), out_dtype), name=name,
        compiler_params=_cparams(("parallel", "parallel", "arbitrary")),
    )(*args)


def _modvec(c, c_ctx, w_ada, b_ada):
    cond = jnp.zeros((1, MOD_ROWS, D_MODEL), F32).at[0, :BATCH].set(c).at[0, BATCH].set(c_ctx)
    m = _proj(cond, w_ada.astype(BF16), name="adaln", tm=MOD_ROWS, tn=1024, silu=True, bias=b_ada)
    return m.reshape(MOD_ROWS, N_MOD, D_MODEL)


def _l0_proj_body(x_ref, mod_ref, wqa_ref, cqa_ref, sqa_ref, wka_ref, cka_ref, ska_ref, wva_ref, wqb_ref,
                  wkb_ref, wvb_ref, qa_ref, ka_ref, va_ref, qb_ref, kb_ref, vb_ref):
    sc = mod_ref[0, SC1:SC1 + 1, :]
    sh = mod_ref[0, SH1:SH1 + 1, :]
    xs = (x_ref[0] * (1.0 + sc) + sh).astype(BF16)

    cq, sq = cqa_ref[...], sqa_ref[...]
    for r in range(A_HEADS * LANES // ROW_CHUNK):
        yt = _dot_nt(wqa_ref[r * ROW_CHUNK:(r + 1) * ROW_CHUNK, :], xs)
        for s in range(ROW_CHUNK // LANES):
            ys = yt[s * LANES:(s + 1) * LANES]
            rot = _rotate(ys, cq, sq, 0)
            parts = []
            for half in range(2):
                seg = ys[half * HALF:(half + 1) * HALF]
                ms = jnp.mean(seg * seg, axis=0, keepdims=True)
                parts.append(rot[half * HALF:(half + 1) * HALF] * lax.rsqrt(ms + EPS))
            row0 = r * ROW_CHUNK + s * LANES
            qa_ref[0, row0:row0 + LANES, :] = jnp.concatenate(parts, axis=0).astype(qa_ref.dtype)

    y = _dot(xs, wka_ref[...])
    lo = lax.broadcasted_iota(jnp.int32, y.shape, 1) < HALF
    ysq = y * y
    ms_lo = jnp.sum(jnp.where(lo, ysq, 0.0), axis=-1, keepdims=True) * (1.0 / HEAD_DIM)
    ms_hi = jnp.sum(jnp.where(lo, 0.0, ysq), axis=-1, keepdims=True) * (1.0 / HEAD_DIM)
    rn = jnp.where(lo, lax.rsqrt(ms_lo + EPS), lax.rsqrt(ms_hi + EPS))
    ka_ref[0] = (_rotate(y, cka_ref[...], ska_ref[...], 1) * rn).astype(ka_ref.dtype)

    _store_vt(va_ref, _dot_nt(wva_ref[...], xs), A_KV_HEADS)

    for r in range(B_HEADS * LANES // ROW_CHUNK):
        rows = slice(r * ROW_CHUNK, (r + 1) * ROW_CHUNK)
        qb_ref[0, rows, :] = (_dot_nt(wqb_ref[rows, :], xs) * (HEAD_SCALE * LOG2E)).astype(qb_ref.dtype)
    kb_ref[0] = _dot(xs, wkb_ref[...]).astype(kb_ref.dtype)
    _store_vt(vb_ref, _dot_nt(wvb_ref[...], xs), B_HEADS)


def _l0_proj(x, mod, mod_row, weights, tables, *, tm, name):
    b, t, d = x.shape
    wqa, wka, wva, wqb, wkb, wvb = weights
    cqa, sqa, cka, ska = tables
    mod_map = (lambda bi, i: (bi, 0, 0)) if mod_row is None else (lambda bi, i: (mod_row, 0, 0))

    def full(a):
        return pl.BlockSpec(a.shape, lambda bi, i: (0,) * a.ndim)

    in_specs = [pl.BlockSpec((1, tm, d), lambda bi, i: (bi, i, 0)),
                pl.BlockSpec((1, N_MOD, d), mod_map),
                full(wqa), pl.BlockSpec((LANES, tm), lambda bi, i: (0, i)), pl.BlockSpec((LANES, tm), lambda bi, i: (0, i)),
                full(wka), pl.BlockSpec((tm, LANES), lambda bi, i: (i, 0)), pl.BlockSpec((tm, LANES), lambda bi, i: (i, 0)),
                full(wva), full(wqb), full(wkb), full(wvb)]
    nb = B_HEADS * HEAD_DIM

    def nat(n):
        return jax.ShapeDtypeStruct((b, t, n), BF16), pl.BlockSpec((1, tm, n), lambda bi, i: (bi, i, 0))

    def tr(n):
        return jax.ShapeDtypeStruct((b, n, t), BF16), pl.BlockSpec((1, n, tm), lambda bi, i: (bi, 0, i))

    outs = [tr(A_HEADS * LANES), nat(LANES), tr(A_KV_HEADS * V_ROWS), tr(B_HEADS * LANES), nat(nb), tr(B_HEADS * V_ROWS)]
    return pl.pallas_call(
        _l0_proj_body, grid=(b, t // tm), in_specs=in_specs,
        out_specs=[o[1] for o in outs], out_shape=[o[0] for o in outs], name=name,
        compiler_params=_cparams(("parallel", "parallel")),
    )(x, mod, wqa, cqa, sqa, wka, cka, ska, wva, wqb, wkb, wvb)


def _l1_proj_body(*refs, with_q):
    it = iter(refs)
    x_ref, mod_ref, wdn_ref, gq_ref, gkv_ref = (next(it) for _ in range(5))
    wq_ref, cq_ref, sq_ref = (next(it) for _ in range(3)) if with_q else (None, None, None)
    wk_ref, ck_ref, sk_ref, wv_ref = (next(it) for _ in range(4))
    q_ref = next(it) if with_q else None
    k_ref, v_ref = next(it), next(it)

    sc = mod_ref[0, SC1:SC1 + 1, :]
    sh = mod_ref[0, SH1:SH1 + 1, :]
    xs = (x_ref[0] * (1.0 + sc) + sh).astype(BF16)

    def rms(v, g_ref):
        ms = jnp.mean(v * v, axis=-1, keepdims=True)
        return (v * lax.rsqrt(ms + EPS) * g_ref[...]).astype(BF16)

    n_q, n_kv = MLA_Q_LORA, MLA_KV_LORA
    c_kv = rms(_dot(xs, wdn_ref[:, n_q:n_q + n_kv]), gkv_ref)
    k_r = _rotate(_dot(xs, wdn_ref[:, n_q + n_kv:]), ck_ref[...], sk_ref[...], 1)

    if with_q:
        c_q = rms(_dot(xs, wdn_ref[:, :n_q]), gq_ref)
        cq = jnp.concatenate([cq_ref[...]] * (ROW_CHUNK // LANES), axis=0)
        sq = jnp.concatenate([sq_ref[...]] * (ROW_CHUNK // LANES), axis=0)
        for r in range(MLA_HEADS * LANES // ROW_CHUNK):
            rows = slice(r * ROW_CHUNK, (r + 1) * ROW_CHUNK)
            q_ref[0, rows, :] = _rotate(_dot_nt(wq_ref[rows, :], c_q), cq, sq, 0).astype(q_ref.dtype)

    k_r_tiled = jnp.concatenate([k_r] * (ROW_CHUNK // LANES), axis=-1)
    for j in range(MLA_HEADS * LANES // ROW_CHUNK):
        cols = slice(j * ROW_CHUNK, (j + 1) * ROW_CHUNK)
        k_ref[0, :, cols] = (_dot(c_kv, wk_ref[:, cols]) + k_r_tiled).astype(k_ref.dtype)

    _store_vt(v_ref, _dot_nt(wv_ref[...], c_kv), MLA_HEADS)


def _l1_proj(x, mod, mod_row, weights, gains, q_tables, k_tables, *, tm, with_q, name):
    b, t, d = x.shape
    wdn, wq, wk, wv = weights
    mod_map = (lambda bi, i: (bi, 0, 0)) if mod_row is None else (lambda bi, i: (mod_row, 0, 0))

    def full(a):
        return pl.BlockSpec(a.shape, lambda bi, i: (0,) * a.ndim)

    in_specs = [pl.BlockSpec((1, tm, d), lambda bi, i: (bi, i, 0)), pl.BlockSpec((1, N_MOD, d), mod_map),
                full(wdn), full(gains[0]), full(gains[1])]
    args = [x, mod, wdn, gains[0], gains[1]]
    if with_q:
        in_specs += [full(wq), pl.BlockSpec((LANES, tm), lambda bi, i: (0, i)), pl.BlockSpec((LANES, tm), lambda bi, i: (0, i))]
        args += [wq, q_tables[0], q_tables[1]]
    in_specs += [full(wk), pl.BlockSpec((tm, LANES), lambda bi, i: (i, 0)), pl.BlockSpec((tm, LANES), lambda bi, i: (i, 0)), full(wv)]
    args += [wk, k_tables[0], k_tables[1], wv]
    out_shape, out_specs = [], []
    if with_q:
        out_shape.append(jax.ShapeDtypeStruct((b, MLA_HEADS * LANES, t), BF16))
        out_specs.append(pl.BlockSpec((1, MLA_HEADS * LANES, tm), lambda bi, i: (bi, 0, i)))
    out_shape += [jax.ShapeDtypeStruct((b, t, MLA_HEADS * LANES), BF16), jax.ShapeDtypeStruct((b, MLA_HEADS * V_ROWS, t), BF16)]
    out_specs += [pl.BlockSpec((1, tm, MLA_HEADS * LANES), lambda bi, i: (bi, i, 0)),
                  pl.BlockSpec((1, MLA_HEADS * V_ROWS, tm), lambda bi, i: (bi, 0, i))]
    return pl.pallas_call(
        functools.partial(_l1_proj_body, with_q=with_q), grid=(b, t // tm), in_specs=in_specs,
        out_specs=out_specs, out_shape=out_shape, name=name,
        compiler_params=_cparams(("parallel", "parallel")),
    )(*args)


def _dense_body(*refs, heads, k_slot, v_row, nk, has_ctx):
    if has_ctx:
        qt_ref, k_ref, vt_ref, kc_ref, vct_ref, out_ref, m_ref, l_ref, acc_ref = refs
    else:
        qt_ref, k_ref, vt_ref, out_ref, m_ref, l_ref, acc_ref = refs
    ki = pl.program_id(2)
    last = nk if has_ctx else nk - 1

    @pl.when(ki == 0)
    def _():
        m_ref[...] = jnp.full(m_ref.shape, MASK_VALUE, F32)
        l_ref[...] = jnp.zeros(l_ref.shape, F32)
        acc_ref[...] = jnp.zeros(acc_ref.shape, F32)

    def step(kr, vr):
        def scores(h):
            ks = k_slot[h]
            return _dot(kr[0, :, ks * LANES:(ks + 1) * LANES], qt_ref[0, h * LANES:(h + 1) * LANES, :])

        s_next = scores(0)
        for h in range(heads):
            s = s_next
            if h + 1 < heads:
                s_next = scores(h + 1)
            rows = slice(h * HEAD_DIM, (h + 1) * HEAD_DIM)
            m_prev = m_ref[h:h + 1, :]
            m_new = jnp.maximum(m_prev, jnp.max(s, axis=0, keepdims=True))
            alpha = jnp.exp2(m_prev - m_new)
            p = jnp.exp2(s - m_new).astype(BF16)
            pv = _dot(vr[0, v_row[h]:v_row[h] + V_ROWS, :], p)
            m_ref[h:h + 1, :] = m_new
            l_ref[h:h + 1, :] = alpha * l_ref[h:h + 1, :] + pv[HEAD_DIM:HEAD_DIM + 1]
            acc_ref[rows, :] = alpha * acc_ref[rows, :] + pv[:HEAD_DIM]

    if has_ctx:
        @pl.when(ki < nk)
        def _():
            step(k_ref, vt_ref)

        @pl.when(ki == nk)
        def _():
            step(kc_ref, vct_ref)
    else:
        step(k_ref, vt_ref)

    @pl.when(ki == last)
    def _():
        for h in range(heads):
            rows = slice(h * HEAD_DIM, (h + 1) * HEAD_DIM)
            acc_ref[rows, :] = acc_ref[rows, :] * (1.0 / l_ref[h:h + 1, :])
        out_ref[0] = acc_ref[...].T.astype(out_ref.dtype)


def _dense_attn(qt, k, vt, kc=None, vct=None, *, heads, k_slot, v_row, tq, tk, name):
    b, nq, lq = qt.shape
    _, lk, nkw = k.shape
    nv = vt.shape[1]
    assert nq == heads * LANES and lq % tq == 0 and lk % tk == 0
    nk = lk // tk
    has_ctx = kc is not None
    steps = nk + (1 if has_ctx else 0)
    in_specs = [
        pl.BlockSpec((1, nq, tq), lambda bi, qi, ki: (bi, 0, qi)),
        pl.BlockSpec((1, tk, nkw), lambda bi, qi, ki: (bi, jnp.minimum(ki, nk - 1), 0)),
        pl.BlockSpec((1, nv, tk), lambda bi, qi, ki: (bi, 0, jnp.minimum(ki, nk - 1))),
    ]
    args = [qt, k, vt]
    if has_ctx:
        lc = kc.shape[1]
        in_specs += [pl.BlockSpec((1, lc, nkw), lambda bi, qi, ki: (bi, 0, 0)),
                     pl.BlockSpec((1, nv, lc), lambda bi, qi, ki: (bi, 0, 0))]
        args += [kc, vct]
    body = functools.partial(_dense_body, heads=heads, k_slot=tuple(k_slot), v_row=tuple(v_row),
                             nk=nk, has_ctx=has_ctx)
    return pl.pallas_call(
        body, grid=(b, lq // tq, steps), in_specs=in_specs,
        out_specs=pl.BlockSpec((1, tq, heads * HEAD_DIM), lambda bi, qi, ki: (bi, qi, 0)),
        out_shape=jax.ShapeDtypeStruct((b, lq, heads * HEAD_DIM), BF16),
        scratch_shapes=[pltpu.VMEM((heads, tq), F32), pltpu.VMEM((heads, tq), F32),
                        pltpu.VMEM((heads * HEAD_DIM, tq), F32)],
        name=name, compiler_params=_cparams(("parallel", "parallel", "arbitrary")),
    )(*args)


NA_QROWS = 4
NA_TQ = NA_QROWS * GRID_W
NA_WROWS = 12
NA_WBLOCKS = NA_WROWS // NA_QROWS
NA_STEPS = GRID_H // NA_QROWS


def _natten_body(qt_ref, k0_ref, k1_ref, k2_ref, v0_ref, v1_ref, v2_ref, kc_ref, vct_ref, bias_ref,
                 out_ref, o_ref):
    k_refs = (k0_ref, k1_ref, k2_ref)
    v_refs = (v0_ref, v1_ref, v2_ref)

    def scores(h):
        lanes = slice((h // 2) * LANES, (h // 2 + 1) * LANES)
        qq = qt_ref[0, h * LANES:(h + 1) * LANES, :]
        return [_dot(kc_ref[0, :, lanes], qq)] + [_dot(k_refs[blk][0, :, lanes], qq) for blk in range(NA_WBLOCKS)]

    s_next = scores(0)
    for h in range(B_HEADS):
        s_all = s_next
        if h + 1 < B_HEADS:
            s_next = scores(h + 1)
        vrows = slice(h * V_ROWS, (h + 1) * V_ROWS)
        s_all = [s_all[0]] + [s_all[1 + blk] + bias_ref[0, h, blk * NA_TQ:(blk + 1) * NA_TQ, :]
                              for blk in range(NA_WBLOCKS)]
        m = s_all[0].max(axis=0, keepdims=True)
        for s in s_all[1:]:
            m = jnp.maximum(m, s.max(axis=0, keepdims=True))
        pv = _dot(vct_ref[0, vrows, :], jnp.exp2(s_all[0] - m).astype(BF16))
        for blk in range(NA_WBLOCKS):
            pv = pv + _dot(v_refs[blk][0, vrows, :], jnp.exp2(s_all[1 + blk] - m).astype(BF16))
        o_ref[h * HEAD_DIM:(h + 1) * HEAD_DIM, :] = pv[:HEAD_DIM] * (1.0 / pv[HEAD_DIM:HEAD_DIM + 1])
    out_ref[0] = o_ref[...].T.astype(out_ref.dtype)


def _natten(qt, k, vt, kc, vct, bias):
    b = qt.shape[0]
    nw = B_HEADS * HEAD_DIM
    nv = B_HEADS * V_ROWS

    def wstart(i):
        return jnp.clip(i - 1, 0, NA_STEPS - NA_WBLOCKS)

    def cls(i):
        return jnp.where(i == 0, 0, jnp.where(i == NA_STEPS - 1, 2, 1))

    in_specs = [pl.BlockSpec((1, B_HEADS * LANES, NA_TQ), lambda bi, i: (bi, 0, i))]
    in_specs += [pl.BlockSpec((1, NA_TQ, nw), functools.partial(lambda bi, i, j: (bi, wstart(i) + j, 0), j=j))
                 for j in range(NA_WBLOCKS)]
    in_specs += [pl.BlockSpec((1, nv, NA_TQ), functools.partial(lambda bi, i, j: (bi, 0, wstart(i) + j), j=j))
                 for j in range(NA_WBLOCKS)]
    in_specs += [pl.BlockSpec((1, CTX_LEN, nw), lambda bi, i: (bi, 0, 0)),
                 pl.BlockSpec((1, nv, CTX_LEN), lambda bi, i: (bi, 0, 0)),
                 pl.BlockSpec((1, B_HEADS, NA_WROWS * GRID_W, NA_TQ), lambda bi, i: (cls(i), 0, 0, 0))]
    return pl.pallas_call(
        _natten_body, grid=(b, NA_STEPS), in_specs=in_specs,
        out_specs=pl.BlockSpec((1, NA_TQ, nw), lambda bi, i: (bi, i, 0)),
        out_shape=jax.ShapeDtypeStruct((b, SEQ, nw), BF16),
        scratch_shapes=[pltpu.VMEM((nw, NA_TQ), F32)],
        name="natten", compiler_params=_cparams(("parallel", "arbitrary")),
    )(qt, k, k, k, vt, vt, vt, kc, vct, bias)


def _natten_bias(rpb):
    v = rpb.astype(F32) * LOG2E
    h, nr, _ = v.shape
    period = 2 * GRID_W
    row = jnp.zeros((h, nr, period), F32)
    row = row.at[..., :NA_COLS].set(v[..., NA_COLS - 1::-1])
    row = row.at[..., period - (NA_COLS - 1):].set(v[..., :NA_COLS - 1:-1])
    toep = jnp.tile(row, (1, 1, GRID_W))[..., :GRID_W * (period - 1)]
    toep = toep.reshape(h, nr, GRID_W, period - 1)[..., :GRID_W]
    col = np.arange(GRID_W)
    cs = np.clip(col - NA_COLS // 2, 0, GRID_W - NA_COLS)
    col_ok = (col[:, None] >= cs[None, :]) & (col[:, None] < cs[None, :] + NA_COLS)
    masked = 2.0 * MASK_VALUE
    toep = jnp.where(jnp.asarray(col_ok), toep, masked)
    dead = jnp.full((h, GRID_W, GRID_W), masked, F32)
    tables = []
    for r0 in (0, 2 * NA_QROWS, GRID_H - NA_QROWS):
        ws = min(max(r0 - NA_QROWS, 0), GRID_H - NA_WROWS)
        key_rows = []
        for ki in range(NA_WROWS):
            blocks = []
            for qi in range(NA_QROWS):
                k_row, q_row = ws + ki, r0 + qi
                rs = min(max(q_row - NA_ROWS // 2, 0), GRID_H - NA_ROWS)
                blocks.append(toep[:, k_row - q_row + NA_ROWS - 1] if rs <= k_row < rs + NA_ROWS else dead)
            key_rows.append(jnp.concatenate(blocks, axis=-1))
        tables.append(jnp.concatenate(key_rows, axis=-2))
    return jnp.stack(tables)


def _ffn_body(x_ref, xp_ref, xn_ref, mod_ref, wg_ref, wv_ref, wd_ref, cw_ref, cb_ref, lng_ref, lnb_ref,
              out_ref, hs_ref, g_ref, acc_ref, *, tm, rc, n_tiles):
    i = pl.program_id(1)
    f = pl.program_id(2)
    nf = pl.num_programs(2)

    @pl.when(f == 0)
    def _():
        sc = mod_ref[0, SC2:SC2 + 1, :]
        sh = mod_ref[0, SH2:SH2 + 1, :]

        def modulate(v):
            return v * (1.0 + sc) + sh

        hp = jnp.where(i > 0, modulate(xp_ref[0]), 0.0)
        hn = jnp.where(i < n_tiles - 1, modulate(xn_ref[0]), 0.0)
        hs_ref[0:FFN_HALO, :] = hp.astype(BF16)
        hs_ref[FFN_HALO:FFN_HALO + tm, :] = modulate(x_ref[0]).astype(BF16)
        hs_ref[FFN_HALO + tm:, :] = hn.astype(BF16)
        acc_ref[...] = jnp.zeros(acc_ref.shape, F32)

    for c in range(tm // rc):
        lo = c * rc
        g_ref[c] = _dot(hs_ref[lo:lo + rc + 2 * FFN_HALO, :], wg_ref[...])
        val = _dot(hs_ref[FFN_HALO + lo:FFN_HALO + lo + rc, :], wv_ref[...])
        gate = (cw_ref[0:1, :] * g_ref[c, FFN_HALO - 1:FFN_HALO - 1 + rc, :]
                + cw_ref[1:2, :] * g_ref[c, FFN_HALO:FFN_HALO + rc, :]
                + cw_ref[2:3, :] * g_ref[c, FFN_HALO + 1:FFN_HALO + 1 + rc, :]
                + cb_ref[...])
        act = gate * jax.nn.sigmoid(gate) * val
        acc_ref[lo:lo + rc, :] += _dot(act.astype(BF16), wd_ref[...])

    @pl.when(f == nf - 1)
    def _():
        g = mod_ref[0, G2:G2 + 1, :]
        out_ref[0] = _layer_norm(DEEPNORM_ALPHA * x_ref[0] + g * acc_ref[...], lng_ref[...], lnb_ref[...])


def _ffn(x, mod, mod_row, wg, wv, wd, conv_w, conv_b, ln_g, ln_b, *, tm, name):
    b, t, d = x.shape
    n_tiles = t // tm
    rc = min(FFN_ROWS, tm)
    hb = tm // FFN_HALO
    n_hblocks = t // FFN_HALO
    mod_map = (lambda bi, i, f: (bi, 0, 0)) if mod_row is None else (lambda bi, i, f: (mod_row, 0, 0))
    in_specs = [
        pl.BlockSpec((1, tm, d), lambda bi, i, f: (bi, i, 0)),
        pl.BlockSpec((1, FFN_HALO, d), lambda bi, i, f: (bi, jnp.maximum(i * hb - 1, 0), 0)),
        pl.BlockSpec((1, FFN_HALO, d), lambda bi, i, f: (bi, jnp.minimum((i + 1) * hb, n_hblocks - 1), 0)),
        pl.BlockSpec((1, N_MOD, d), mod_map),
        pl.BlockSpec((d, TF), lambda bi, i, f: (0, f)),
        pl.BlockSpec((d, TF), lambda bi, i, f: (0, f)),
        pl.BlockSpec((TF, d), lambda bi, i, f: (f, 0)),
        pl.BlockSpec((3, TF), lambda bi, i, f: (0, f)),
        pl.BlockSpec((1, TF), lambda bi, i, f: (0, f)),
        pl.BlockSpec((1, d), lambda bi, i, f: (0, 0)),
        pl.BlockSpec((1, d), lambda bi, i, f: (0, 0)),
    ]
    body = functools.partial(_ffn_body, tm=tm, rc=rc, n_tiles=n_tiles)
    return pl.pallas_call(
        body, grid=(b, n_tiles, D_FF // TF), in_specs=in_specs,
        out_specs=pl.BlockSpec((1, tm, d), lambda bi, i, f: (bi, i, 0)),
        out_shape=jax.ShapeDtypeStruct((b, t, d), F32),
        scratch_shapes=[pltpu.VMEM((tm + 2 * FFN_HALO, d), BF16),
                        pltpu.VMEM((tm // rc, rc + 2 * FFN_HALO, TF), F32),
                        pltpu.VMEM((tm, d), F32)],
        name=name, compiler_params=_cparams(("parallel", "parallel", "arbitrary")),
    )(x, x, x, mod, wg, wv, wd, conv_w.astype(F32), conv_b.reshape(1, D_FF).astype(F32),
      ln_g.reshape(1, d).astype(F32), ln_b.reshape(1, d).astype(F32))


def _rope_cos_sin(rot_dim):
    pos = jnp.arange(SEQ, dtype=jnp.int32)
    rows = (pos // GRID_W).astype(F32)
    cols = (pos % GRID_W).astype(F32)
    axis_dim = rot_dim // 2
    inv = ROPE_THETA ** (-jnp.arange(0, axis_dim, 2, dtype=F32) / axis_dim)
    ang = jnp.concatenate([rows[:, None] * inv, cols[:, None] * inv], axis=-1)
    cos = jnp.repeat(jnp.cos(ang), 2, axis=-1)
    sin = jnp.repeat(jnp.sin(ang), 2, axis=-1) * jnp.tile(jnp.array([-1.0, 1.0], F32), rot_dim // 2)
    return cos, sin


def _swap_pairs(g):
    return g.reshape(-1, 2)[:, ::-1].reshape(-1)


def _gqa_tables(gain, scale, length, rope):
    if rope:
        cos, sin = _rope_cos_sin(HEAD_DIM)
    else:
        cos, sin = jnp.ones((length, HEAD_DIM), F32), jnp.zeros((length, HEAD_DIM), F32)
    c = cos * (gain * scale)
    s = sin * (_swap_pairs(gain) * scale)
    return jnp.concatenate([c, c], axis=-1), jnp.concatenate([s, s], axis=-1)


def _mla_tables(scale, length, rope):
    c = jnp.full((length, LANES), scale, F32)
    s = jnp.zeros((length, LANES), F32)
    if rope:
        cos, sin = _rope_cos_sin(MLA_ROPE_DIM)
        lo, hi = MLA_NOPE_DIM, MLA_NOPE_DIM + MLA_ROPE_DIM
        c = c.at[:, lo:hi].set(cos * scale)
        s = s.at[:, lo:hi].set(sin * scale)
    return c, s


def _slot_rows(w, heads, width, half_of):
    k = w.shape[0]
    out = jnp.zeros((heads * LANES, k), w.dtype)
    for h in range(heads):
        off = h * LANES + half_of(h) * HALF
        out = out.at[off:off + width].set(w[:, h * width:(h + 1) * width].T)
    return out


def _layer0_mixer(x, xc, mod, w_in, q_gain, k_gain, rpb):
    w = w_in.astype(BF16)
    aq, akv, bw = A_HEADS * HEAD_DIM, A_KV_HEADS * HEAD_DIM, B_HEADS * HEAD_DIM
    o = 0
    wqa = _slot_rows(w[:, o:o + aq], A_HEADS, HEAD_DIM, lambda h: h // A_GROUP); o += aq
    wka = w[:, o:o + akv]; o += akv
    wva = w[:, o:o + akv].T; o += akv
    wqb = _slot_rows(w[:, o:o + bw], B_HEADS, HEAD_DIM, lambda h: h % 2); o += bw
    wkb = w[:, o:o + bw]; o += bw
    wvb = w[:, o:o + bw].T
    weights = (wqa, wka, wva, wqb, wkb, wvb)
    qscale = HEAD_SCALE * LOG2E

    def tables(length, rope):
        cq, sq = _gqa_tables(q_gain, qscale, length, rope)
        ck, sk = _gqa_tables(k_gain, 1.0, length, rope)
        return cq.T, sq.T, ck, sk

    qa_t, ka, va_t, qb_t, kb, vb_t = _l0_proj(x, mod, None, weights, tables(SEQ, True), tm=TM, name="l0_proj")
    qac_t, kac, vac_t, qbc_t, kbc, vbc_t = _l0_proj(xc, mod, CTX_MOD_ROW, weights, tables(CTX_LEN, False),
                                                    tm=TMC, name="l0_proj_ctx")

    a_kslot = [0] * A_HEADS
    a_vrow = [(h // A_GROUP) * V_ROWS for h in range(A_HEADS)]
    b_kslot = [h // 2 for h in range(B_HEADS)]
    b_vrow = [h * V_ROWS for h in range(B_HEADS)]
    ya = _dense_attn(qa_t, ka, va_t, kac, vac_t, heads=A_HEADS, k_slot=a_kslot, v_row=a_vrow,
                     tq=TQ, tk=TK, name="gqa")
    yb = _natten(qb_t, kb, vb_t, kbc, vbc_t, _natten_bias(rpb))
    yac = _dense_attn(qac_t, kac, vac_t, heads=A_HEADS, k_slot=a_kslot, v_row=a_vrow,
                      tq=CTX_LEN, tk=CTX_LEN, name="gqa_ctx")
    ybc = _dense_attn(qbc_t, kbc, vbc_t, heads=B_HEADS, k_slot=b_kslot, v_row=b_vrow,
                      tq=CTX_LEN, tk=CTX_LEN, name="nbr_ctx")
    return jnp.concatenate([ya, yb], axis=-1), jnp.concatenate([yac, ybc], axis=-1)


def _layer1_mixer(x, xc, mod, w_in, cq_gain, ckv_gain, w_uq, w_ukv):
    qk_dim = MLA_NOPE_DIM + MLA_ROPE_DIM
    kv_dim = MLA_NOPE_DIM + MLA_V_DIM
    n_in = MLA_Q_LORA + MLA_KV_LORA
    wdn = jnp.zeros((D_MODEL, n_in + LANES), BF16)
    wdn = wdn.at[:, :n_in].set(w_in[:, :n_in].astype(BF16))
    wdn = wdn.at[:, n_in + MLA_NOPE_DIM:n_in + qk_dim].set(w_in[:, n_in:].astype(BF16))
    wq = _slot_rows(w_uq.astype(BF16), MLA_HEADS, qk_dim, lambda h: 0)
    wkv = w_ukv.astype(BF16).reshape(MLA_KV_LORA, MLA_HEADS, kv_dim)
    wk = _slot_rows(wkv[:, :, :MLA_NOPE_DIM].reshape(MLA_KV_LORA, -1), MLA_HEADS, MLA_NOPE_DIM, lambda h: 0).T
    wv = wkv[:, :, MLA_NOPE_DIM:].reshape(MLA_KV_LORA, MLA_HEADS * MLA_V_DIM).T
    weights = (wdn, wq, wk, wv)
    gains = (cq_gain.reshape(1, -1).astype(F32), ckv_gain.reshape(1, -1).astype(F32))

    cq, sq = _mla_tables(MLA_SCALE * LOG2E, SEQ, True)
    qt, kk, vt = _l1_proj(x, mod, None, weights, gains, (cq.T, sq.T), _mla_tables(1.0, SEQ, True),
                          tm=TM, with_q=True, name="l1_proj")
    kkc, vtc = _l1_proj(xc, mod, CTX_MOD_ROW, weights, gains, None, _mla_tables(1.0, CTX_LEN, False),
                        tm=TMC, with_q=False, name="l1_proj_ctx")
    return _dense_attn(qt, kk, vt, kkc, vtc, heads=MLA_HEADS, k_slot=list(range(MLA_HEADS)),
                       v_row=[h * V_ROWS for h in range(MLA_HEADS)], tq=TQ, tk=TK, name="mla")


def _post_mixer(x, y, mod, mod_row, w_out, ln1, w_up, conv_w, conv_b, w_down, ln2, tm, tag):
    x = _proj(y, w_out.astype(BF16), name="out_ln" + tag, tm=tm, tn=D_MODEL, mod=mod, mod_row=mod_row,
              gate_row=G1, resid=x, ln=ln1)
    wup = w_up.astype(BF16)
    return _ffn(x, mod, mod_row, wup[:, :D_FF], wup[:, D_FF:], w_down.astype(BF16), conv_w, conv_b,
                ln2[0], ln2[1], tm=tm, name="ffn" + tag)


def kernel(x, c, ctx, c_ctx, l0_w_ada, l0_b_ada, l0_w_in, l0_q_gain, l0_k_gain, l0_rpb, l0_w_out, l0_ln1_g, l0_ln1_b, l0_w_up, l0_conv_w, l0_conv_b, l0_w_down, l0_ln2_g, l0_ln2_b, l1_w_ada, l1_b_ada, l1_w_in, l1_cq_gain, l1_ckv_gain, l1_w_uq, l1_w_ukv, l1_w_out, l1_ln1_g, l1_ln1_b, l1_w_up, l1_conv_w, l1_conv_b, l1_w_down, l1_ln2_g, l1_ln2_b):
    xc = ctx
    mod = _modvec(c, c_ctx, l0_w_ada, l0_b_ada)
    y, yc = _layer0_mixer(x, xc, mod, l0_w_in, l0_q_gain, l0_k_gain, l0_rpb)
    post0 = (l0_w_out, (l0_ln1_g, l0_ln1_b), l0_w_up, l0_conv_w, l0_conv_b, l0_w_down, (l0_ln2_g, l0_ln2_b))
    x = _post_mixer(x, y, mod, None, *post0, tm=TM, tag="0")
    xc = _post_mixer(xc, yc, mod, CTX_MOD_ROW, *post0, tm=TMC, tag="0_ctx")
    mod = _modvec(c, c_ctx, l1_w_ada, l1_b_ada)
    y = _layer1_mixer(x, xc, mod, l1_w_in, l1_cq_gain, l1_ckv_gain, l1_w_uq, l1_w_ukv)
    post1 = (l1_w_out, (l1_ln1_g, l1_ln1_b), l1_w_up, l1_conv_w, l1_conv_b, l1_w_down, (l1_ln2_g, l1_ln2_b))
    return _post_mixer(x, y, mod, None, *post1, tm=TM, tag="1")
```

```python
import functools
import math

import numpy as np
import jax
import jax.numpy as jnp
from jax import lax
from jax.experimental import pallas as pl
from jax.experimental.pallas import tpu as pltpu

D_MODEL = 1024
BATCH = 4
SEQ = 4096
DEPTH = 2
GRID_W = 64
GRID_H = SEQ // GRID_W
CTX_LEN = 256
HEAD_DIM = 64
A_HEADS = 8
A_KV_HEADS = 2
A_GROUP = A_HEADS // A_KV_HEADS
B_HEADS = 8
NA_ROWS = 8
NA_COLS = 16
ROPE_THETA = 10000.0
MLA_HEADS = 16
MLA_Q_LORA = 768
MLA_KV_LORA = 256
MLA_NOPE_DIM = 64
MLA_ROPE_DIM = 32
MLA_V_DIM = 64
D_FF = 2816
N_MOD = 6
EPS = 1e-6
DEEPNORM_ALPHA = (2 * DEPTH) ** 0.25
HEAD_SCALE = HEAD_DIM ** -0.5
MLA_SCALE = (MLA_NOPE_DIM + MLA_ROPE_DIM) ** -0.5
LOG2E = math.log2(math.e)

LANES = 128
HALF = LANES // 2
BF16_ROWS = 16
V_ROWS = HEAD_DIM + BF16_ROWS
MASK_VALUE = -1e30
VMEM_LIMIT = 56 * 1024 * 1024

SH1, SC1, G1, SH2, SC2, G2 = range(N_MOD)

F32 = jnp.float32
BF16 = jnp.bfloat16

MOD_ROWS = 16
CTX_MOD_ROW = BATCH
TM = 512
TMC = CTX_LEN
TQ = 512
TK = 512
AHEAD = 2
FFN_ROWS = 256
FFN_HALO = BF16_ROWS
ROW_CHUNK = 512


def _cparams(sem):
    return pltpu.CompilerParams(dimension_semantics=sem, vmem_limit_bytes=VMEM_LIMIT)


def _dot(a, b):
    return jnp.dot(a, b, preferred_element_type=F32)


def _dot_nt(a, b):
    return lax.dot_general(a, b, (((1,), (1,)), ((), ())), preferred_element_type=F32)


def _pair_swap(y, axis):
    n = y.shape[axis]
    idx = lax.broadcasted_iota(jnp.int32, y.shape, axis)
    nxt = pltpu.roll(y, n - 1, axis=axis)
    prv = pltpu.roll(y, 1, axis=axis)
    return jnp.where(idx % 2 == 0, nxt, prv)


def _rotate(y, c, s, axis):
    return y * c + _pair_swap(y, axis) * s


def _layer_norm(z, g, b):
    mu = jnp.mean(z, axis=-1, keepdims=True)
    zc = z - mu
    var = jnp.mean(zc * zc, axis=-1, keepdims=True)
    return zc * lax.rsqrt(var + EPS) * g + b


def _store_vt(out_ref, yt, heads):
    ones = jnp.ones((BF16_ROWS, yt.shape[1]), out_ref.dtype)
    for h in range(heads):
        out_ref[0, h * V_ROWS:h * V_ROWS + HEAD_DIM, :] = yt[h * HEAD_DIM:(h + 1) * HEAD_DIM].astype(out_ref.dtype)
        out_ref[0, h * V_ROWS + HEAD_DIM:(h + 1) * V_ROWS, :] = ones


def _proj_body(*refs, silu, bias, resid_ln, gate_row, two):
    it = iter(refs)
    x_ref = next(it)
    w_ref = next(it)
    x2_ref = next(it) if two else None
    w2_ref = next(it) if two else None
    bias_ref = next(it) if bias else None
    mod_ref = next(it) if resid_ln else None
    xres_ref = next(it) if resid_ln else None
    lng_ref = next(it) if resid_ln else None
    lnb_ref = next(it) if resid_ln else None
    out_ref = next(it)
    x = x_ref[0]
    if silu:
        x = x * jax.nn.sigmoid(x)
    y = _dot(x.astype(BF16), w_ref[...])
    if two:
        y = y + _dot(x2_ref[0], w2_ref[...])
    if bias:
        y = y + bias_ref[...]
    if resid_ln:
        g = mod_ref[0, gate_row:gate_row + 1, :]
        y = _layer_norm(DEEPNORM_ALPHA * xres_ref[0] + g * y, lng_ref[...], lnb_ref[...])
    out_ref[0] = y.astype(out_ref.dtype)


def _proj(x, w, *, name, tm, tn, x2=None, w2=None, silu=False, bias=None, mod=None, mod_row=None,
          gate_row=None, resid=None, ln=None, out_dtype=F32):
    b, t, k = x.shape
    n = w.shape[1]
    assert t % tm == 0 and n % tn == 0
    resid_ln = resid is not None
    in_specs = [pl.BlockSpec((1, tm, k), lambda bi, i, j: (bi, i, 0)),
                pl.BlockSpec((k, tn), lambda bi, i, j: (0, j))]
    args = [x, w]
    if x2 is not None:
        k2 = x2.shape[2]
        in_specs += [pl.BlockSpec((1, tm, k2), lambda bi, i, j: (bi, i, 0)),
                     pl.BlockSpec((k2, tn), lambda bi, i, j: (0, j))]
        args += [x2, w2]
    if bias is not None:
        in_specs.append(pl.BlockSpec((1, tn), lambda bi, i, j: (0, j)))
        args.append(bias.reshape(1, n).astype(F32))
    if resid_ln:
        mod_map = (lambda bi, i, j: (bi, 0, 0)) if mod_row is None else (lambda bi, i, j: (mod_row, 0, 0))
        in_specs += [pl.BlockSpec((1, N_MOD, D_MODEL), mod_map),
                     pl.BlockSpec((1, tm, n), lambda bi, i, j: (bi, i, 0)),
                     pl.BlockSpec((1, n), lambda bi, i, j: (0, 0)),
                     pl.BlockSpec((1, n), lambda bi, i, j: (0, 0))]
        args += [mod, resid, ln[0].reshape(1, n).astype(F32), ln[1].reshape(1, n).astype(F32)]
    body = functools.partial(_proj_body, silu=silu, bias=bias is not None, resid_ln=resid_ln, gate_row=gate_row,
                             two=x2 is not None)
    return pl.pallas_call(
        body, grid=(b, t // tm, n // tn), in_specs=in_specs,
        out_specs=pl.BlockSpec((1, tm, tn), lambda bi, i, j: (bi, i, j)),
        out_shape=jax.ShapeDtypeStruct((b, t, n), out_dtype), name=name,
        compiler_params=_cparams(("parallel", "parallel", "arbitrary")),
    )(*args)


def _modvec(c, c_ctx, w_ada, b_ada):
    cond = jnp.zeros((1, MOD_ROWS, D_MODEL), F32).at[0, :BATCH].set(c).at[0, BATCH].set(c_ctx)
    m = _proj(cond, w_ada.astype(BF16), name="adaln", tm=MOD_ROWS, tn=1024, silu=True, bias=b_ada)
    return m.reshape(MOD_ROWS, N_MOD, D_MODEL)


def _l0_proj_body(x_ref, mod_ref, wqa_ref, cqa_ref, sqa_ref, wka_ref, cka_ref, ska_ref, wva_ref, wqb_ref,
                  wkb_ref, wvb_ref, qa_ref, ka_ref, va_ref, qb_ref, kb_ref, vb_ref):
    sc = mod_ref[0, SC1:SC1 + 1, :]
    sh = mod_ref[0, SH1:SH1 + 1, :]
    xs = (x_ref[0] * (1.0 + sc) + sh).astype(BF16)

    cq, sq = cqa_ref[...], sqa_ref[...]
    for r in range(A_HEADS * LANES // ROW_CHUNK):
        yt = _dot_nt(wqa_ref[r * ROW_CHUNK:(r + 1) * ROW_CHUNK, :], xs)
        for s in range(ROW_CHUNK // LANES):
            ys = yt[s * LANES:(s + 1) * LANES]
            rot = _rotate(ys, cq, sq, 0)
            parts = []
            for half in range(2):
                seg = ys[half * HALF:(half + 1) * HALF]
                ms = jnp.mean(seg * seg, axis=0, keepdims=True)
                parts.append(rot[half * HALF:(half + 1) * HALF] * lax.rsqrt(ms + EPS))
            row0 = r * ROW_CHUNK + s * LANES
            qa_ref[0, row0:row0 + LANES, :] = jnp.concatenate(parts, axis=0).astype(qa_ref.dtype)

    y = _dot(xs, wka_ref[...])
    lo = lax.broadcasted_iota(jnp.int32, y.shape, 1) < HALF
    ysq = y * y
    ms_lo = jnp.sum(jnp.where(lo, ysq, 0.0), axis=-1, keepdims=True) * (1.0 / HEAD_DIM)
    ms_hi = jnp.sum(jnp.where(lo, 0.0, ysq), axis=-1, keepdims=True) * (1.0 / HEAD_DIM)
    rn = jnp.where(lo, lax.rsqrt(ms_lo + EPS), lax.rsqrt(ms_hi + EPS))
    ka_ref[0] = (_rotate(y, cka_ref[...], ska_ref[...], 1) * rn).astype(ka_ref.dtype)

    _store_vt(va_ref, _dot_nt(wva_ref[...], xs), A_KV_HEADS)

    for r in range(B_HEADS * LANES // ROW_CHUNK):
        rows = slice(r * ROW_CHUNK, (r + 1) * ROW_CHUNK)
        qb_ref[0, rows, :] = (_dot_nt(wqb_ref[rows, :], xs) * (HEAD_SCALE * LOG2E)).astype(qb_ref.dtype)
    kb_ref[0] = _dot(xs, wkb_ref[...]).astype(kb_ref.dtype)
    _store_vt(vb_ref, _dot_nt(wvb_ref[...], xs), B_HEADS)


def _l0_proj(x, mod, mod_row, weights, tables, *, tm, name):
    b, t, d = x.shape
    wqa, wka, wva, wqb, wkb, wvb = weights
    cqa, sqa, cka, ska = tables
    mod_map = (lambda bi, i: (bi, 0, 0)) if mod_row is None else (lambda bi, i: (mod_row, 0, 0))

    def full(a):
        return pl.BlockSpec(a.shape, lambda bi, i: (0,) * a.ndim)

    in_specs = [pl.BlockSpec((1, tm, d), lambda bi, i: (bi, i, 0)),
                pl.BlockSpec((1, N_MOD, d), mod_map),
                full(wqa), pl.BlockSpec((LANES, tm), lambda bi, i: (0, i)), pl.BlockSpec((LANES, tm), lambda bi, i: (0, i)),
                full(wka), pl.BlockSpec((tm, LANES), lambda bi, i: (i, 0)), pl.BlockSpec((tm, LANES), lambda bi, i: (i, 0)),
                full(wva), full(wqb), full(wkb), full(wvb)]
    nb = B_HEADS * HEAD_DIM

    def nat(n):
        return jax.ShapeDtypeStruct((b, t, n), BF16), pl.BlockSpec((1, tm, n), lambda bi, i: (bi, i, 0))

    def tr(n):
        return jax.ShapeDtypeStruct((b, n, t), BF16), pl.BlockSpec((1, n, tm), lambda bi, i: (bi, 0, i))

    outs = [tr(A_HEADS * LANES), nat(LANES), tr(A_KV_HEADS * V_ROWS), tr(B_HEADS * LANES), nat(nb), tr(B_HEADS * V_ROWS)]
    return pl.pallas_call(
        _l0_proj_body, grid=(b, t // tm), in_specs=in_specs,
        out_specs=[o[1] for o in outs], out_shape=[o[0] for o in outs], name=name,
        compiler_params=_cparams(("parallel", "parallel")),
    )(x, mod, wqa, cqa, sqa, wka, cka, ska, wva, wqb, wkb, wvb)


def _l1_proj_body(*refs, with_q):
    it = iter(refs)
    x_ref, mod_ref, wdn_ref, gq_ref, gkv_ref = (next(it) for _ in range(5))
    wq_ref, cq_ref, sq_ref = (next(it) for _ in range(3)) if with_q else (None, None, None)
    wk_ref, ck_ref, sk_ref, wv_ref = (next(it) for _ in range(4))
    q_ref = next(it) if with_q else None
    k_ref, v_ref = next(it), next(it)

    sc = mod_ref[0, SC1:SC1 + 1, :]
    sh = mod_ref[0, SH1:SH1 + 1, :]
    xs = (x_ref[0] * (1.0 + sc) + sh).astype(BF16)

    def rms(v, g_ref):
        ms = jnp.mean(v * v, axis=-1, keepdims=True)
        return (v * lax.rsqrt(ms + EPS) * g_ref[...]).astype(BF16)

    n_q, n_kv = MLA_Q_LORA, MLA_KV_LORA
    c_kv = rms(_dot(xs, wdn_ref[:, n_q:n_q + n_kv]), gkv_ref)
    k_r = _rotate(_dot(xs, wdn_ref[:, n_q + n_kv:]), ck_ref[...], sk_ref[...], 1)

    if with_q:
        c_q = rms(_dot(xs, wdn_ref[:, :n_q]), gq_ref)
        cq = jnp.concatenate([cq_ref[...]] * (ROW_CHUNK // LANES), axis=0)
        sq = jnp.concatenate([sq_ref[...]] * (ROW_CHUNK // LANES), axis=0)
        for r in range(MLA_HEADS * LANES // ROW_CHUNK):
            rows = slice(r * ROW_CHUNK, (r + 1) * ROW_CHUNK)
            q_ref[0, rows, :] = _rotate(_dot_nt(wq_ref[rows, :], c_q), cq, sq, 0).astype(q_ref.dtype)

    k_r_tiled = jnp.concatenate([k_r] * (ROW_CHUNK // LANES), axis=-1)
    for j in range(MLA_HEADS * LANES // ROW_CHUNK):
        cols = slice(j * ROW_CHUNK, (j + 1) * ROW_CHUNK)
        k_ref[0, :, cols] = (_dot(c_kv, wk_ref[:, cols]) + k_r_tiled).astype(k_ref.dtype)

    _store_vt(v_ref, _dot_nt(wv_ref[...], c_kv), MLA_HEADS)


def _l1_proj(x, mod, mod_row, weights, gains, q_tables, k_tables, *, tm, with_q, name):
    b, t, d = x.shape
    wdn, wq, wk, wv = weights
    mod_map = (lambda bi, i: (bi, 0, 0)) if mod_row is None else (lambda bi, i: (mod_row, 0, 0))

    def full(a):
        return pl.BlockSpec(a.shape, lambda bi, i: (0,) * a.ndim)

    in_specs = [pl.BlockSpec((1, tm, d), lambda bi, i: (bi, i, 0)), pl.BlockSpec((1, N_MOD, d), mod_map),
                full(wdn), full(gains[0]), full(gains[1])]
    args = [x, mod, wdn, gains[0], gains[1]]
    if with_q:
        in_specs += [full(wq), pl.BlockSpec((LANES, tm), lambda bi, i: (0, i)), pl.BlockSpec((LANES, tm), lambda bi, i: (0, i))]
        args += [wq, q_tables[0], q_tables[1]]
    in_specs += [full(wk), pl.BlockSpec((tm, LANES), lambda bi, i: (i, 0)), pl.BlockSpec((tm, LANES), lambda bi, i: (i, 0)), full(wv)]
    args += [wk, k_tables[0], k_tables[1], wv]
    out_shape, out_specs = [], []
    if with_q:
        out_shape.append(jax.ShapeDtypeStruct((b, MLA_HEADS * LANES, t), BF16))
        out_specs.append(pl.BlockSpec((1, MLA_HEADS * LANES, tm), lambda bi, i: (bi, 0, i)))
    out_shape += [jax.ShapeDtypeStruct((b, t, MLA_HEADS * LANES), BF16), jax.ShapeDtypeStruct((b, MLA_HEADS * V_ROWS, t), BF16)]
    out_specs += [pl.BlockSpec((1, tm, MLA_HEADS * LANES), lambda bi, i: (bi, i, 0)),
                  pl.BlockSpec((1, MLA_HEADS * V_ROWS, tm), lambda bi, i: (bi, 0, i))]
    return pl.pallas_call(
        functools.partial(_l1_proj_body, with_q=with_q), grid=(b, t // tm), in_specs=in_specs,
        out_specs=out_specs, out_shape=out_shape, name=name,
        compiler_params=_cparams(("parallel", "parallel")),
    )(*args)


def _dense_body(*refs, heads, k_slot, v_row, nk, has_ctx):
    if has_ctx:
        qt_ref, k_ref, vt_ref, kc_ref, vct_ref, out_ref, m_ref, l_ref, acc_ref = refs
    else:
        qt_ref, k_ref, vt_ref, out_ref, m_ref, l_ref, acc_ref = refs
    ki = pl.program_id(2)
    last = nk if has_ctx else nk - 1

    @pl.when(ki == 0)
    def _():
        m_ref[...] = jnp.full(m_ref.shape, MASK_VALUE, F32)
        l_ref[...] = jnp.zeros(l_ref.shape, F32)
        acc_ref[...] = jnp.zeros(acc_ref.shape, F32)

    def step(kr, vr):
        def scores(h):
            ks = k_slot[h]
            return _dot(kr[0, :, ks * LANES:(ks + 1) * LANES], qt_ref[0, h * LANES:(h + 1) * LANES, :])

        pending = [scores(h) for h in range(min(AHEAD, heads))]
        for h in range(heads):
            s = pending.pop(0)
            if h + AHEAD < heads:
                pending.append(scores(h + AHEAD))
            rows = slice(h * HEAD_DIM, (h + 1) * HEAD_DIM)
            m_prev = m_ref[h:h + 1, :]
            m_new = jnp.maximum(m_prev, jnp.max(s, axis=0, keepdims=True))
            alpha = jnp.exp2(m_prev - m_new)
            p = jnp.exp2(s - m_new).astype(BF16)
            pv = _dot(vr[0, v_row[h]:v_row[h] + V_ROWS, :], p)
            m_ref[h:h + 1, :] = m_new
            l_ref[h:h + 1, :] = alpha * l_ref[h:h + 1, :] + pv[HEAD_DIM:HEAD_DIM + 1]
            acc_ref[rows, :] = alpha * acc_ref[rows, :] + pv[:HEAD_DIM]

    if has_ctx:
        @pl.when(ki < nk)
        def _():
            step(k_ref, vt_ref)

        @pl.when(ki == nk)
        def _():
            step(kc_ref, vct_ref)
    else:
        step(k_ref, vt_ref)

    @pl.when(ki == last)
    def _():
        for h in range(heads):
            rows = slice(h * HEAD_DIM, (h + 1) * HEAD_DIM)
            acc_ref[rows, :] = acc_ref[rows, :] * (1.0 / l_ref[h:h + 1, :])
        out_ref[0] = acc_ref[...].T.astype(out_ref.dtype)


def _dense_attn(qt, k, vt, kc=None, vct=None, *, heads, k_slot, v_row, tq, tk, name):
    b, nq, lq = qt.shape
    _, lk, nkw = k.shape
    nv = vt.shape[1]
    assert nq == heads * LANES and lq % tq == 0 and lk % tk == 0
    nk = lk // tk
    has_ctx = kc is not None
    steps = nk + (1 if has_ctx else 0)
    in_specs = [
        pl.BlockSpec((1, nq, tq), lambda bi, qi, ki: (bi, 0, qi)),
        pl.BlockSpec((1, tk, nkw), lambda bi, qi, ki: (bi, jnp.minimum(ki, nk - 1), 0)),
        pl.BlockSpec((1, nv, tk), lambda bi, qi, ki: (bi, 0, jnp.minimum(ki, nk - 1))),
    ]
    args = [qt, k, vt]
    if has_ctx:
        lc = kc.shape[1]
        in_specs += [pl.BlockSpec((1, lc, nkw), lambda bi, qi, ki: (bi, 0, 0)),
                     pl.BlockSpec((1, nv, lc), lambda bi, qi, ki: (bi, 0, 0))]
        args += [kc, vct]
    body = functools.partial(_dense_body, heads=heads, k_slot=tuple(k_slot), v_row=tuple(v_row),
                             nk=nk, has_ctx=has_ctx)
    return pl.pallas_call(
        body, grid=(b, lq // tq, steps), in_specs=in_specs,
        out_specs=pl.BlockSpec((1, tq, heads * HEAD_DIM), lambda bi, qi, ki: (bi, qi, 0)),
        out_shape=jax.ShapeDtypeStruct((b, lq, heads * HEAD_DIM), BF16),
        scratch_shapes=[pltpu.VMEM((heads, tq), F32), pltpu.VMEM((heads, tq), F32),
                        pltpu.VMEM((heads * HEAD_DIM, tq), F32)],
        name=name, compiler_params=_cparams(("parallel", "parallel", "arbitrary")),
    )(*args)


NA_QROWS = 4
NA_TQ = NA_QROWS * GRID_W
NA_WROWS = 12
NA_WBLOCKS = NA_WROWS // NA_QROWS
NA_STEPS = GRID_H // NA_QROWS


def _natten_body(qt_ref, k0_ref, k1_ref, k2_ref, v0_ref, v1_ref, v2_ref, kc_ref, vct_ref, bias_ref,
                 out_ref, o_ref):
    k_refs = (k0_ref, k1_ref, k2_ref)
    v_refs = (v0_ref, v1_ref, v2_ref)

    def scores(h):
        lanes = slice((h // 2) * LANES, (h // 2 + 1) * LANES)
        qq = qt_ref[0, h * LANES:(h + 1) * LANES, :]
        return [_dot(kc_ref[0, :, lanes], qq)] + [_dot(k_refs[blk][0, :, lanes], qq) for blk in range(NA_WBLOCKS)]

    pending = [scores(h) for h in range(AHEAD)]
    for h in range(B_HEADS):
        s_all = pending.pop(0)
        if h + AHEAD < B_HEADS:
            pending.append(scores(h + AHEAD))
        vrows = slice(h * V_ROWS, (h + 1) * V_ROWS)
        s_all = [s_all[0]] + [s_all[1 + blk] + bias_ref[0, h, blk * NA_TQ:(blk + 1) * NA_TQ, :]
                              for blk in range(NA_WBLOCKS)]
        m = s_all[0].max(axis=0, keepdims=True)
        for s in s_all[1:]:
            m = jnp.maximum(m, s.max(axis=0, keepdims=True))
        pv = _dot(vct_ref[0, vrows, :], jnp.exp2(s_all[0] - m).astype(BF16))
        for blk in range(NA_WBLOCKS):
            pv = pv + _dot(v_refs[blk][0, vrows, :], jnp.exp2(s_all[1 + blk] - m).astype(BF16))
        o_ref[h * HEAD_DIM:(h + 1) * HEAD_DIM, :] = pv[:HEAD_DIM] * (1.0 / pv[HEAD_DIM:HEAD_DIM + 1])
    out_ref[0] = o_ref[...].T.astype(out_ref.dtype)


def _natten(qt, k, vt, kc, vct, bias):
    b = qt.shape[0]
    nw = B_HEADS * HEAD_DIM
    nv = B_HEADS * V_ROWS

    def wstart(i):
        return jnp.clip(i - 1, 0, NA_STEPS - NA_WBLOCKS)

    def cls(i):
        return jnp.where(i == 0, 0, jnp.where(i == NA_STEPS - 1, 2, 1))

    in_specs = [pl.BlockSpec((1, B_HEADS * LANES, NA_TQ), lambda bi, i: (bi, 0, i))]
    in_specs += [pl.BlockSpec((1, NA_TQ, nw), functools.partial(lambda bi, i, j: (bi, wstart(i) + j, 0), j=j))
                 for j in range(NA_WBLOCKS)]
    in_specs += [pl.BlockSpec((1, nv, NA_TQ), functools.partial(lambda bi, i, j: (bi, 0, wstart(i) + j), j=j))
                 for j in range(NA_WBLOCKS)]
    in_specs += [pl.BlockSpec((1, CTX_LEN, nw), lambda bi, i: (bi, 0, 0)),
                 pl.BlockSpec((1, nv, CTX_LEN), lambda bi, i: (bi, 0, 0)),
                 pl.BlockSpec((1, B_HEADS, NA_WROWS * GRID_W, NA_TQ), lambda bi, i: (cls(i), 0, 0, 0))]
    return pl.pallas_call(
        _natten_body, grid=(b, NA_STEPS), in_specs=in_specs,
        out_specs=pl.BlockSpec((1, NA_TQ, nw), lambda bi, i: (bi, i, 0)),
        out_shape=jax.ShapeDtypeStruct((b, SEQ, nw), BF16),
        scratch_shapes=[pltpu.VMEM((nw, NA_TQ), F32)],
        name="natten", compiler_params=_cparams(("parallel", "arbitrary")),
    )(qt, k, k, k, vt, vt, vt, kc, vct, bias)


def _natten_bias(rpb):
    v = rpb.astype(F32) * LOG2E
    h, nr, _ = v.shape
    period = 2 * GRID_W
    row = jnp.zeros((h, nr, period), F32)
    row = row.at[..., :NA_COLS].set(v[..., NA_COLS - 1::-1])
    row = row.at[..., period - (NA_COLS - 1):].set(v[..., :NA_COLS - 1:-1])
    toep = jnp.tile(row, (1, 1, GRID_W))[..., :GRID_W * (period - 1)]
    toep = toep.reshape(h, nr, GRID_W, period - 1)[..., :GRID_W]
    col = np.arange(GRID_W)
    cs = np.clip(col - NA_COLS // 2, 0, GRID_W - NA_COLS)
    col_ok = (col[:, None] >= cs[None, :]) & (col[:, None] < cs[None, :] + NA_COLS)
    masked = 2.0 * MASK_VALUE
    toep = jnp.where(jnp.asarray(col_ok), toep, masked)
    dead = jnp.full((h, GRID_W, GRID_W), masked, F32)
    tables = []
    for r0 in (0, 2 * NA_QROWS, GRID_H - NA_QROWS):
        ws = min(max(r0 - NA_QROWS, 0), GRID_H - NA_WROWS)
        key_rows = []
        for ki in range(NA_WROWS):
            blocks = []
            for qi in range(NA_QROWS):
                k_row, q_row = ws + ki, r0 + qi
                rs = min(max(q_row - NA_ROWS // 2, 0), GRID_H - NA_ROWS)
                blocks.append(toep[:, k_row - q_row + NA_ROWS - 1] if rs <= k_row < rs + NA_ROWS else dead)
            key_rows.append(jnp.concatenate(blocks, axis=-1))
        tables.append(jnp.concatenate(key_rows, axis=-2))
    return jnp.stack(tables)


def _ffn_body(x_ref, xp_ref, xn_ref, mod_ref, wg_ref, wv_ref, wd_ref, cw_ref, cb_ref, lng_ref, lnb_ref,
              out_ref, hs_ref, g_ref, *, tm, rc, n_tiles):
    i = pl.program_id(1)
    sc = mod_ref[0, SC2:SC2 + 1, :]
    sh = mod_ref[0, SH2:SH2 + 1, :]
    gate_mod = mod_ref[0, G2:G2 + 1, :]

    def modulate(v):
        return v * (1.0 + sc) + sh

    hp = jnp.where(i > 0, modulate(xp_ref[0]), 0.0)
    hn = jnp.where(i < n_tiles - 1, modulate(xn_ref[0]), 0.0)
    hs_ref[0:FFN_HALO, :] = hp.astype(BF16)
    hs_ref[FFN_HALO:FFN_HALO + tm, :] = modulate(x_ref[0]).astype(BF16)
    hs_ref[FFN_HALO + tm:, :] = hn.astype(BF16)

    def up(c):
        lo = c * rc
        g_ref[c] = _dot(hs_ref[lo:lo + rc + 2 * FFN_HALO, :], wg_ref[...])
        return _dot(hs_ref[FFN_HALO + lo:FFN_HALO + lo + rc, :], wv_ref[...])

    n_chunks = tm // rc
    val_next = up(0)
    for c in range(n_chunks):
        val = val_next
        if c + 1 < n_chunks:
            val_next = up(c + 1)
        gate = (cw_ref[0:1, :] * g_ref[c, FFN_HALO - 1:FFN_HALO - 1 + rc, :]
                + cw_ref[1:2, :] * g_ref[c, FFN_HALO:FFN_HALO + rc, :]
                + cw_ref[2:3, :] * g_ref[c, FFN_HALO + 1:FFN_HALO + 1 + rc, :]
                + cb_ref[...])
        act = gate * jax.nn.sigmoid(gate) * val
        y = _dot(act.astype(BF16), wd_ref[...])
        rows = slice(c * rc, (c + 1) * rc)
        out_ref[0, rows, :] = _layer_norm(DEEPNORM_ALPHA * x_ref[0, rows, :] + gate_mod * y, lng_ref[...], lnb_ref[...])


def _ffn(x, mod, mod_row, wg, wv, wd, conv_w, conv_b, ln_g, ln_b, *, tm, name):
    b, t, d = x.shape
    n_tiles = t // tm
    rc = min(FFN_ROWS, tm)
    hb = tm // FFN_HALO
    n_hblocks = t // FFN_HALO
    mod_map = (lambda bi, i: (bi, 0, 0)) if mod_row is None else (lambda bi, i: (mod_row, 0, 0))

    def resident(shape):
        return pl.BlockSpec(shape, lambda bi, i: (0, 0), pipeline_mode=pl.Buffered(1))

    in_specs = [
        pl.BlockSpec((1, tm, d), lambda bi, i: (bi, i, 0)),
        pl.BlockSpec((1, FFN_HALO, d), lambda bi, i: (bi, jnp.maximum(i * hb - 1, 0), 0)),
        pl.BlockSpec((1, FFN_HALO, d), lambda bi, i: (bi, jnp.minimum((i + 1) * hb, n_hblocks - 1), 0)),
        pl.BlockSpec((1, N_MOD, d), mod_map),
        resident((d, D_FF)), resident((d, D_FF)), resident((D_FF, d)),
        resident((3, D_FF)), resident((1, D_FF)), resident((1, d)), resident((1, d)),
    ]
    body = functools.partial(_ffn_body, tm=tm, rc=rc, n_tiles=n_tiles)
    return pl.pallas_call(
        body, grid=(b, n_tiles), in_specs=in_specs,
        out_specs=pl.BlockSpec((1, tm, d), lambda bi, i: (bi, i, 0)),
        out_shape=jax.ShapeDtypeStruct((b, t, d), F32),
        scratch_shapes=[pltpu.VMEM((tm + 2 * FFN_HALO, d), BF16),
                        pltpu.VMEM((tm // rc, rc + 2 * FFN_HALO, D_FF), F32)],
        name=name, compiler_params=_cparams(("parallel", "parallel")),
    )(x, x, x, mod, wg, wv, wd, conv_w.astype(F32), conv_b.reshape(1, D_FF).astype(F32),
      ln_g.reshape(1, d).astype(F32), ln_b.reshape(1, d).astype(F32))


def _rope_cos_sin(rot_dim):
    pos = jnp.arange(SEQ, dtype=jnp.int32)
    rows = (pos // GRID_W).astype(F32)
    cols = (pos % GRID_W).astype(F32)
    axis_dim = rot_dim // 2
    inv = ROPE_THETA ** (-jnp.arange(0, axis_dim, 2, dtype=F32) / axis_dim)
    ang = jnp.concatenate([rows[:, None] * inv, cols[:, None] * inv], axis=-1)
    cos = jnp.repeat(jnp.cos(ang), 2, axis=-1)
    sin = jnp.repeat(jnp.sin(ang), 2, axis=-1) * jnp.tile(jnp.array([-1.0, 1.0], F32), rot_dim // 2)
    return cos, sin


def _swap_pairs(g):
    return g.reshape(-1, 2)[:, ::-1].reshape(-1)


def _gqa_tables(gain, scale, length, rope):
    if rope:
        cos, sin = _rope_cos_sin(HEAD_DIM)
    else:
        cos, sin = jnp.ones((length, HEAD_DIM), F32), jnp.zeros((length, HEAD_DIM), F32)
    c = cos * (gain * scale)
    s = sin * (_swap_pairs(gain) * scale)
    return jnp.concatenate([c, c], axis=-1), jnp.concatenate([s, s], axis=-1)


def _mla_tables(scale, length, rope):
    c = jnp.full((length, LANES), scale, F32)
    s = jnp.zeros((length, LANES), F32)
    if rope:
        cos, sin = _rope_cos_sin(MLA_ROPE_DIM)
        lo, hi = MLA_NOPE_DIM, MLA_NOPE_DIM + MLA_ROPE_DIM
        c = c.at[:, lo:hi].set(cos * scale)
        s = s.at[:, lo:hi].set(sin * scale)
    return c, s


def _slot_rows(w, heads, width, half_of):
    k = w.shape[0]
    out = jnp.zeros((heads * LANES, k), w.dtype)
    for h in range(heads):
        off = h * LANES + half_of(h) * HALF
        out = out.at[off:off + width].set(w[:, h * width:(h + 1) * width].T)
    return out


def _layer0_mixer(x, xc, mod, w_in, q_gain, k_gain, rpb):
    w = w_in.astype(BF16)
    aq, akv, bw = A_HEADS * HEAD_DIM, A_KV_HEADS * HEAD_DIM, B_HEADS * HEAD_DIM
    o = 0
    wqa = _slot_rows(w[:, o:o + aq], A_HEADS, HEAD_DIM, lambda h: h // A_GROUP); o += aq
    wka = w[:, o:o + akv]; o += akv
    wva = w[:, o:o + akv].T; o += akv
    wqb = _slot_rows(w[:, o:o + bw], B_HEADS, HEAD_DIM, lambda h: h % 2); o += bw
    wkb = w[:, o:o + bw]; o += bw
    wvb = w[:, o:o + bw].T
    weights = (wqa, wka, wva, wqb, wkb, wvb)
    qscale = HEAD_SCALE * LOG2E

    def tables(length, rope):
        cq, sq = _gqa_tables(q_gain, qscale, length, rope)
        ck, sk = _gqa_tables(k_gain, 1.0, length, rope)
        return cq.T, sq.T, ck, sk

    qa_t, ka, va_t, qb_t, kb, vb_t = _l0_proj(x, mod, None, weights, tables(SEQ, True), tm=TM, name="l0_proj")
    qac_t, kac, vac_t, qbc_t, kbc, vbc_t = _l0_proj(xc, mod, CTX_MOD_ROW, weights, tables(CTX_LEN, False),
                                                    tm=TMC, name="l0_proj_ctx")

    a_kslot = [0] * A_HEADS
    a_vrow = [(h // A_GROUP) * V_ROWS for h in range(A_HEADS)]
    b_kslot = [h // 2 for h in range(B_HEADS)]
    b_vrow = [h * V_ROWS for h in range(B_HEADS)]
    ya = _dense_attn(qa_t, ka, va_t, kac, vac_t, heads=A_HEADS, k_slot=a_kslot, v_row=a_vrow,
                     tq=TQ, tk=TK, name="gqa")
    yb = _natten(qb_t, kb, vb_t, kbc, vbc_t, _natten_bias(rpb))
    yac = _dense_attn(qac_t, kac, vac_t, heads=A_HEADS, k_slot=a_kslot, v_row=a_vrow,
                      tq=CTX_LEN, tk=CTX_LEN, name="gqa_ctx")
    ybc = _dense_attn(qbc_t, kbc, vbc_t, heads=B_HEADS, k_slot=b_kslot, v_row=b_vrow,
                      tq=CTX_LEN, tk=CTX_LEN, name="nbr_ctx")
    return (ya, yb), (yac, ybc)


def _layer1_mixer(x, xc, mod, w_in, cq_gain, ckv_gain, w_uq, w_ukv):
    qk_dim = MLA_NOPE_DIM + MLA_ROPE_DIM
    kv_dim = MLA_NOPE_DIM + MLA_V_DIM
    n_in = MLA_Q_LORA + MLA_KV_LORA
    wdn = jnp.zeros((D_MODEL, n_in + LANES), BF16)
    wdn = wdn.at[:, :n_in].set(w_in[:, :n_in].astype(BF16))
    wdn = wdn.at[:, n_in + MLA_NOPE_DIM:n_in + qk_dim].set(w_in[:, n_in:].astype(BF16))
    wq = _slot_rows(w_uq.astype(BF16), MLA_HEADS, qk_dim, lambda h: 0)
    wkv = w_ukv.astype(BF16).reshape(MLA_KV_LORA, MLA_HEADS, kv_dim)
    wk = _slot_rows(wkv[:, :, :MLA_NOPE_DIM].reshape(MLA_KV_LORA, -1), MLA_HEADS, MLA_NOPE_DIM, lambda h: 0).T
    wv = wkv[:, :, MLA_NOPE_DIM:].reshape(MLA_KV_LORA, MLA_HEADS * MLA_V_DIM).T
    weights = (wdn, wq, wk, wv)
    gains = (cq_gain.reshape(1, -1).astype(F32), ckv_gain.reshape(1, -1).astype(F32))

    cq, sq = _mla_tables(MLA_SCALE * LOG2E, SEQ, True)
    qt, kk, vt = _l1_proj(x, mod, None, weights, gains, (cq.T, sq.T), _mla_tables(1.0, SEQ, True),
                          tm=TM, with_q=True, name="l1_proj")
    kkc, vtc = _l1_proj(xc, mod, CTX_MOD_ROW, weights, gains, None, _mla_tables(1.0, CTX_LEN, False),
                        tm=TMC, with_q=False, name="l1_proj_ctx")
    return _dense_attn(qt, kk, vt, kkc, vtc, heads=MLA_HEADS, k_slot=list(range(MLA_HEADS)),
                       v_row=[h * V_ROWS for h in range(MLA_HEADS)], tq=TQ, tk=TK, name="mla")


def _post_mixer(x, y, mod, mod_row, w_out, ln1, w_up, conv_w, conv_b, w_down, ln2, tm, tag):
    wo = w_out.astype(BF16)
    if isinstance(y, tuple):
        n0 = y[0].shape[2]
        extra = dict(x2=y[1], w2=wo[n0:])
        y, wo = y[0], wo[:n0]
    else:
        extra = {}
    x = _proj(y, wo, name="out_ln" + tag, tm=tm, tn=D_MODEL, mod=mod, mod_row=mod_row, gate_row=G1, resid=x,
              ln=ln1, **extra)
    wup = w_up.astype(BF16)
    return _ffn(x, mod, mod_row, wup[:, :D_FF], wup[:, D_FF:], w_down.astype(BF16), conv_w, conv_b,
                ln2[0], ln2[1], tm=tm, name="ffn" + tag)


def kernel(x, c, ctx, c_ctx, l0_w_ada, l0_b_ada, l0_w_in, l0_q_gain, l0_k_gain, l0_rpb, l0_w_out, l0_ln1_g, l0_ln1_b, l0_w_up, l0_conv_w, l0_conv_b, l0_w_down, l0_ln2_g, l0_ln2_b, l1_w_ada, l1_b_ada, l1_w_in, l1_cq_gain, l1_ckv_gain, l1_w_uq, l1_w_ukv, l1_w_out, l1_ln1_g, l1_ln1_b, l1_w_up, l1_conv_w, l1_conv_b, l1_w_down, l1_ln2_g, l1_ln2_b):
    xc = ctx
    mod = _modvec(c, c_ctx, l0_w_ada, l0_b_ada)
    y, yc = _layer0_mixer(x, xc, mod, l0_w_in, l0_q_gain, l0_k_gain, l0_rpb)
    post0 = (l0_w_out, (l0_ln1_g, l0_ln1_b), l0_w_up, l0_conv_w, l0_conv_b, l0_w_down, (l0_ln2_g, l0_ln2_b))
    x = _post_mixer(x, y, mod, None, *post0, tm=TM, tag="0")
    xc = _post_mixer(xc, yc, mod, CTX_MOD_ROW, *post0, tm=TMC, tag="0_ctx")
    mod = _modvec(c, c_ctx, l1_w_ada, l1_b_ada)
    y = _layer1_mixer(x, xc, mod, l1_w_in, l1_cq_gain, l1_ckv_gain, l1_w_uq, l1_w_ukv)
    post1 = (l1_w_out, (l1_ln1_g, l1_ln1_b), l1_w_up, l1_conv_w, l1_conv_b, l1_w_down, (l1_ln2_g, l1_ln2_b))
    return _post_mixer(x, y, mod, None, *post1, tm=TM, tag="1")
```

```python
import functools
import math

import numpy as np
import jax
import jax.numpy as jnp
from jax import lax
from jax.experimental import pallas as pl
from jax.experimental.pallas import tpu as pltpu

D_MODEL = 1024
BATCH = 4
SEQ = 4096
DEPTH = 2
GRID_W = 64
GRID_H = SEQ // GRID_W
CTX_LEN = 256
HEAD_DIM = 64
A_HEADS = 8
A_KV_HEADS = 2
A_GROUP = A_HEADS // A_KV_HEADS
B_HEADS = 8
NA_ROWS = 8
NA_COLS = 16
ROPE_THETA = 10000.0
MLA_HEADS = 16
MLA_Q_LORA = 768
MLA_KV_LORA = 256
MLA_NOPE_DIM = 64
MLA_ROPE_DIM = 32
MLA_V_DIM = 64
D_FF = 2816
N_MOD = 6
EPS = 1e-6
DEEPNORM_ALPHA = (2 * DEPTH) ** 0.25
HEAD_SCALE = HEAD_DIM ** -0.5
MLA_SCALE = (MLA_NOPE_DIM + MLA_ROPE_DIM) ** -0.5
LOG2E = math.log2(math.e)

LANES = 128
HALF = LANES // 2
BF16_ROWS = 16
V_ROWS = HEAD_DIM + BF16_ROWS
MASK_VALUE = -1e30
VMEM_LIMIT = 56 * 1024 * 1024

SH1, SC1, G1, SH2, SC2, G2 = range(N_MOD)

F32 = jnp.float32
BF16 = jnp.bfloat16

MOD_ROWS = 16
CTX_MOD_ROW = BATCH
TM = 512
TMC = CTX_LEN
TQ = 512
TK = 512
AHEAD = 2
BOUND_SLACK = 1.0 + 1e-3
L_MIN_OK = 2.0 ** -80
FFN_ROWS = 256
FFN_HALO = BF16_ROWS
ROW_CHUNK = 512


def _cparams(sem):
    return pltpu.CompilerParams(dimension_semantics=sem, vmem_limit_bytes=VMEM_LIMIT)


def _dot(a, b):
    return jnp.dot(a, b, preferred_element_type=F32)


def _dot_nt(a, b):
    return lax.dot_general(a, b, (((1,), (1,)), ((), ())), preferred_element_type=F32)


def _pair_swap(y, axis):
    n = y.shape[axis]
    idx = lax.broadcasted_iota(jnp.int32, y.shape, axis)
    nxt = pltpu.roll(y, n - 1, axis=axis)
    prv = pltpu.roll(y, 1, axis=axis)
    return jnp.where(idx % 2 == 0, nxt, prv)


def _rotate(y, c, s, axis):
    return y * c + _pair_swap(y, axis) * s


def _layer_norm(z, g, b):
    mu = jnp.mean(z, axis=-1, keepdims=True)
    zc = z - mu
    var = jnp.mean(zc * zc, axis=-1, keepdims=True)
    return zc * lax.rsqrt(var + EPS) * g + b


def _slot_norm_rows(kb, n_slots):
    rows = []
    for sl in range(n_slots):
        kf = kb[:, sl * LANES:(sl + 1) * LANES].astype(F32)
        n2 = jnp.sum(kf * kf, axis=-1, keepdims=True)
        rows.append(jnp.broadcast_to(jnp.max(n2, axis=0, keepdims=True), (1, LANES)))
    return rows[0] if n_slots == 1 else jnp.concatenate(rows, axis=0)


def _store_vt(out_ref, yt, heads):
    ones = jnp.ones((BF16_ROWS, yt.shape[1]), out_ref.dtype)
    for h in range(heads):
        out_ref[0, h * V_ROWS:h * V_ROWS + HEAD_DIM, :] = yt[h * HEAD_DIM:(h + 1) * HEAD_DIM].astype(out_ref.dtype)
        out_ref[0, h * V_ROWS + HEAD_DIM:(h + 1) * V_ROWS, :] = ones


def _proj_body(*refs, silu, bias, resid_ln, gate_row, two):
    it = iter(refs)
    x_ref = next(it)
    w_ref = next(it)
    x2_ref = next(it) if two else None
    w2_ref = next(it) if two else None
    bias_ref = next(it) if bias else None
    mod_ref = next(it) if resid_ln else None
    xres_ref = next(it) if resid_ln else None
    lng_ref = next(it) if resid_ln else None
    lnb_ref = next(it) if resid_ln else None
    out_ref = next(it)
    x = x_ref[0]
    if silu:
        x = x * jax.nn.sigmoid(x)
    y = _dot(x.astype(BF16), w_ref[...])
    if two:
        y = y + _dot(x2_ref[0], w2_ref[...])
    if bias:
        y = y + bias_ref[...]
    if resid_ln:
        g = mod_ref[0, gate_row:gate_row + 1, :]
        y = _layer_norm(DEEPNORM_ALPHA * xres_ref[0] + g * y, lng_ref[...], lnb_ref[...])
    out_ref[0] = y.astype(out_ref.dtype)


def _proj(x, w, *, name, tm, tn, x2=None, w2=None, silu=False, bias=None, mod=None, mod_row=None,
          gate_row=None, resid=None, ln=None, out_dtype=F32):
    b, t, k = x.shape
    n = w.shape[1]
    assert t % tm == 0 and n % tn == 0
    resid_ln = resid is not None
    in_specs = [pl.BlockSpec((1, tm, k), lambda bi, i, j: (bi, i, 0)),
                pl.BlockSpec((k, tn), lambda bi, i, j: (0, j))]
    args = [x, w]
    if x2 is not None:
        k2 = x2.shape[2]
        in_specs += [pl.BlockSpec((1, tm, k2), lambda bi, i, j: (bi, i, 0)),
                     pl.BlockSpec((k2, tn), lambda bi, i, j: (0, j))]
        args += [x2, w2]
    if bias is not None:
        in_specs.append(pl.BlockSpec((1, tn), lambda bi, i, j: (0, j)))
        args.append(bias.reshape(1, n).astype(F32))
    if resid_ln:
        mod_map = (lambda bi, i, j: (bi, 0, 0)) if mod_row is None else (lambda bi, i, j: (mod_row, 0, 0))
        in_specs += [pl.BlockSpec((1, N_MOD, D_MODEL), mod_map),
                     pl.BlockSpec((1, tm, n), lambda bi, i, j: (bi, i, 0)),
                     pl.BlockSpec((1, n), lambda bi, i, j: (0, 0)),
                     pl.BlockSpec((1, n), lambda bi, i, j: (0, 0))]
        args += [mod, resid, ln[0].reshape(1, n).astype(F32), ln[1].reshape(1, n).astype(F32)]
    body = functools.partial(_proj_body, silu=silu, bias=bias is not None, resid_ln=resid_ln, gate_row=gate_row,
                             two=x2 is not None)
    return pl.pallas_call(
        body, grid=(b, t // tm, n // tn), in_specs=in_specs,
        out_specs=pl.BlockSpec((1, tm, tn), lambda bi, i, j: (bi, i, j)),
        out_shape=jax.ShapeDtypeStruct((b, t, n), out_dtype), name=name,
        compiler_params=_cparams(("parallel", "parallel", "arbitrary")),
    )(*args)


def _modvec(c, c_ctx, w_ada, b_ada):
    cond = jnp.zeros((1, MOD_ROWS, D_MODEL), F32).at[0, :BATCH].set(c).at[0, BATCH].set(c_ctx)
    m = _proj(cond, w_ada.astype(BF16), name="adaln", tm=MOD_ROWS, tn=1024, silu=True, bias=b_ada)
    return m.reshape(MOD_ROWS, N_MOD, D_MODEL)


def _l0_proj_body(x_ref, mod_ref, wqa_ref, cqa_ref, sqa_ref, wka_ref, cka_ref, ska_ref, wva_ref, wqb_ref,
                  wkb_ref, wvb_ref, qa_ref, ka_ref, kna_ref, va_ref, qb_ref, kb_ref, vb_ref):
    sc = mod_ref[0, SC1:SC1 + 1, :]
    sh = mod_ref[0, SH1:SH1 + 1, :]
    xs = (x_ref[0] * (1.0 + sc) + sh).astype(BF16)

    cq, sq = cqa_ref[...], sqa_ref[...]
    for r in range(A_HEADS * LANES // ROW_CHUNK):
        yt = _dot_nt(wqa_ref[r * ROW_CHUNK:(r + 1) * ROW_CHUNK, :], xs)
        for s in range(ROW_CHUNK // LANES):
            ys = yt[s * LANES:(s + 1) * LANES]
            rot = _rotate(ys, cq, sq, 0)
            parts = []
            for half in range(2):
                seg = ys[half * HALF:(half + 1) * HALF]
                ms = jnp.mean(seg * seg, axis=0, keepdims=True)
                parts.append(rot[half * HALF:(half + 1) * HALF] * lax.rsqrt(ms + EPS))
            row0 = r * ROW_CHUNK + s * LANES
            qa_ref[0, row0:row0 + LANES, :] = jnp.concatenate(parts, axis=0).astype(qa_ref.dtype)

    y = _dot(xs, wka_ref[...])
    lo = lax.broadcasted_iota(jnp.int32, y.shape, 1) < HALF
    ysq = y * y
    ms_lo = jnp.sum(jnp.where(lo, ysq, 0.0), axis=-1, keepdims=True) * (1.0 / HEAD_DIM)
    ms_hi = jnp.sum(jnp.where(lo, 0.0, ysq), axis=-1, keepdims=True) * (1.0 / HEAD_DIM)
    rn = jnp.where(lo, lax.rsqrt(ms_lo + EPS), lax.rsqrt(ms_hi + EPS))
    ka = (_rotate(y, cka_ref[...], ska_ref[...], 1) * rn).astype(ka_ref.dtype)
    ka_ref[0] = ka
    kna_ref[0, 0] = jnp.broadcast_to(_slot_norm_rows(ka, 1), kna_ref.shape[2:])

    _store_vt(va_ref, _dot_nt(wva_ref[...], xs), A_KV_HEADS)

    for r in range(B_HEADS * LANES // ROW_CHUNK):
        rows = slice(r * ROW_CHUNK, (r + 1) * ROW_CHUNK)
        qb_ref[0, rows, :] = (_dot_nt(wqb_ref[rows, :], xs) * (HEAD_SCALE * LOG2E)).astype(qb_ref.dtype)
    kb_ref[0] = _dot(xs, wkb_ref[...]).astype(kb_ref.dtype)
    _store_vt(vb_ref, _dot_nt(wvb_ref[...], xs), B_HEADS)


def _l0_proj(x, mod, mod_row, weights, tables, *, tm, name):
    b, t, d = x.shape
    wqa, wka, wva, wqb, wkb, wvb = weights
    cqa, sqa, cka, ska = tables
    mod_map = (lambda bi, i: (bi, 0, 0)) if mod_row is None else (lambda bi, i: (mod_row, 0, 0))

    def full(a):
        return pl.BlockSpec(a.shape, lambda bi, i: (0,) * a.ndim)

    in_specs = [pl.BlockSpec((1, tm, d), lambda bi, i: (bi, i, 0)),
                pl.BlockSpec((1, N_MOD, d), mod_map),
                full(wqa), pl.BlockSpec((LANES, tm), lambda bi, i: (0, i)), pl.BlockSpec((LANES, tm), lambda bi, i: (0, i)),
                full(wka), pl.BlockSpec((tm, LANES), lambda bi, i: (i, 0)), pl.BlockSpec((tm, LANES), lambda bi, i: (i, 0)),
                full(wva), full(wqb), full(wkb), full(wvb)]
    nb = B_HEADS * HEAD_DIM

    def nat(n):
        return jax.ShapeDtypeStruct((b, t, n), BF16), pl.BlockSpec((1, tm, n), lambda bi, i: (bi, i, 0))

    def tr(n):
        return jax.ShapeDtypeStruct((b, n, t), BF16), pl.BlockSpec((1, n, tm), lambda bi, i: (bi, 0, i))

    kn = (jax.ShapeDtypeStruct((b, t // tm, 8, LANES), F32), pl.BlockSpec((1, 1, 8, LANES), lambda bi, i: (bi, i, 0, 0)))
    outs = [tr(A_HEADS * LANES), nat(LANES), kn, tr(A_KV_HEADS * V_ROWS), tr(B_HEADS * LANES), nat(nb), tr(B_HEADS * V_ROWS)]
    return pl.pallas_call(
        _l0_proj_body, grid=(b, t // tm), in_specs=in_specs,
        out_specs=[o[1] for o in outs], out_shape=[o[0] for o in outs], name=name,
        compiler_params=_cparams(("parallel", "parallel")),
    )(x, mod, wqa, cqa, sqa, wka, cka, ska, wva, wqb, wkb, wvb)


def _l1_proj_body(*refs, with_q):
    it = iter(refs)
    x_ref, mod_ref, wdn_ref, gq_ref, gkv_ref = (next(it) for _ in range(5))
    wq_ref, cq_ref, sq_ref = (next(it) for _ in range(3)) if with_q else (None, None, None)
    wk_ref, ck_ref, sk_ref, wv_ref = (next(it) for _ in range(4))
    q_ref = next(it) if with_q else None
    k_ref, kn_ref, v_ref = next(it), next(it), next(it)

    sc = mod_ref[0, SC1:SC1 + 1, :]
    sh = mod_ref[0, SH1:SH1 + 1, :]
    xs = (x_ref[0] * (1.0 + sc) + sh).astype(BF16)

    def rms(v, g_ref):
        ms = jnp.mean(v * v, axis=-1, keepdims=True)
        return (v * lax.rsqrt(ms + EPS) * g_ref[...]).astype(BF16)

    n_q, n_kv = MLA_Q_LORA, MLA_KV_LORA
    c_kv = rms(_dot(xs, wdn_ref[:, n_q:n_q + n_kv]), gkv_ref)
    k_r = _rotate(_dot(xs, wdn_ref[:, n_q + n_kv:]), ck_ref[...], sk_ref[...], 1)

    if with_q:
        c_q = rms(_dot(xs, wdn_ref[:, :n_q]), gq_ref)
        cq = jnp.concatenate([cq_ref[...]] * (ROW_CHUNK // LANES), axis=0)
        sq = jnp.concatenate([sq_ref[...]] * (ROW_CHUNK // LANES), axis=0)
        for r in range(MLA_HEADS * LANES // ROW_CHUNK):
            rows = slice(r * ROW_CHUNK, (r + 1) * ROW_CHUNK)
            q_ref[0, rows, :] = _rotate(_dot_nt(wq_ref[rows, :], c_q), cq, sq, 0).astype(q_ref.dtype)

    k_r_tiled = jnp.concatenate([k_r] * (ROW_CHUNK // LANES), axis=-1)
    for j in range(MLA_HEADS * LANES // ROW_CHUNK):
        cols = slice(j * ROW_CHUNK, (j + 1) * ROW_CHUNK)
        kb = (_dot(c_kv, wk_ref[:, cols]) + k_r_tiled).astype(k_ref.dtype)
        k_ref[0, :, cols] = kb
        n_sl = ROW_CHUNK // LANES
        kn_ref[0, 0, j * n_sl:(j + 1) * n_sl, :] = _slot_norm_rows(kb, n_sl)

    _store_vt(v_ref, _dot_nt(wv_ref[...], c_kv), MLA_HEADS)


def _l1_proj(x, mod, mod_row, weights, gains, q_tables, k_tables, *, tm, with_q, name):
    b, t, d = x.shape
    wdn, wq, wk, wv = weights
    mod_map = (lambda bi, i: (bi, 0, 0)) if mod_row is None else (lambda bi, i: (mod_row, 0, 0))

    def full(a):
        return pl.BlockSpec(a.shape, lambda bi, i: (0,) * a.ndim)

    in_specs = [pl.BlockSpec((1, tm, d), lambda bi, i: (bi, i, 0)), pl.BlockSpec((1, N_MOD, d), mod_map),
                full(wdn), full(gains[0]), full(gains[1])]
    args = [x, mod, wdn, gains[0], gains[1]]
    if with_q:
        in_specs += [full(wq), pl.BlockSpec((LANES, tm), lambda bi, i: (0, i)), pl.BlockSpec((LANES, tm), lambda bi, i: (0, i))]
        args += [wq, q_tables[0], q_tables[1]]
    in_specs += [full(wk), pl.BlockSpec((tm, LANES), lambda bi, i: (i, 0)), pl.BlockSpec((tm, LANES), lambda bi, i: (i, 0)), full(wv)]
    args += [wk, k_tables[0], k_tables[1], wv]
    out_shape, out_specs = [], []
    if with_q:
        out_shape.append(jax.ShapeDtypeStruct((b, MLA_HEADS * LANES, t), BF16))
        out_specs.append(pl.BlockSpec((1, MLA_HEADS * LANES, tm), lambda bi, i: (bi, 0, i)))
    out_shape += [jax.ShapeDtypeStruct((b, t, MLA_HEADS * LANES), BF16),
                  jax.ShapeDtypeStruct((b, t // tm, MLA_HEADS, LANES), F32),
                  jax.ShapeDtypeStruct((b, MLA_HEADS * V_ROWS, t), BF16)]
    out_specs += [pl.BlockSpec((1, tm, MLA_HEADS * LANES), lambda bi, i: (bi, i, 0)),
                  pl.BlockSpec((1, 1, MLA_HEADS, LANES), lambda bi, i: (bi, i, 0, 0)),
                  pl.BlockSpec((1, MLA_HEADS * V_ROWS, tm), lambda bi, i: (bi, 0, i))]
    return pl.pallas_call(
        functools.partial(_l1_proj_body, with_q=with_q), grid=(b, t // tm), in_specs=in_specs,
        out_specs=out_specs, out_shape=out_shape, name=name,
        compiler_params=_cparams(("parallel", "parallel")),
    )(*args)


def _dense_body(*refs, heads, k_slot, v_row, nk, has_ctx):
    if has_ctx:
        qt_ref, k_ref, vt_ref, kc_ref, vct_ref, out_ref, m_ref, l_ref, acc_ref = refs
    else:
        qt_ref, k_ref, vt_ref, out_ref, m_ref, l_ref, acc_ref = refs
    ki = pl.program_id(2)
    last = nk if has_ctx else nk - 1

    @pl.when(ki == 0)
    def _():
        m_ref[...] = jnp.full(m_ref.shape, MASK_VALUE, F32)
        l_ref[...] = jnp.zeros(l_ref.shape, F32)
        acc_ref[...] = jnp.zeros(acc_ref.shape, F32)

    def step(kr, vr):
        def scores(h):
            ks = k_slot[h]
            return _dot(kr[0, :, ks * LANES:(ks + 1) * LANES], qt_ref[0, h * LANES:(h + 1) * LANES, :])

        pending = [scores(h) for h in range(min(AHEAD, heads))]
        for h in range(heads):
            s = pending.pop(0)
            if h + AHEAD < heads:
                pending.append(scores(h + AHEAD))
            rows = slice(h * HEAD_DIM, (h + 1) * HEAD_DIM)
            m_prev = m_ref[h:h + 1, :]
            m_new = jnp.maximum(m_prev, jnp.max(s, axis=0, keepdims=True))
            alpha = jnp.exp2(m_prev - m_new)
            p = jnp.exp2(s - m_new).astype(BF16)
            pv = _dot(vr[0, v_row[h]:v_row[h] + V_ROWS, :], p)
            m_ref[h:h + 1, :] = m_new
            l_ref[h:h + 1, :] = alpha * l_ref[h:h + 1, :] + pv[HEAD_DIM:HEAD_DIM + 1]
            acc_ref[rows, :] = alpha * acc_ref[rows, :] + pv[:HEAD_DIM]

    if has_ctx:
        @pl.when(ki < nk)
        def _():
            step(k_ref, vt_ref)

        @pl.when(ki == nk)
        def _():
            step(kc_ref, vct_ref)
    else:
        step(k_ref, vt_ref)

    @pl.when(ki == last)
    def _():
        for h in range(heads):
            rows = slice(h * HEAD_DIM, (h + 1) * HEAD_DIM)
            acc_ref[rows, :] = acc_ref[rows, :] * (1.0 / l_ref[h:h + 1, :])
        out_ref[0] = acc_ref[...].T.astype(out_ref.dtype)


def _dense_attn(qt, k, vt, kc=None, vct=None, *, heads, k_slot, v_row, tq, tk, name):
    b, nq, lq = qt.shape
    _, lk, nkw = k.shape
    nv = vt.shape[1]
    assert nq == heads * LANES and lq % tq == 0 and lk % tk == 0
    nk = lk // tk
    has_ctx = kc is not None
    steps = nk + (1 if has_ctx else 0)
    in_specs = [
        pl.BlockSpec((1, nq, tq), lambda bi, qi, ki: (bi, 0, qi)),
        pl.BlockSpec((1, tk, nkw), lambda bi, qi, ki: (bi, jnp.minimum(ki, nk - 1), 0)),
        pl.BlockSpec((1, nv, tk), lambda bi, qi, ki: (bi, 0, jnp.minimum(ki, nk - 1))),
    ]
    args = [qt, k, vt]
    if has_ctx:
        lc = kc.shape[1]
        in_specs += [pl.BlockSpec((1, lc, nkw), lambda bi, qi, ki: (bi, 0, 0)),
                     pl.BlockSpec((1, nv, lc), lambda bi, qi, ki: (bi, 0, 0))]
        args += [kc, vct]
    body = functools.partial(_dense_body, heads=heads, k_slot=tuple(k_slot), v_row=tuple(v_row),
                             nk=nk, has_ctx=has_ctx)
    return pl.pallas_call(
        body, grid=(b, lq // tq, steps), in_specs=in_specs,
        out_specs=pl.BlockSpec((1, tq, heads * HEAD_DIM), lambda bi, qi, ki: (bi, qi, 0)),
        out_shape=jax.ShapeDtypeStruct((b, lq, heads * HEAD_DIM), BF16),
        scratch_shapes=[pltpu.VMEM((heads, tq), F32), pltpu.VMEM((heads, tq), F32),
                        pltpu.VMEM((heads * HEAD_DIM, tq), F32)],
        name=name, compiler_params=_cparams(("parallel", "parallel", "arbitrary")),
    )(*args)


def _dense_fast_body(qt_ref, k_ref, vt_ref, kc_ref, vct_ref, knl_ref, knc_ref, out_ref, lmin_ref,
                     b_ref, acc_ref, o_ref, *, heads, k_slot, v_row, nk):
    ki = pl.program_id(2)
    tq = qt_ref.shape[2]

    @pl.when(ki == 0)
    def _():
        kn = jnp.sqrt(jnp.maximum(jnp.max(knl_ref[0], axis=0), knc_ref[0, 0])) * BOUND_SLACK
        for h in range(heads):
            qf = qt_ref[0, h * LANES:(h + 1) * LANES, :].astype(F32)
            qn = jnp.sqrt(jnp.sum(qf * qf, axis=0, keepdims=True))
            krow = kn[k_slot[h]:k_slot[h] + 1, :]
            b_ref[h:h + 1, :] = qn * jnp.concatenate([krow] * (tq // LANES), axis=1)
        acc_ref[...] = jnp.zeros(acc_ref.shape, F32)

    def step(kr, vr):
        def scores(h):
            ks = k_slot[h]
            return _dot(kr[0, :, ks * LANES:(ks + 1) * LANES], qt_ref[0, h * LANES:(h + 1) * LANES, :])

        pending = [scores(h) for h in range(min(AHEAD, heads))]
        for h in range(heads):
            s = pending.pop(0)
            if h + AHEAD < heads:
                pending.append(scores(h + AHEAD))
            p = jnp.exp2(s - b_ref[h:h + 1, :]).astype(BF16)
            rows = slice(h * V_ROWS, (h + 1) * V_ROWS)
            acc_ref[rows, :] += _dot(vr[0, v_row[h]:v_row[h] + V_ROWS, :], p)

    @pl.when(ki < nk)
    def _():
        step(k_ref, vt_ref)

    @pl.when(ki == nk)
    def _():
        step(kc_ref, vct_ref)
        lmin = None
        for h in range(heads):
            l = acc_ref[h * V_ROWS + HEAD_DIM:h * V_ROWS + HEAD_DIM + 1, :]
            lmin = l if lmin is None else jnp.minimum(lmin, l)
            o_ref[h * HEAD_DIM:(h + 1) * HEAD_DIM, :] = acc_ref[h * V_ROWS:h * V_ROWS + HEAD_DIM, :] * (1.0 / l)
        out_ref[0] = o_ref[...].T.astype(out_ref.dtype)
        lmin_ref[0, 0] = jnp.broadcast_to(jnp.min(lmin, axis=1, keepdims=True), lmin_ref.shape[2:])


def _dense_attn_fast(qt, k, vt, kc, vct, knl, knc, *, heads, k_slot, v_row, tq, tk, name):
    b, nq, lq = qt.shape
    _, lk, nkw = k.shape
    nv = vt.shape[1]
    lc = kc.shape[1]
    nk = lk // tk
    assert nq == heads * LANES and lq % tq == 0 and lk % tk == 0 and knl.shape[1] == nk
    n_slots = knl.shape[2]
    in_specs = [
        pl.BlockSpec((1, nq, tq), lambda bi, qi, ki: (bi, 0, qi)),
        pl.BlockSpec((1, tk, nkw), lambda bi, qi, ki: (bi, jnp.minimum(ki, nk - 1), 0)),
        pl.BlockSpec((1, nv, tk), lambda bi, qi, ki: (bi, 0, jnp.minimum(ki, nk - 1))),
        pl.BlockSpec((1, lc, nkw), lambda bi, qi, ki: (bi, 0, 0)),
        pl.BlockSpec((1, nv, lc), lambda bi, qi, ki: (bi, 0, 0)),
        pl.BlockSpec((1, nk, n_slots, LANES), lambda bi, qi, ki: (bi, 0, 0, 0)),
        pl.BlockSpec((1, 1, n_slots, LANES), lambda bi, qi, ki: (bi, 0, 0, 0)),
    ]
    body = functools.partial(_dense_fast_body, heads=heads, k_slot=tuple(k_slot), v_row=tuple(v_row), nk=nk)
    return pl.pallas_call(
        body, grid=(b, lq // tq, nk + 1), in_specs=in_specs,
        out_specs=[pl.BlockSpec((1, tq, heads * HEAD_DIM), lambda bi, qi, ki: (bi, qi, 0)),
                   pl.BlockSpec((1, 1, 8, LANES), lambda bi, qi, ki: (bi, qi, 0, 0))],
        out_shape=[jax.ShapeDtypeStruct((b, lq, heads * HEAD_DIM), BF16),
                   jax.ShapeDtypeStruct((b, lq // tq, 8, LANES), F32)],
        scratch_shapes=[pltpu.VMEM((heads, tq), F32), pltpu.VMEM((heads * V_ROWS, tq), F32),
                        pltpu.VMEM((heads * HEAD_DIM, tq), F32)],
        name=name, compiler_params=_cparams(("parallel", "parallel", "arbitrary")),
    )(qt, k, vt, kc, vct, knl, knc)


def _attend(qt, k, vt, kc, vct, knl, knc, *, name, **kw):
    y, lmin = _dense_attn_fast(qt, k, vt, kc, vct, knl, knc, name=name + "_fast", **kw)
    return lax.cond(jnp.min(lmin) > L_MIN_OK, lambda: y,
                    lambda: _dense_attn(qt, k, vt, kc, vct, name=name, **kw))


NA_QROWS = 4
NA_TQ = NA_QROWS * GRID_W
NA_WROWS = 12
NA_WBLOCKS = NA_WROWS // NA_QROWS
NA_STEPS = GRID_H // NA_QROWS


def _natten_body(qt_ref, k0_ref, k1_ref, k2_ref, v0_ref, v1_ref, v2_ref, kc_ref, vct_ref, bias_ref,
                 out_ref, o_ref):
    k_refs = (k0_ref, k1_ref, k2_ref)
    v_refs = (v0_ref, v1_ref, v2_ref)

    def scores(h):
        lanes = slice((h // 2) * LANES, (h // 2 + 1) * LANES)
        qq = qt_ref[0, h * LANES:(h + 1) * LANES, :]
        return [_dot(kc_ref[0, :, lanes], qq)] + [_dot(k_refs[blk][0, :, lanes], qq) for blk in range(NA_WBLOCKS)]

    pending = [scores(h) for h in range(AHEAD)]
    for h in range(B_HEADS):
        s_all = pending.pop(0)
        if h + AHEAD < B_HEADS:
            pending.append(scores(h + AHEAD))
        vrows = slice(h * V_ROWS, (h + 1) * V_ROWS)
        s_all = [s_all[0]] + [s_all[1 + blk] + bias_ref[0, h, blk * NA_TQ:(blk + 1) * NA_TQ, :]
                              for blk in range(NA_WBLOCKS)]
        m = s_all[0].max(axis=0, keepdims=True)
        for s in s_all[1:]:
            m = jnp.maximum(m, s.max(axis=0, keepdims=True))
        pv = _dot(vct_ref[0, vrows, :], jnp.exp2(s_all[0] - m).astype(BF16))
        for blk in range(NA_WBLOCKS):
            pv = pv + _dot(v_refs[blk][0, vrows, :], jnp.exp2(s_all[1 + blk] - m).astype(BF16))
        o_ref[h * HEAD_DIM:(h + 1) * HEAD_DIM, :] = pv[:HEAD_DIM] * (1.0 / pv[HEAD_DIM:HEAD_DIM + 1])
    out_ref[0] = o_ref[...].T.astype(out_ref.dtype)


def _natten(qt, k, vt, kc, vct, bias):
    b = qt.shape[0]
    nw = B_HEADS * HEAD_DIM
    nv = B_HEADS * V_ROWS

    def wstart(i):
        return jnp.clip(i - 1, 0, NA_STEPS - NA_WBLOCKS)

    def cls(i):
        return jnp.where(i == 0, 0, jnp.where(i == NA_STEPS - 1, 2, 1))

    in_specs = [pl.BlockSpec((1, B_HEADS * LANES, NA_TQ), lambda bi, i: (bi, 0, i))]
    in_specs += [pl.BlockSpec((1, NA_TQ, nw), functools.partial(lambda bi, i, j: (bi, wstart(i) + j, 0), j=j))
                 for j in range(NA_WBLOCKS)]
    in_specs += [pl.BlockSpec((1, nv, NA_TQ), functools.partial(lambda bi, i, j: (bi, 0, wstart(i) + j), j=j))
                 for j in range(NA_WBLOCKS)]
    in_specs += [pl.BlockSpec((1, CTX_LEN, nw), lambda bi, i: (bi, 0, 0)),
                 pl.BlockSpec((1, nv, CTX_LEN), lambda bi, i: (bi, 0, 0)),
                 pl.BlockSpec((1, B_HEADS, NA_WROWS * GRID_W, NA_TQ), lambda bi, i: (cls(i), 0, 0, 0))]
    return pl.pallas_call(
        _natten_body, grid=(b, NA_STEPS), in_specs=in_specs,
        out_specs=pl.BlockSpec((1, NA_TQ, nw), lambda bi, i: (bi, i, 0)),
        out_shape=jax.ShapeDtypeStruct((b, SEQ, nw), BF16),
        scratch_shapes=[pltpu.VMEM((nw, NA_TQ), F32)],
        name="natten", compiler_params=_cparams(("parallel", "arbitrary")),
    )(qt, k, k, k, vt, vt, vt, kc, vct, bias)


def _natten_bias(rpb):
    v = rpb.astype(F32) * LOG2E
    h, nr, _ = v.shape
    period = 2 * GRID_W
    row = jnp.zeros((h, nr, period), F32)
    row = row.at[..., :NA_COLS].set(v[..., NA_COLS - 1::-1])
    row = row.at[..., period - (NA_COLS - 1):].set(v[..., :NA_COLS - 1:-1])
    toep = jnp.tile(row, (1, 1, GRID_W))[..., :GRID_W * (period - 1)]
    toep = toep.reshape(h, nr, GRID_W, period - 1)[..., :GRID_W]
    col = np.arange(GRID_W)
    cs = np.clip(col - NA_COLS // 2, 0, GRID_W - NA_COLS)
    col_ok = (col[:, None] >= cs[None, :]) & (col[:, None] < cs[None, :] + NA_COLS)
    masked = 2.0 * MASK_VALUE
    toep = jnp.where(jnp.asarray(col_ok), toep, masked)
    dead = jnp.full((h, GRID_W, GRID_W), masked, F32)
    tables = []
    for r0 in (0, 2 * NA_QROWS, GRID_H - NA_QROWS):
        ws = min(max(r0 - NA_QROWS, 0), GRID_H - NA_WROWS)
        key_rows = []
        for ki in range(NA_WROWS):
            blocks = []
            for qi in range(NA_QROWS):
                k_row, q_row = ws + ki, r0 + qi
                rs = min(max(q_row - NA_ROWS // 2, 0), GRID_H - NA_ROWS)
                blocks.append(toep[:, k_row - q_row + NA_ROWS - 1] if rs <= k_row < rs + NA_ROWS else dead)
            key_rows.append(jnp.concatenate(blocks, axis=-1))
        tables.append(jnp.concatenate(key_rows, axis=-2))
    return jnp.stack(tables)


def _ffn_body(x_ref, xp_ref, xn_ref, mod_ref, wg_ref, wv_ref, wd_ref, cw_ref, cb_ref, lng_ref, lnb_ref,
              out_ref, hs_ref, g_ref, *, tm, rc, n_tiles):
    i = pl.program_id(1)
    sc = mod_ref[0, SC2:SC2 + 1, :]
    sh = mod_ref[0, SH2:SH2 + 1, :]
    gate_mod = mod_ref[0, G2:G2 + 1, :]

    def modulate(v):
        return v * (1.0 + sc) + sh

    hp = jnp.where(i > 0, modulate(xp_ref[0]), 0.0)
    hn = jnp.where(i < n_tiles - 1, modulate(xn_ref[0]), 0.0)
    hs_ref[0:FFN_HALO, :] = hp.astype(BF16)
    hs_ref[FFN_HALO:FFN_HALO + tm, :] = modulate(x_ref[0]).astype(BF16)
    hs_ref[FFN_HALO + tm:, :] = hn.astype(BF16)

    def up(c):
        lo = c * rc
        g_ref[c] = _dot(hs_ref[lo:lo + rc + 2 * FFN_HALO, :], wg_ref[...])
        return _dot(hs_ref[FFN_HALO + lo:FFN_HALO + lo + rc, :], wv_ref[...])

    n_chunks = tm // rc
    val_next = up(0)
    for c in range(n_chunks):
        val = val_next
        if c + 1 < n_chunks:
            val_next = up(c + 1)
        gate = (cw_ref[0:1, :] * g_ref[c, FFN_HALO - 1:FFN_HALO - 1 + rc, :]
                + cw_ref[1:2, :] * g_ref[c, FFN_HALO:FFN_HALO + rc, :]
                + cw_ref[2:3, :] * g_ref[c, FFN_HALO + 1:FFN_HALO + 1 + rc, :]
                + cb_ref[...])
        act = gate * jax.nn.sigmoid(gate) * val
        y = _dot(act.astype(BF16), wd_ref[...])
        rows = slice(c * rc, (c + 1) * rc)
        out_ref[0, rows, :] = _layer_norm(DEEPNORM_ALPHA * x_ref[0, rows, :] + gate_mod * y, lng_ref[...], lnb_ref[...])


def _ffn(x, mod, mod_row, wg, wv, wd, conv_w, conv_b, ln_g, ln_b, *, tm, name):
    b, t, d = x.shape
    n_tiles = t // tm
    rc = min(FFN_ROWS, tm)
    hb = tm // FFN_HALO
    n_hblocks = t // FFN_HALO
    mod_map = (lambda bi, i: (bi, 0, 0)) if mod_row is None else (lambda bi, i: (mod_row, 0, 0))

    def resident(shape):
        return pl.BlockSpec(shape, lambda bi, i: (0, 0), pipeline_mode=pl.Buffered(1))

    in_specs = [
        pl.BlockSpec((1, tm, d), lambda bi, i: (bi, i, 0)),
        pl.BlockSpec((1, FFN_HALO, d), lambda bi, i: (bi, jnp.maximum(i * hb - 1, 0), 0)),
        pl.BlockSpec((1, FFN_HALO, d), lambda bi, i: (bi, jnp.minimum((i + 1) * hb, n_hblocks - 1), 0)),
        pl.BlockSpec((1, N_MOD, d), mod_map),
        resident((d, D_FF)), resident((d, D_FF)), resident((D_FF, d)),
        resident((3, D_FF)), resident((1, D_FF)), resident((1, d)), resident((1, d)),
    ]
    body = functools.partial(_ffn_body, tm=tm, rc=rc, n_tiles=n_tiles)
    return pl.pallas_call(
        body, grid=(b, n_tiles), in_specs=in_specs,
        out_specs=pl.BlockSpec((1, tm, d), lambda bi, i: (bi, i, 0)),
        out_shape=jax.ShapeDtypeStruct((b, t, d), F32),
        scratch_shapes=[pltpu.VMEM((tm + 2 * FFN_HALO, d), BF16),
                        pltpu.VMEM((tm // rc, rc + 2 * FFN_HALO, D_FF), F32)],
        name=name, compiler_params=_cparams(("parallel", "parallel")),
    )(x, x, x, mod, wg, wv, wd, conv_w.astype(F32), conv_b.reshape(1, D_FF).astype(F32),
      ln_g.reshape(1, d).astype(F32), ln_b.reshape(1, d).astype(F32))


def _rope_cos_sin(rot_dim):
    pos = jnp.arange(SEQ, dtype=jnp.int32)
    rows = (pos // GRID_W).astype(F32)
    cols = (pos % GRID_W).astype(F32)
    axis_dim = rot_dim // 2
    inv = ROPE_THETA ** (-jnp.arange(0, axis_dim, 2, dtype=F32) / axis_dim)
    ang = jnp.concatenate([rows[:, None] * inv, cols[:, None] * inv], axis=-1)
    cos = jnp.repeat(jnp.cos(ang), 2, axis=-1)
    sin = jnp.repeat(jnp.sin(ang), 2, axis=-1) * jnp.tile(jnp.array([-1.0, 1.0], F32), rot_dim // 2)
    return cos, sin


def _swap_pairs(g):
    return g.reshape(-1, 2)[:, ::-1].reshape(-1)


def _gqa_tables(gain, scale, length, rope):
    if rope:
        cos, sin = _rope_cos_sin(HEAD_DIM)
    else:
        cos, sin = jnp.ones((length, HEAD_DIM), F32), jnp.zeros((length, HEAD_DIM), F32)
    c = cos * (gain * scale)
    s = sin * (_swap_pairs(gain) * scale)
    return jnp.concatenate([c, c], axis=-1), jnp.concatenate([s, s], axis=-1)


def _mla_tables(scale, length, rope):
    c = jnp.full((length, LANES), scale, F32)
    s = jnp.zeros((length, LANES), F32)
    if rope:
        cos, sin = _rope_cos_sin(MLA_ROPE_DIM)
        lo, hi = MLA_NOPE_DIM, MLA_NOPE_DIM + MLA_ROPE_DIM
        c = c.at[:, lo:hi].set(cos * scale)
        s = s.at[:, lo:hi].set(sin * scale)
    return c, s


def _slot_rows(w, heads, width, half_of):
    k = w.shape[0]
    out = jnp.zeros((heads * LANES, k), w.dtype)
    for h in range(heads):
        off = h * LANES + half_of(h) * HALF
        out = out.at[off:off + width].set(w[:, h * width:(h + 1) * width].T)
    return out


def _layer0_mixer(x, xc, mod, w_in, q_gain, k_gain, rpb):
    w = w_in.astype(BF16)
    aq, akv, bw = A_HEADS * HEAD_DIM, A_KV_HEADS * HEAD_DIM, B_HEADS * HEAD_DIM
    o = 0
    wqa = _slot_rows(w[:, o:o + aq], A_HEADS, HEAD_DIM, lambda h: h // A_GROUP); o += aq
    wka = w[:, o:o + akv]; o += akv
    wva = w[:, o:o + akv].T; o += akv
    wqb = _slot_rows(w[:, o:o + bw], B_HEADS, HEAD_DIM, lambda h: h % 2); o += bw
    wkb = w[:, o:o + bw]; o += bw
    wvb = w[:, o:o + bw].T
    weights = (wqa, wka, wva, wqb, wkb, wvb)
    qscale = HEAD_SCALE * LOG2E

    def tables(length, rope):
        cq, sq = _gqa_tables(q_gain, qscale, length, rope)
        ck, sk = _gqa_tables(k_gain, 1.0, length, rope)
        return cq.T, sq.T, ck, sk

    qa_t, ka, kna, va_t, qb_t, kb, vb_t = _l0_proj(x, mod, None, weights, tables(SEQ, True), tm=TM, name="l0_proj")
    qac_t, kac, knac, vac_t, qbc_t, kbc, vbc_t = _l0_proj(xc, mod, CTX_MOD_ROW, weights, tables(CTX_LEN, False),
                                                    tm=TMC, name="l0_proj_ctx")

    a_kslot = [0] * A_HEADS
    a_vrow = [(h // A_GROUP) * V_ROWS for h in range(A_HEADS)]
    b_kslot = [h // 2 for h in range(B_HEADS)]
    b_vrow = [h * V_ROWS for h in range(B_HEADS)]
    ya = _attend(qa_t, ka, va_t, kac, vac_t, kna, knac, heads=A_HEADS, k_slot=a_kslot, v_row=a_vrow,
                 tq=TQ, tk=TK, name="gqa")
    yb = _natten(qb_t, kb, vb_t, kbc, vbc_t, _natten_bias(rpb))
    yac = _dense_attn(qac_t, kac, vac_t, heads=A_HEADS, k_slot=a_kslot, v_row=a_vrow,
                      tq=CTX_LEN, tk=CTX_LEN, name="gqa_ctx")
    ybc = _dense_attn(qbc_t, kbc, vbc_t, heads=B_HEADS, k_slot=b_kslot, v_row=b_vrow,
                      tq=CTX_LEN, tk=CTX_LEN, name="nbr_ctx")
    return (ya, yb), (yac, ybc)


def _layer1_mixer(x, xc, mod, w_in, cq_gain, ckv_gain, w_uq, w_ukv):
    qk_dim = MLA_NOPE_DIM + MLA_ROPE_DIM
    kv_dim = MLA_NOPE_DIM + MLA_V_DIM
    n_in = MLA_Q_LORA + MLA_KV_LORA
    wdn = jnp.zeros((D_MODEL, n_in + LANES), BF16)
    wdn = wdn.at[:, :n_in].set(w_in[:, :n_in].astype(BF16))
    wdn = wdn.at[:, n_in + MLA_NOPE_DIM:n_in + qk_dim].set(w_in[:, n_in:].astype(BF16))
    wq = _slot_rows(w_uq.astype(BF16), MLA_HEADS, qk_dim, lambda h: 0)
    wkv = w_ukv.astype(BF16).reshape(MLA_KV_LORA, MLA_HEADS, kv_dim)
    wk = _slot_rows(wkv[:, :, :MLA_NOPE_DIM].reshape(MLA_KV_LORA, -1), MLA_HEADS, MLA_NOPE_DIM, lambda h: 0).T
    wv = wkv[:, :, MLA_NOPE_DIM:].reshape(MLA_KV_LORA, MLA_HEADS * MLA_V_DIM).T
    weights = (wdn, wq, wk, wv)
    gains = (cq_gain.reshape(1, -1).astype(F32), ckv_gain.reshape(1, -1).astype(F32))

    cq, sq = _mla_tables(MLA_SCALE * LOG2E, SEQ, True)
    qt, kk, kn, vt = _l1_proj(x, mod, None, weights, gains, (cq.T, sq.T), _mla_tables(1.0, SEQ, True),
                          tm=TM, with_q=True, name="l1_proj")
    kkc, knc, vtc = _l1_proj(xc, mod, CTX_MOD_ROW, weights, gains, None, _mla_tables(1.0, CTX_LEN, False),
                        tm=TMC, with_q=False, name="l1_proj_ctx")
    return _attend(qt, kk, vt, kkc, vtc, kn, knc, heads=MLA_HEADS, k_slot=list(range(MLA_HEADS)),
                   v_row=[h * V_ROWS for h in range(MLA_HEADS)], tq=TQ, tk=TK, name="mla")


def _post_mixer(x, y, mod, mod_row, w_out, ln1, w_up, conv_w, conv_b, w_down, ln2, tm, tag):
    wo = w_out.astype(BF16)
    if isinstance(y, tuple):
        n0 = y[0].shape[2]
        extra = dict(x2=y[1], w2=wo[n0:])
        y, wo = y[0], wo[:n0]
    else:
        extra = {}
    x = _proj(y, wo, name="out_ln" + tag, tm=tm, tn=D_MODEL, mod=mod, mod_row=mod_row, gate_row=G1, resid=x,
              ln=ln1, **extra)
    wup = w_up.astype(BF16)
    return _ffn(x, mod, mod_row, wup[:, :D_FF], wup[:, D_FF:], w_down.astype(BF16), conv_w, conv_b,
                ln2[0], ln2[1], tm=tm, name="ffn" + tag)


def kernel(x, c, ctx, c_ctx, l0_w_ada, l0_b_ada, l0_w_in, l0_q_gain, l0_k_gain, l0_rpb, l0_w_out, l0_ln1_g, l0_ln1_b, l0_w_up, l0_conv_w, l0_conv_b, l0_w_down, l0_ln2_g, l0_ln2_b, l1_w_ada, l1_b_ada, l1_w_in, l1_cq_gain, l1_ckv_gain, l1_w_uq, l1_w_ukv, l1_w_out, l1_ln1_g, l1_ln1_b, l1_w_up, l1_conv_w, l1_conv_b, l1_w_down, l1_ln2_g, l1_ln2_b):
    xc = ctx
    mod = _modvec(c, c_ctx, l0_w_ada, l0_b_ada)
    y, yc = _layer0_mixer(x, xc, mod, l0_w_in, l0_q_gain, l0_k_gain, l0_rpb)
    post0 = (l0_w_out, (l0_ln1_g, l0_ln1_b), l0_w_up, l0_conv_w, l0_conv_b, l0_w_down, (l0_ln2_g, l0_ln2_b))
    x = _post_mixer(x, y, mod, None, *post0, tm=TM, tag="0")
    xc = _post_mixer(xc, yc, mod, CTX_MOD_ROW, *post0, tm=TMC, tag="0_ctx")
    mod = _modvec(c, c_ctx, l1_w_ada, l1_b_ada)
    y = _layer1_mixer(x, xc, mod, l1_w_in, l1_cq_gain, l1_ckv_gain, l1_w_uq, l1_w_ukv)
    post1 = (l1_w_out, (l1_ln1_g, l1_ln1_b), l1_w_up, l1_conv_w, l1_conv_b, l1_w_down, (l1_ln2_g, l1_ln2_b))
    return _post_mixer(x, y, mod, None, *post1, tm=TM, tag="1")
```

```python
import functools
import math

import numpy as np
import jax
import jax.numpy as jnp
from jax import lax
from jax.experimental import pallas as pl
from jax.experimental.pallas import tpu as pltpu

D_MODEL = 1024
BATCH = 4
SEQ = 4096
DEPTH = 2
GRID_W = 64
GRID_H = SEQ // GRID_W
CTX_LEN = 256
HEAD_DIM = 64
A_HEADS = 8
A_KV_HEADS = 2
A_GROUP = A_HEADS // A_KV_HEADS
B_HEADS = 8
NA_ROWS = 8
NA_COLS = 16
ROPE_THETA = 10000.0
MLA_HEADS = 16
MLA_Q_LORA = 768
MLA_KV_LORA = 256
MLA_NOPE_DIM = 64
MLA_ROPE_DIM = 32
MLA_V_DIM = 64
D_FF = 2816
N_MOD = 6
EPS = 1e-6
DEEPNORM_ALPHA = (2 * DEPTH) ** 0.25
HEAD_SCALE = HEAD_DIM ** -0.5
MLA_SCALE = (MLA_NOPE_DIM + MLA_ROPE_DIM) ** -0.5
LOG2E = math.log2(math.e)

LANES = 128
HALF = LANES // 2
BF16_ROWS = 16
V_ROWS = HEAD_DIM + BF16_ROWS
MASK_VALUE = -1e30
VMEM_LIMIT = 56 * 1024 * 1024

SH1, SC1, G1, SH2, SC2, G2 = range(N_MOD)

F32 = jnp.float32
BF16 = jnp.bfloat16

MOD_ROWS = 16
CTX_MOD_ROW = BATCH
TM = 512
TMC = CTX_LEN
TQ = 1024
TK = 512
AHEAD = 2
BOUND_SLACK = 1.0 + 1e-3
L_MIN_OK = 2.0 ** -80
FFN_ROWS = 256
FFN_HALO = BF16_ROWS
ROW_CHUNK = 512


def _cparams(sem):
    return pltpu.CompilerParams(dimension_semantics=sem, vmem_limit_bytes=VMEM_LIMIT)


def _dot(a, b):
    return jnp.dot(a, b, preferred_element_type=F32)


def _dot_nt(a, b):
    return lax.dot_general(a, b, (((1,), (1,)), ((), ())), preferred_element_type=F32)


def _pair_swap(y, axis):
    n = y.shape[axis]
    idx = lax.broadcasted_iota(jnp.int32, y.shape, axis)
    nxt = pltpu.roll(y, n - 1, axis=axis)
    prv = pltpu.roll(y, 1, axis=axis)
    return jnp.where(idx % 2 == 0, nxt, prv)


def _rotate(y, c, s, axis):
    return y * c + _pair_swap(y, axis) * s


def _layer_norm(z, g, b):
    mu = jnp.mean(z, axis=-1, keepdims=True)
    zc = z - mu
    var = jnp.mean(zc * zc, axis=-1, keepdims=True)
    return zc * lax.rsqrt(var + EPS) * g + b


def _slot_norm_rows(kb, n_slots):
    rows = []
    for sl in range(n_slots):
        kf = kb[:, sl * LANES:(sl + 1) * LANES].astype(F32)
        n2 = jnp.sum(kf * kf, axis=-1, keepdims=True)
        rows.append(jnp.broadcast_to(jnp.max(n2, axis=0, keepdims=True), (1, LANES)))
    return rows[0] if n_slots == 1 else jnp.concatenate(rows, axis=0)


def _store_vt(out_ref, yt, heads):
    ones = jnp.ones((BF16_ROWS, yt.shape[1]), out_ref.dtype)
    for h in range(heads):
        out_ref[0, h * V_ROWS:h * V_ROWS + HEAD_DIM, :] = yt[h * HEAD_DIM:(h + 1) * HEAD_DIM].astype(out_ref.dtype)
        out_ref[0, h * V_ROWS + HEAD_DIM:(h + 1) * V_ROWS, :] = ones


def _proj_body(*refs, silu, bias, resid_ln, gate_row, two):
    it = iter(refs)
    x_ref = next(it)
    w_ref = next(it)
    x2_ref = next(it) if two else None
    w2_ref = next(it) if two else None
    bias_ref = next(it) if bias else None
    mod_ref = next(it) if resid_ln else None
    xres_ref = next(it) if resid_ln else None
    lng_ref = next(it) if resid_ln else None
    lnb_ref = next(it) if resid_ln else None
    out_ref = next(it)
    x = x_ref[0]
    if silu:
        x = x * jax.nn.sigmoid(x)
    y = _dot(x.astype(BF16), w_ref[...].astype(BF16))
    if two:
        y = y + _dot(x2_ref[0], w2_ref[...])
    if bias:
        y = y + bias_ref[...]
    if resid_ln:
        g = mod_ref[0, gate_row:gate_row + 1, :]
        y = _layer_norm(DEEPNORM_ALPHA * xres_ref[0] + g * y, lng_ref[...], lnb_ref[...])
    out_ref[0] = y.astype(out_ref.dtype)


def _proj(x, w, *, name, tm, tn, x2=None, w2=None, silu=False, bias=None, mod=None, mod_row=None,
          gate_row=None, resid=None, ln=None, out_dtype=F32):
    b, t, k = x.shape
    n = w.shape[1]
    assert t % tm == 0 and n % tn == 0
    resid_ln = resid is not None
    in_specs = [pl.BlockSpec((1, tm, k), lambda bi, i, j: (bi, i, 0)),
                pl.BlockSpec((k, tn), lambda bi, i, j: (0, j))]
    args = [x, w]
    if x2 is not None:
        k2 = x2.shape[2]
        in_specs += [pl.BlockSpec((1, tm, k2), lambda bi, i, j: (bi, i, 0)),
                     pl.BlockSpec((k2, tn), lambda bi, i, j: (0, j))]
        args += [x2, w2]
    if bias is not None:
        in_specs.append(pl.BlockSpec((1, tn), lambda bi, i, j: (0, j)))
        args.append(bias.reshape(1, n).astype(F32))
    if resid_ln:
        mod_map = (lambda bi, i, j: (bi, 0, 0)) if mod_row is None else (lambda bi, i, j: (mod_row, 0, 0))
        in_specs += [pl.BlockSpec((1, N_MOD, D_MODEL), mod_map),
                     pl.BlockSpec((1, tm, n), lambda bi, i, j: (bi, i, 0)),
                     pl.BlockSpec((1, n), lambda bi, i, j: (0, 0)),
                     pl.BlockSpec((1, n), lambda bi, i, j: (0, 0))]
        args += [mod, resid, ln[0].reshape(1, n).astype(F32), ln[1].reshape(1, n).astype(F32)]
    body = functools.partial(_proj_body, silu=silu, bias=bias is not None, resid_ln=resid_ln, gate_row=gate_row,
                             two=x2 is not None)
    return pl.pallas_call(
        body, grid=(b, t // tm, n // tn), in_specs=in_specs,
        out_specs=pl.BlockSpec((1, tm, tn), lambda bi, i, j: (bi, i, j)),
        out_shape=jax.ShapeDtypeStruct((b, t, n), out_dtype), name=name,
        compiler_params=_cparams(("parallel", "parallel", "arbitrary")),
    )(*args)


def _modvec(c, c_ctx, w_ada, b_ada):
    cond = jnp.zeros((1, MOD_ROWS, D_MODEL), F32).at[0, :BATCH].set(c).at[0, BATCH].set(c_ctx)
    m = _proj(cond, w_ada, name="adaln", tm=MOD_ROWS, tn=1024, silu=True, bias=b_ada)
    return m.reshape(MOD_ROWS, N_MOD, D_MODEL)


def _l0_proj_body(x_ref, mod_ref, wqa_ref, cqa_ref, sqa_ref, wka_ref, cka_ref, ska_ref, wva_ref, wqb_ref,
                  wkb_ref, wvb_ref, qa_ref, ka_ref, kna_ref, va_ref, qb_ref, kb_ref, knb_ref, vb_ref):
    sc = mod_ref[0, SC1:SC1 + 1, :]
    sh = mod_ref[0, SH1:SH1 + 1, :]
    xs = (x_ref[0] * (1.0 + sc) + sh).astype(BF16)

    cq, sq = cqa_ref[...], sqa_ref[...]
    zeros = jnp.zeros((HALF, xs.shape[0]), qa_ref.dtype)
    yt = _dot_nt(wqa_ref[...], xs)
    for h in range(A_HEADS):
        seg = yt[h * HEAD_DIM:(h + 1) * HEAD_DIM]
        ms = jnp.mean(seg * seg, axis=0, keepdims=True)
        q = (_rotate(seg, cq, sq, 0) * lax.rsqrt(ms + EPS)).astype(qa_ref.dtype)
        half = h // A_GROUP
        qa_ref[0, h * LANES + half * HALF:h * LANES + (half + 1) * HALF, :] = q
        qa_ref[0, h * LANES + (1 - half) * HALF:h * LANES + (2 - half) * HALF, :] = zeros

    y = _dot(xs, wka_ref[...])
    lo = lax.broadcasted_iota(jnp.int32, y.shape, 1) < HALF
    ysq = y * y
    ms_lo = jnp.sum(jnp.where(lo, ysq, 0.0), axis=-1, keepdims=True) * (1.0 / HEAD_DIM)
    ms_hi = jnp.sum(jnp.where(lo, 0.0, ysq), axis=-1, keepdims=True) * (1.0 / HEAD_DIM)
    rn = jnp.where(lo, lax.rsqrt(ms_lo + EPS), lax.rsqrt(ms_hi + EPS))
    ka = (_rotate(y, cka_ref[...], ska_ref[...], 1) * rn).astype(ka_ref.dtype)
    ka_ref[0] = ka
    kna_ref[0, 0] = jnp.broadcast_to(_slot_norm_rows(ka, 1), kna_ref.shape[2:])

    _store_vt(va_ref, _dot_nt(wva_ref[...], xs), A_KV_HEADS)

    yt = _dot_nt(wqb_ref[...], xs) * (HEAD_SCALE * LOG2E)
    for h in range(B_HEADS):
        half = h % 2
        qb_ref[0, h * LANES + half * HALF:h * LANES + (half + 1) * HALF, :] = (
            yt[h * HEAD_DIM:(h + 1) * HEAD_DIM].astype(qb_ref.dtype))
        qb_ref[0, h * LANES + (1 - half) * HALF:h * LANES + (2 - half) * HALF, :] = zeros
    kb = _dot(xs, wkb_ref[...]).astype(kb_ref.dtype)
    kb_ref[0] = kb
    n_pairs = B_HEADS // 2
    knb_ref[0, 0] = jnp.concatenate([_slot_norm_rows(kb, n_pairs), jnp.zeros((8 - n_pairs, LANES), F32)], axis=0)
    _store_vt(vb_ref, _dot_nt(wvb_ref[...], xs), B_HEADS)


def _l0_proj(x, mod, mod_row, weights, tables, *, tm, name):
    b, t, d = x.shape
    wqa, wka, wva, wqb, wkb, wvb = weights
    cqa, sqa, cka, ska = tables
    mod_map = (lambda bi, i: (bi, 0, 0)) if mod_row is None else (lambda bi, i: (mod_row, 0, 0))

    def full(a):
        return pl.BlockSpec(a.shape, lambda bi, i: (0,) * a.ndim)

    in_specs = [pl.BlockSpec((1, tm, d), lambda bi, i: (bi, i, 0)),
                pl.BlockSpec((1, N_MOD, d), mod_map),
                full(wqa), pl.BlockSpec((HEAD_DIM, tm), lambda bi, i: (0, i)), pl.BlockSpec((HEAD_DIM, tm), lambda bi, i: (0, i)),
                full(wka), pl.BlockSpec((tm, LANES), lambda bi, i: (i, 0)), pl.BlockSpec((tm, LANES), lambda bi, i: (i, 0)),
                full(wva), full(wqb), full(wkb), full(wvb)]
    nb = B_HEADS * HEAD_DIM

    def nat(n):
        return jax.ShapeDtypeStruct((b, t, n), BF16), pl.BlockSpec((1, tm, n), lambda bi, i: (bi, i, 0))

    def tr(n):
        return jax.ShapeDtypeStruct((b, n, t), BF16), pl.BlockSpec((1, n, tm), lambda bi, i: (bi, 0, i))

    kn = (jax.ShapeDtypeStruct((b, t // tm, 8, LANES), F32), pl.BlockSpec((1, 1, 8, LANES), lambda bi, i: (bi, i, 0, 0)))
    outs = [tr(A_HEADS * LANES), nat(LANES), kn, tr(A_KV_HEADS * V_ROWS), tr(B_HEADS * LANES), nat(nb), kn, tr(B_HEADS * V_ROWS)]
    return pl.pallas_call(
        _l0_proj_body, grid=(b, t // tm), in_specs=in_specs,
        out_specs=[o[1] for o in outs], out_shape=[o[0] for o in outs], name=name,
        compiler_params=_cparams(("parallel", "parallel")),
    )(x, mod, wqa, cqa, sqa, wka, cka, ska, wva, wqb, wkb, wvb)


def _l1_proj_body(*refs, with_q):
    it = iter(refs)
    x_ref, mod_ref, wdn_ref, gq_ref, gkv_ref = (next(it) for _ in range(5))
    wq_ref, cq_ref, sq_ref = (next(it) for _ in range(3)) if with_q else (None, None, None)
    wk_ref, ck_ref, sk_ref, wv_ref = (next(it) for _ in range(4))
    q_ref = next(it) if with_q else None
    k_ref, kn_ref, v_ref = next(it), next(it), next(it)

    sc = mod_ref[0, SC1:SC1 + 1, :]
    sh = mod_ref[0, SH1:SH1 + 1, :]
    xs = (x_ref[0] * (1.0 + sc) + sh).astype(BF16)

    def rms(v, g_ref):
        ms = jnp.mean(v * v, axis=-1, keepdims=True)
        return (v * lax.rsqrt(ms + EPS) * g_ref[...]).astype(BF16)

    n_q, n_kv = MLA_Q_LORA, MLA_KV_LORA
    c_kv = rms(_dot(xs, wdn_ref[:, n_q:n_q + n_kv]), gkv_ref)
    k_r = _rotate(_dot(xs, wdn_ref[:, n_q + n_kv:]), ck_ref[...], sk_ref[...], 1)

    if with_q:
        c_q = rms(_dot(xs, wdn_ref[:, :n_q]), gq_ref)
        cq = jnp.concatenate([cq_ref[...]] * (ROW_CHUNK // LANES), axis=0)
        sq = jnp.concatenate([sq_ref[...]] * (ROW_CHUNK // LANES), axis=0)
        for r in range(MLA_HEADS * LANES // ROW_CHUNK):
            rows = slice(r * ROW_CHUNK, (r + 1) * ROW_CHUNK)
            q_ref[0, rows, :] = _rotate(_dot_nt(wq_ref[rows, :], c_q), cq, sq, 0).astype(q_ref.dtype)

    k_r_tiled = jnp.concatenate([k_r] * (ROW_CHUNK // LANES), axis=-1)
    for j in range(MLA_HEADS * LANES // ROW_CHUNK):
        cols = slice(j * ROW_CHUNK, (j + 1) * ROW_CHUNK)
        kb = (_dot(c_kv, wk_ref[:, cols]) + k_r_tiled).astype(k_ref.dtype)
        k_ref[0, :, cols] = kb
        n_sl = ROW_CHUNK // LANES
        kn_ref[0, 0, j * n_sl:(j + 1) * n_sl, :] = _slot_norm_rows(kb, n_sl)

    _store_vt(v_ref, _dot_nt(wv_ref[...], c_kv), MLA_HEADS)


def _l1_proj(x, mod, mod_row, weights, gains, q_tables, k_tables, *, tm, with_q, name):
    b, t, d = x.shape
    wdn, wq, wk, wv = weights
    mod_map = (lambda bi, i: (bi, 0, 0)) if mod_row is None else (lambda bi, i: (mod_row, 0, 0))

    def full(a):
        return pl.BlockSpec(a.shape, lambda bi, i: (0,) * a.ndim)

    in_specs = [pl.BlockSpec((1, tm, d), lambda bi, i: (bi, i, 0)), pl.BlockSpec((1, N_MOD, d), mod_map),
                full(wdn), full(gains[0]), full(gains[1])]
    args = [x, mod, wdn, gains[0], gains[1]]
    if with_q:
        in_specs += [full(wq), pl.BlockSpec((LANES, tm), lambda bi, i: (0, i)), pl.BlockSpec((LANES, tm), lambda bi, i: (0, i))]
        args += [wq, q_tables[0], q_tables[1]]
    in_specs += [full(wk), pl.BlockSpec((tm, LANES), lambda bi, i: (i, 0)), pl.BlockSpec((tm, LANES), lambda bi, i: (i, 0)), full(wv)]
    args += [wk, k_tables[0], k_tables[1], wv]
    out_shape, out_specs = [], []
    if with_q:
        out_shape.append(jax.ShapeDtypeStruct((b, MLA_HEADS * LANES, t), BF16))
        out_specs.append(pl.BlockSpec((1, MLA_HEADS * LANES, tm), lambda bi, i: (bi, 0, i)))
    out_shape += [jax.ShapeDtypeStruct((b, t, MLA_HEADS * LANES), BF16),
                  jax.ShapeDtypeStruct((b, t // tm, MLA_HEADS, LANES), F32),
                  jax.ShapeDtypeStruct((b, MLA_HEADS * V_ROWS, t), BF16)]
    out_specs += [pl.BlockSpec((1, tm, MLA_HEADS * LANES), lambda bi, i: (bi, i, 0)),
                  pl.BlockSpec((1, 1, MLA_HEADS, LANES), lambda bi, i: (bi, i, 0, 0)),
                  pl.BlockSpec((1, MLA_HEADS * V_ROWS, tm), lambda bi, i: (bi, 0, i))]
    return pl.pallas_call(
        functools.partial(_l1_proj_body, with_q=with_q), grid=(b, t // tm), in_specs=in_specs,
        out_specs=out_specs, out_shape=out_shape, name=name,
        compiler_params=_cparams(("parallel", "parallel")),
    )(*args)


def _dense_body(*refs, heads, k_slot, v_row, nk, has_ctx):
    if has_ctx:
        qt_ref, k_ref, vt_ref, kc_ref, vct_ref, out_ref, m_ref, l_ref, acc_ref = refs
    else:
        qt_ref, k_ref, vt_ref, out_ref, m_ref, l_ref, acc_ref = refs
    ki = pl.program_id(2)
    last = nk if has_ctx else nk - 1

    @pl.when(ki == 0)
    def _():
        m_ref[...] = jnp.full(m_ref.shape, MASK_VALUE, F32)
        l_ref[...] = jnp.zeros(l_ref.shape, F32)
        acc_ref[...] = jnp.zeros(acc_ref.shape, F32)

    def step(kr, vr):
        def scores(h):
            ks = k_slot[h]
            return _dot(kr[0, :, ks * LANES:(ks + 1) * LANES], qt_ref[0, h * LANES:(h + 1) * LANES, :])

        pending = [scores(h) for h in range(min(AHEAD, heads))]
        for h in range(heads):
            s = pending.pop(0)
            if h + AHEAD < heads:
                pending.append(scores(h + AHEAD))
            rows = slice(h * HEAD_DIM, (h + 1) * HEAD_DIM)
            m_prev = m_ref[h:h + 1, :]
            m_new = jnp.maximum(m_prev, jnp.max(s, axis=0, keepdims=True))
            alpha = jnp.exp2(m_prev - m_new)
            p = jnp.exp2(s - m_new).astype(BF16)
            pv = _dot(vr[0, v_row[h]:v_row[h] + V_ROWS, :], p)
            m_ref[h:h + 1, :] = m_new
            l_ref[h:h + 1, :] = alpha * l_ref[h:h + 1, :] + pv[HEAD_DIM:HEAD_DIM + 1]
            acc_ref[rows, :] = alpha * acc_ref[rows, :] + pv[:HEAD_DIM]

    if has_ctx:
        @pl.when(ki < nk)
        def _():
            step(k_ref, vt_ref)

        @pl.when(ki == nk)
        def _():
            step(kc_ref, vct_ref)
    else:
        step(k_ref, vt_ref)

    @pl.when(ki == last)
    def _():
        for h in range(heads):
            rows = slice(h * HEAD_DIM, (h + 1) * HEAD_DIM)
            acc_ref[rows, :] = acc_ref[rows, :] * (1.0 / l_ref[h:h + 1, :])
        out_ref[0] = acc_ref[...].T.astype(out_ref.dtype)


def _dense_attn(qt, k, vt, kc=None, vct=None, *, heads, k_slot, v_row, tq, tk, name):
    b, nq, lq = qt.shape
    _, lk, nkw = k.shape
    nv = vt.shape[1]
    assert nq == heads * LANES and lq % tq == 0 and lk % tk == 0
    nk = lk // tk
    has_ctx = kc is not None
    steps = nk + (1 if has_ctx else 0)
    in_specs = [
        pl.BlockSpec((1, nq, tq), lambda bi, qi, ki: (bi, 0, qi)),
        pl.BlockSpec((1, tk, nkw), lambda bi, qi, ki: (bi, jnp.minimum(ki, nk - 1), 0)),
        pl.BlockSpec((1, nv, tk), lambda bi, qi, ki: (bi, 0, jnp.minimum(ki, nk - 1))),
    ]
    args = [qt, k, vt]
    if has_ctx:
        lc = kc.shape[1]
        in_specs += [pl.BlockSpec((1, lc, nkw), lambda bi, qi, ki: (bi, 0, 0)),
                     pl.BlockSpec((1, nv, lc), lambda bi, qi, ki: (bi, 0, 0))]
        args += [kc, vct]
    body = functools.partial(_dense_body, heads=heads, k_slot=tuple(k_slot), v_row=tuple(v_row),
                             nk=nk, has_ctx=has_ctx)
    return pl.pallas_call(
        body, grid=(b, lq // tq, steps), in_specs=in_specs,
        out_specs=pl.BlockSpec((1, tq, heads * HEAD_DIM), lambda bi, qi, ki: (bi, qi, 0)),
        out_shape=jax.ShapeDtypeStruct((b, lq, heads * HEAD_DIM), BF16),
        scratch_shapes=[pltpu.VMEM((heads, tq), F32), pltpu.VMEM((heads, tq), F32),
                        pltpu.VMEM((heads * HEAD_DIM, tq), F32)],
        name=name, compiler_params=_cparams(("parallel", "parallel", "arbitrary")),
    )(*args)


def _dense_fast_body(qt_ref, k_ref, vt_ref, kc_ref, vct_ref, knl_ref, knc_ref, out_ref, lmin_ref,
                     b_ref, acc_ref, o_ref, *, heads, k_slot, v_row, nk):
    ki = pl.program_id(2)
    tq = qt_ref.shape[2]

    @pl.when(ki == 0)
    def _():
        kn = jnp.sqrt(jnp.maximum(jnp.max(knl_ref[0], axis=0), knc_ref[0, 0])) * BOUND_SLACK
        for h in range(heads):
            qf = qt_ref[0, h * LANES:(h + 1) * LANES, :].astype(F32)
            qn = jnp.sqrt(jnp.sum(qf * qf, axis=0, keepdims=True))
            krow = kn[k_slot[h]:k_slot[h] + 1, :]
            b_ref[h:h + 1, :] = qn * jnp.concatenate([krow] * (tq // LANES), axis=1)
        acc_ref[...] = jnp.zeros(acc_ref.shape, F32)

    def step(kr, vr):
        def scores(h):
            ks = k_slot[h]
            return _dot(kr[0, :, ks * LANES:(ks + 1) * LANES], qt_ref[0, h * LANES:(h + 1) * LANES, :])

        pending = [scores(h) for h in range(min(AHEAD, heads))]
        for h in range(heads):
            s = pending.pop(0)
            if h + AHEAD < heads:
                pending.append(scores(h + AHEAD))
            p = jnp.exp2(s - b_ref[h:h + 1, :]).astype(BF16)
            rows = slice(h * V_ROWS, (h + 1) * V_ROWS)
            acc_ref[rows, :] += _dot(vr[0, v_row[h]:v_row[h] + V_ROWS, :], p)

    @pl.when(ki < nk)
    def _():
        step(k_ref, vt_ref)

    @pl.when(ki == nk)
    def _():
        step(kc_ref, vct_ref)
        lmin = None
        for h in range(heads):
            l = acc_ref[h * V_ROWS + HEAD_DIM:h * V_ROWS + HEAD_DIM + 1, :]
            lmin = l if lmin is None else jnp.minimum(lmin, l)
            o_ref[h * HEAD_DIM:(h + 1) * HEAD_DIM, :] = acc_ref[h * V_ROWS:h * V_ROWS + HEAD_DIM, :] * (1.0 / l)
        out_ref[0] = o_ref[...].T.astype(out_ref.dtype)
        lmin_ref[0, 0] = jnp.broadcast_to(jnp.min(lmin, axis=1, keepdims=True), lmin_ref.shape[2:])


def _dense_attn_fast(qt, k, vt, kc, vct, knl, knc, *, heads, k_slot, v_row, tq, tk, name):
    b, nq, lq = qt.shape
    _, lk, nkw = k.shape
    nv = vt.shape[1]
    lc = kc.shape[1]
    nk = lk // tk
    assert nq == heads * LANES and lq % tq == 0 and lk % tk == 0 and knl.shape[1] == nk
    n_slots = knl.shape[2]
    in_specs = [
        pl.BlockSpec((1, nq, tq), lambda bi, qi, ki: (bi, 0, qi)),
        pl.BlockSpec((1, tk, nkw), lambda bi, qi, ki: (bi, jnp.minimum(ki, nk - 1), 0)),
        pl.BlockSpec((1, nv, tk), lambda bi, qi, ki: (bi, 0, jnp.minimum(ki, nk - 1))),
        pl.BlockSpec((1, lc, nkw), lambda bi, qi, ki: (bi, 0, 0)),
        pl.BlockSpec((1, nv, lc), lambda bi, qi, ki: (bi, 0, 0)),
        pl.BlockSpec((1, nk, n_slots, LANES), lambda bi, qi, ki: (bi, 0, 0, 0)),
        pl.BlockSpec((1, 1, n_slots, LANES), lambda bi, qi, ki: (bi, 0, 0, 0)),
    ]
    body = functools.partial(_dense_fast_body, heads=heads, k_slot=tuple(k_slot), v_row=tuple(v_row), nk=nk)
    return pl.pallas_call(
        body, grid=(b, lq // tq, nk + 1), in_specs=in_specs,
        out_specs=[pl.BlockSpec((1, tq, heads * HEAD_DIM), lambda bi, qi, ki: (bi, qi, 0)),
                   pl.BlockSpec((1, 1, 8, LANES), lambda bi, qi, ki: (bi, qi, 0, 0))],
        out_shape=[jax.ShapeDtypeStruct((b, lq, heads * HEAD_DIM), BF16),
                   jax.ShapeDtypeStruct((b, lq // tq, 8, LANES), F32)],
        scratch_shapes=[pltpu.VMEM((heads, tq), F32), pltpu.VMEM((heads * V_ROWS, tq), F32),
                        pltpu.VMEM((heads * HEAD_DIM, tq), F32)],
        name=name, compiler_params=_cparams(("parallel", "parallel", "arbitrary")),
    )(qt, k, vt, kc, vct, knl, knc)


def _attend(qt, k, vt, kc, vct, knl, knc, *, name, **kw):
    y, lmin = _dense_attn_fast(qt, k, vt, kc, vct, knl, knc, name=name + "_fast", **kw)
    return lax.cond(jnp.min(lmin) > L_MIN_OK, lambda: y,
                    lambda: _dense_attn(qt, k, vt, kc, vct, name=name, **kw))


NA_QROWS = 4
NA_TQ = NA_QROWS * GRID_W
NA_WROWS = 12
NA_WBLOCKS = NA_WROWS // NA_QROWS
NA_STEPS = GRID_H // NA_QROWS


def _natten_body(qt_ref, k0_ref, k1_ref, k2_ref, v0_ref, v1_ref, v2_ref, kc_ref, vct_ref, bias_ref,
                 out_ref, o_ref):
    k_refs = (k0_ref, k1_ref, k2_ref)
    v_refs = (v0_ref, v1_ref, v2_ref)

    def scores(h):
        lanes = slice((h // 2) * LANES, (h // 2 + 1) * LANES)
        qq = qt_ref[0, h * LANES:(h + 1) * LANES, :]
        return [_dot(kc_ref[0, :, lanes], qq)] + [_dot(k_refs[blk][0, :, lanes], qq) for blk in range(NA_WBLOCKS)]

    pending = [scores(h) for h in range(AHEAD)]
    for h in range(B_HEADS):
        s_all = pending.pop(0)
        if h + AHEAD < B_HEADS:
            pending.append(scores(h + AHEAD))
        vrows = slice(h * V_ROWS, (h + 1) * V_ROWS)
        s_all = [s_all[0]] + [s_all[1 + blk] + bias_ref[0, h, blk * NA_TQ:(blk + 1) * NA_TQ, :]
                              for blk in range(NA_WBLOCKS)]
        m = s_all[0].max(axis=0, keepdims=True)
        for s in s_all[1:]:
            m = jnp.maximum(m, s.max(axis=0, keepdims=True))
        pv = _dot(vct_ref[0, vrows, :], jnp.exp2(s_all[0] - m).astype(BF16))
        for blk in range(NA_WBLOCKS):
            pv = pv + _dot(v_refs[blk][0, vrows, :], jnp.exp2(s_all[1 + blk] - m).astype(BF16))
        o_ref[h * HEAD_DIM:(h + 1) * HEAD_DIM, :] = pv[:HEAD_DIM] * (1.0 / pv[HEAD_DIM:HEAD_DIM + 1])
    out_ref[0] = o_ref[...].T.astype(out_ref.dtype)


def _natten_fast_body(qt_ref, k0_ref, k1_ref, k2_ref, v0_ref, v1_ref, v2_ref, kc_ref, vct_ref, bias_ref,
                      knl_ref, knc_ref, bmax_ref, out_ref, lmin_ref, o_ref):
    k_refs = (k0_ref, k1_ref, k2_ref)
    v_refs = (v0_ref, v1_ref, v2_ref)
    kn = jnp.sqrt(jnp.maximum(jnp.max(knl_ref[0], axis=0), knc_ref[0, 0])) * BOUND_SLACK

    def scores(h):
        lanes = slice((h // 2) * LANES, (h // 2 + 1) * LANES)
        qq = qt_ref[0, h * LANES:(h + 1) * LANES, :]
        return [_dot(kc_ref[0, :, lanes], qq)] + [_dot(k_refs[blk][0, :, lanes], qq) for blk in range(NA_WBLOCKS)]

    pending = [scores(h) for h in range(AHEAD)]
    lmin = None
    for h in range(B_HEADS):
        s_all = pending.pop(0)
        if h + AHEAD < B_HEADS:
            pending.append(scores(h + AHEAD))
        vrows = slice(h * V_ROWS, (h + 1) * V_ROWS)
        qf = qt_ref[0, h * LANES:(h + 1) * LANES, :].astype(F32)
        qn = jnp.sqrt(jnp.sum(qf * qf, axis=0, keepdims=True))
        krow = kn[h // 2:h // 2 + 1, :]
        shift = qn * jnp.concatenate([krow] * (NA_TQ // LANES), axis=1) + bmax_ref[h:h + 1, :]
        pv = _dot(vct_ref[0, vrows, :], jnp.exp2(s_all[0] - shift).astype(BF16))
        for blk in range(NA_WBLOCKS):
            sb = s_all[1 + blk] + (bias_ref[0, h, blk * NA_TQ:(blk + 1) * NA_TQ, :] - shift)
            pv = pv + _dot(v_refs[blk][0, vrows, :], jnp.exp2(sb).astype(BF16))
        l = pv[HEAD_DIM:HEAD_DIM + 1]
        lmin = l if lmin is None else jnp.minimum(lmin, l)
        o_ref[h * HEAD_DIM:(h + 1) * HEAD_DIM, :] = pv[:HEAD_DIM] * (1.0 / l)
    out_ref[0] = o_ref[...].T.astype(out_ref.dtype)
    lmin_ref[0, 0] = jnp.broadcast_to(jnp.min(lmin, axis=1, keepdims=True), lmin_ref.shape[2:])


def _natten_call(body, name, qt, k, vt, kc, vct, bias, extra=(), with_lmin=False):
    b = qt.shape[0]
    nw = B_HEADS * HEAD_DIM
    nv = B_HEADS * V_ROWS

    def wstart(i):
        return jnp.clip(i - 1, 0, NA_STEPS - NA_WBLOCKS)

    def cls(i):
        return jnp.where(i == 0, 0, jnp.where(i == NA_STEPS - 1, 2, 1))

    in_specs = [pl.BlockSpec((1, B_HEADS * LANES, NA_TQ), lambda bi, i: (bi, 0, i))]
    in_specs += [pl.BlockSpec((1, NA_TQ, nw), functools.partial(lambda bi, i, j: (bi, wstart(i) + j, 0), j=j))
                 for j in range(NA_WBLOCKS)]
    in_specs += [pl.BlockSpec((1, nv, NA_TQ), functools.partial(lambda bi, i, j: (bi, 0, wstart(i) + j), j=j))
                 for j in range(NA_WBLOCKS)]
    in_specs += [pl.BlockSpec((1, CTX_LEN, nw), lambda bi, i: (bi, 0, 0)),
                 pl.BlockSpec((1, nv, CTX_LEN), lambda bi, i: (bi, 0, 0)),
                 pl.BlockSpec((1, B_HEADS, NA_WROWS * GRID_W, NA_TQ), lambda bi, i: (cls(i), 0, 0, 0))]
    for a, per_batch in extra:
        blk = (1,) + a.shape[1:] if per_batch else a.shape
        nz = len(blk) - 1
        in_specs.append(pl.BlockSpec(blk, (lambda bi, i, nz=nz: (bi,) + (0,) * nz) if per_batch
                                     else (lambda bi, i, nz=nz: (0,) * (nz + 1))))
    out_specs = [pl.BlockSpec((1, NA_TQ, nw), lambda bi, i: (bi, i, 0))]
    out_shape = [jax.ShapeDtypeStruct((b, SEQ, nw), BF16)]
    if with_lmin:
        out_specs.append(pl.BlockSpec((1, 1, 8, LANES), lambda bi, i: (bi, i, 0, 0)))
        out_shape.append(jax.ShapeDtypeStruct((b, NA_STEPS, 8, LANES), F32))
    return pl.pallas_call(
        body, grid=(b, NA_STEPS), in_specs=in_specs, out_specs=out_specs, out_shape=out_shape,
        scratch_shapes=[pltpu.VMEM((nw, NA_TQ), F32)],
        name=name, compiler_params=_cparams(("parallel", "arbitrary")),
    )(qt, k, k, k, vt, vt, vt, kc, vct, bias, *[a for a, _ in extra])


def _natten(qt, k, vt, kc, vct, rpb, knl, knc):
    bias = _natten_bias(rpb)
    bmax = jnp.maximum(jnp.max(rpb.astype(F32), axis=(1, 2)), 0.0) * LOG2E
    bmax = jnp.broadcast_to(bmax[:, None], (B_HEADS, NA_TQ))
    y, lmin = _natten_call(_natten_fast_body, "natten_fast", qt, k, vt, kc, vct, bias,
                           extra=((knl, True), (knc, True), (bmax, False)), with_lmin=True)
    return lax.cond(jnp.min(lmin) > L_MIN_OK, lambda: y,
                    lambda: _natten_call(_natten_body, "natten", qt, k, vt, kc, vct, bias)[0])


def _natten_bias(rpb):
    v = rpb.astype(F32) * LOG2E
    h, nr, _ = v.shape
    period = 2 * GRID_W
    row = jnp.zeros((h, nr, period), F32)
    row = row.at[..., :NA_COLS].set(v[..., NA_COLS - 1::-1])
    row = row.at[..., period - (NA_COLS - 1):].set(v[..., :NA_COLS - 1:-1])
    toep = jnp.tile(row, (1, 1, GRID_W))[..., :GRID_W * (period - 1)]
    toep = toep.reshape(h, nr, GRID_W, period - 1)[..., :GRID_W]
    col = np.arange(GRID_W)
    cs = np.clip(col - NA_COLS // 2, 0, GRID_W - NA_COLS)
    col_ok = (col[:, None] >= cs[None, :]) & (col[:, None] < cs[None, :] + NA_COLS)
    masked = 2.0 * MASK_VALUE
    toep = jnp.where(jnp.asarray(col_ok), toep, masked)
    dead = jnp.full((h, GRID_W, GRID_W), masked, F32)
    tables = []
    for r0 in (0, 2 * NA_QROWS, GRID_H - NA_QROWS):
        ws = min(max(r0 - NA_QROWS, 0), GRID_H - NA_WROWS)
        key_rows = []
        for ki in range(NA_WROWS):
            blocks = []
            for qi in range(NA_QROWS):
                k_row, q_row = ws + ki, r0 + qi
                rs = min(max(q_row - NA_ROWS // 2, 0), GRID_H - NA_ROWS)
                blocks.append(toep[:, k_row - q_row + NA_ROWS - 1] if rs <= k_row < rs + NA_ROWS else dead)
            key_rows.append(jnp.concatenate(blocks, axis=-1))
        tables.append(jnp.concatenate(key_rows, axis=-2))
    return jnp.stack(tables)


def _ffn_body(x_ref, xp_ref, xn_ref, mod_ref, wg_ref, wv_ref, wd_ref, cw_ref, cb_ref, lng_ref, lnb_ref,
              out_ref, hs_ref, g_ref, *, tm, rc, n_tiles):
    i = pl.program_id(1)
    sc = mod_ref[0, SC2:SC2 + 1, :]
    sh = mod_ref[0, SH2:SH2 + 1, :]
    gate_mod = mod_ref[0, G2:G2 + 1, :]

    def modulate(v):
        return v * (1.0 + sc) + sh

    hp = jnp.where(i > 0, modulate(xp_ref[0]), 0.0)
    hn = jnp.where(i < n_tiles - 1, modulate(xn_ref[0]), 0.0)
    hs_ref[0:FFN_HALO, :] = hp.astype(BF16)
    hs_ref[FFN_HALO:FFN_HALO + tm, :] = modulate(x_ref[0]).astype(BF16)
    hs_ref[FFN_HALO + tm:, :] = hn.astype(BF16)

    def up(c):
        lo = c * rc
        g_ref[c] = _dot(hs_ref[lo:lo + rc + 2 * FFN_HALO, :], wg_ref[...])
        return _dot(hs_ref[FFN_HALO + lo:FFN_HALO + lo + rc, :], wv_ref[...])

    n_chunks = tm // rc
    val_next = up(0)
    for c in range(n_chunks):
        val = val_next
        if c + 1 < n_chunks:
            val_next = up(c + 1)
        gate = (cw_ref[0:1, :] * g_ref[c, FFN_HALO - 1:FFN_HALO - 1 + rc, :]
                + cw_ref[1:2, :] * g_ref[c, FFN_HALO:FFN_HALO + rc, :]
                + cw_ref[2:3, :] * g_ref[c, FFN_HALO + 1:FFN_HALO + 1 + rc, :]
                + cb_ref[...])
        act = gate * jax.nn.sigmoid(gate) * val
        y = _dot(act.astype(BF16), wd_ref[...])
        rows = slice(c * rc, (c + 1) * rc)
        out_ref[0, rows, :] = _layer_norm(DEEPNORM_ALPHA * x_ref[0, rows, :] + gate_mod * y, lng_ref[...], lnb_ref[...])


def _ffn(x, mod, mod_row, wup, wd, conv_w, conv_b, ln_g, ln_b, *, tm, name):
    b, t, d = x.shape
    n_tiles = t // tm
    rc = min(FFN_ROWS, tm)
    hb = tm // FFN_HALO
    n_hblocks = t // FFN_HALO
    mod_map = (lambda bi, i: (bi, 0, 0)) if mod_row is None else (lambda bi, i: (mod_row, 0, 0))

    def resident(shape, col=0):
        return pl.BlockSpec(shape, lambda bi, i: (0, col), pipeline_mode=pl.Buffered(1))

    in_specs = [
        pl.BlockSpec((1, tm, d), lambda bi, i: (bi, i, 0)),
        pl.BlockSpec((1, FFN_HALO, d), lambda bi, i: (bi, jnp.maximum(i * hb - 1, 0), 0)),
        pl.BlockSpec((1, FFN_HALO, d), lambda bi, i: (bi, jnp.minimum((i + 1) * hb, n_hblocks - 1), 0)),
        pl.BlockSpec((1, N_MOD, d), mod_map),
        resident((d, D_FF), 0), resident((d, D_FF), 1), resident((D_FF, d)),
        resident((3, D_FF)), resident((1, D_FF)), resident((1, d)), resident((1, d)),
    ]
    body = functools.partial(_ffn_body, tm=tm, rc=rc, n_tiles=n_tiles)
    return pl.pallas_call(
        body, grid=(b, n_tiles), in_specs=in_specs,
        out_specs=pl.BlockSpec((1, tm, d), lambda bi, i: (bi, i, 0)),
        out_shape=jax.ShapeDtypeStruct((b, t, d), F32),
        scratch_shapes=[pltpu.VMEM((tm + 2 * FFN_HALO, d), BF16),
                        pltpu.VMEM((tm // rc, rc + 2 * FFN_HALO, D_FF), F32)],
        name=name, compiler_params=_cparams(("parallel", "parallel")),
    )(x, x, x, mod, wup, wup, wd, conv_w.astype(F32), conv_b.reshape(1, D_FF).astype(F32),
      ln_g.reshape(1, d).astype(F32), ln_b.reshape(1, d).astype(F32))


def _rope_cos_sin(rot_dim):
    pos = jnp.arange(SEQ, dtype=jnp.int32)
    rows = (pos // GRID_W).astype(F32)
    cols = (pos % GRID_W).astype(F32)
    axis_dim = rot_dim // 2
    inv = ROPE_THETA ** (-jnp.arange(0, axis_dim, 2, dtype=F32) / axis_dim)
    ang = jnp.concatenate([rows[:, None] * inv, cols[:, None] * inv], axis=-1)
    cos = jnp.repeat(jnp.cos(ang), 2, axis=-1)
    sin = jnp.repeat(jnp.sin(ang), 2, axis=-1) * jnp.tile(jnp.array([-1.0, 1.0], F32), rot_dim // 2)
    return cos, sin


def _swap_pairs(g):
    return g.reshape(-1, 2)[:, ::-1].reshape(-1)


def _gqa_tables(gain, scale, length, rope):
    if rope:
        cos, sin = _rope_cos_sin(HEAD_DIM)
    else:
        cos, sin = jnp.ones((length, HEAD_DIM), F32), jnp.zeros((length, HEAD_DIM), F32)
    c = cos * (gain * scale)
    s = sin * (_swap_pairs(gain) * scale)
    return jnp.concatenate([c, c], axis=-1), jnp.concatenate([s, s], axis=-1)


def _mla_tables(scale, length, rope):
    c = jnp.full((length, LANES), scale, F32)
    s = jnp.zeros((length, LANES), F32)
    if rope:
        cos, sin = _rope_cos_sin(MLA_ROPE_DIM)
        lo, hi = MLA_NOPE_DIM, MLA_NOPE_DIM + MLA_ROPE_DIM
        c = c.at[:, lo:hi].set(cos * scale)
        s = s.at[:, lo:hi].set(sin * scale)
    return c, s


def _slot_rows(w, heads, width, half_of):
    k = w.shape[0]
    out = jnp.zeros((heads * LANES, k), w.dtype)
    for h in range(heads):
        off = h * LANES + half_of(h) * HALF
        out = out.at[off:off + width].set(w[:, h * width:(h + 1) * width].T)
    return out


def _layer0_mixer(x, xc, mod, w_in, q_gain, k_gain, rpb):
    w = w_in.astype(BF16)
    aq, akv, bw = A_HEADS * HEAD_DIM, A_KV_HEADS * HEAD_DIM, B_HEADS * HEAD_DIM
    o = 0
    wqa = w[:, o:o + aq].T; o += aq
    wka = w[:, o:o + akv]; o += akv
    wva = w[:, o:o + akv].T; o += akv
    wqb = w[:, o:o + bw].T; o += bw
    wkb = w[:, o:o + bw]; o += bw
    wvb = w[:, o:o + bw].T
    weights = (wqa, wka, wva, wqb, wkb, wvb)
    qscale = HEAD_SCALE * LOG2E

    def tables(length, rope):
        cq, sq = _gqa_tables(q_gain, qscale, length, rope)
        ck, sk = _gqa_tables(k_gain, 1.0, length, rope)
        return cq[:, :HEAD_DIM].T, sq[:, :HEAD_DIM].T, ck, sk

    qa_t, ka, kna, va_t, qb_t, kb, knb, vb_t = _l0_proj(x, mod, None, weights, tables(SEQ, True), tm=TM, name="l0_proj")
    qac_t, kac, knac, vac_t, qbc_t, kbc, knbc, vbc_t = _l0_proj(xc, mod, CTX_MOD_ROW, weights, tables(CTX_LEN, False),
                                                    tm=TMC, name="l0_proj_ctx")

    a_kslot = [0] * A_HEADS
    a_vrow = [(h // A_GROUP) * V_ROWS for h in range(A_HEADS)]
    b_kslot = [h // 2 for h in range(B_HEADS)]
    b_vrow = [h * V_ROWS for h in range(B_HEADS)]
    ya = _attend(qa_t, ka, va_t, kac, vac_t, kna, knac, heads=A_HEADS, k_slot=a_kslot, v_row=a_vrow,
                 tq=TQ, tk=TK, name="gqa")
    yb = _natten(qb_t, kb, vb_t, kbc, vbc_t, rpb, knb, knbc)
    yac = _dense_attn(qac_t, kac, vac_t, heads=A_HEADS, k_slot=a_kslot, v_row=a_vrow,
                      tq=CTX_LEN, tk=CTX_LEN, name="gqa_ctx")
    ybc = _dense_attn(qbc_t, kbc, vbc_t, heads=B_HEADS, k_slot=b_kslot, v_row=b_vrow,
                      tq=CTX_LEN, tk=CTX_LEN, name="nbr_ctx")
    return (ya, yb), (yac, ybc)


def _layer1_mixer(x, xc, mod, w_in, cq_gain, ckv_gain, w_uq, w_ukv):
    qk_dim = MLA_NOPE_DIM + MLA_ROPE_DIM
    kv_dim = MLA_NOPE_DIM + MLA_V_DIM
    n_in = MLA_Q_LORA + MLA_KV_LORA
    wdn = jnp.zeros((D_MODEL, n_in + LANES), BF16)
    wdn = wdn.at[:, :n_in].set(w_in[:, :n_in].astype(BF16))
    wdn = wdn.at[:, n_in + MLA_NOPE_DIM:n_in + qk_dim].set(w_in[:, n_in:].astype(BF16))
    wq = _slot_rows(w_uq.astype(BF16), MLA_HEADS, qk_dim, lambda h: 0)
    wkv = w_ukv.astype(BF16).reshape(MLA_KV_LORA, MLA_HEADS, kv_dim)
    wk = _slot_rows(wkv[:, :, :MLA_NOPE_DIM].reshape(MLA_KV_LORA, -1), MLA_HEADS, MLA_NOPE_DIM, lambda h: 0).T
    wv = wkv[:, :, MLA_NOPE_DIM:].reshape(MLA_KV_LORA, MLA_HEADS * MLA_V_DIM).T
    weights = (wdn, wq, wk, wv)
    gains = (cq_gain.reshape(1, -1).astype(F32), ckv_gain.reshape(1, -1).astype(F32))

    cq, sq = _mla_tables(MLA_SCALE * LOG2E, SEQ, True)
    qt, kk, kn, vt = _l1_proj(x, mod, None, weights, gains, (cq.T, sq.T), _mla_tables(1.0, SEQ, True),
                          tm=TM, with_q=True, name="l1_proj")
    kkc, knc, vtc = _l1_proj(xc, mod, CTX_MOD_ROW, weights, gains, None, _mla_tables(1.0, CTX_LEN, False),
                        tm=TMC, with_q=False, name="l1_proj_ctx")
    return _attend(qt, kk, vt, kkc, vtc, kn, knc, heads=MLA_HEADS, k_slot=list(range(MLA_HEADS)),
                   v_row=[h * V_ROWS for h in range(MLA_HEADS)], tq=TQ, tk=TK, name="mla")


def _post_mixer(x, y, mod, mod_row, w_out, ln1, w_up, conv_w, conv_b, w_down, ln2, tm, tag):
    wo = w_out.astype(BF16)
    if isinstance(y, tuple):
        n0 = y[0].shape[2]
        extra = dict(x2=y[1], w2=wo[n0:])
        y, wo = y[0], wo[:n0]
    else:
        extra = {}
    x = _proj(y, wo, name="out_ln" + tag, tm=tm, tn=D_MODEL, mod=mod, mod_row=mod_row, gate_row=G1, resid=x,
              ln=ln1, **extra)
    return _ffn(x, mod, mod_row, w_up.astype(BF16), w_down.astype(BF16), conv_w, conv_b, ln2[0], ln2[1],
                tm=tm, name="ffn" + tag)


def kernel(x, c, ctx, c_ctx, l0_w_ada, l0_b_ada, l0_w_in, l0_q_gain, l0_k_gain, l0_rpb, l0_w_out, l0_ln1_g, l0_ln1_b, l0_w_up, l0_conv_w, l0_conv_b, l0_w_down, l0_ln2_g, l0_ln2_b, l1_w_ada, l1_b_ada, l1_w_in, l1_cq_gain, l1_ckv_gain, l1_w_uq, l1_w_ukv, l1_w_out, l1_ln1_g, l1_ln1_b, l1_w_up, l1_conv_w, l1_conv_b, l1_w_down, l1_ln2_g, l1_ln2_b):
    xc = ctx
    mod = _modvec(c, c_ctx, l0_w_ada, l0_b_ada)
    y, yc = _layer0_mixer(x, xc, mod, l0_w_in, l0_q_gain, l0_k_gain, l0_rpb)
    post0 = (l0_w_out, (l0_ln1_g, l0_ln1_b), l0_w_up, l0_conv_w, l0_conv_b, l0_w_down, (l0_ln2_g, l0_ln2_b))
    x = _post_mixer(x, y, mod, None, *post0, tm=TM, tag="0")
    xc = _post_mixer(xc, yc, mod, CTX_MOD_ROW, *post0, tm=TMC, tag="0_ctx")
    mod = _modvec(c, c_ctx, l1_w_ada, l1_b_ada)
    y = _layer1_mixer(x, xc, mod, l1_w_in, l1_cq_gain, l1_ckv_gain, l1_w_uq, l1_w_ukv)
    post1 = (l1_w_out, (l1_ln1_g, l1_ln1_b), l1_w_up, l1_conv_w, l1_conv_b, l1_w_down, (l1_ln2_g, l1_ln2_b))
    return _post_mixer(x, y, mod, None, *post1, tm=TM, tag="1")
```

```python
import functools
import math

import numpy as np
import jax
import jax.numpy as jnp
from jax import lax
from jax.experimental import pallas as pl
from jax.experimental.pallas import tpu as pltpu

D_MODEL = 1024
BATCH = 4
SEQ = 4096
DEPTH = 2
GRID_W = 64
GRID_H = SEQ // GRID_W
CTX_LEN = 256
HEAD_DIM = 64
A_HEADS = 8
A_KV_HEADS = 2
A_GROUP = A_HEADS // A_KV_HEADS
B_HEADS = 8
NA_ROWS = 8
NA_COLS = 16
ROPE_THETA = 10000.0
MLA_HEADS = 16
MLA_Q_LORA = 768
MLA_KV_LORA = 256
MLA_NOPE_DIM = 64
MLA_ROPE_DIM = 32
MLA_V_DIM = 64
D_FF = 2816
N_MOD = 6
EPS = 1e-6
DEEPNORM_ALPHA = (2 * DEPTH) ** 0.25
HEAD_SCALE = HEAD_DIM ** -0.5
MLA_SCALE = (MLA_NOPE_DIM + MLA_ROPE_DIM) ** -0.5
LOG2E = math.log2(math.e)

LANES = 128
HALF = LANES // 2
BF16_ROWS = 16
V_ROWS = HEAD_DIM + BF16_ROWS
MASK_VALUE = -1e30
VMEM_LIMIT = 56 * 1024 * 1024

SH1, SC1, G1, SH2, SC2, G2 = range(N_MOD)

F32 = jnp.float32
BF16 = jnp.bfloat16

MOD_ROWS = 16
CTX_MOD_ROW = BATCH
TM = 512
TMC = CTX_LEN
TQ = 1024
TK = 512
AHEAD = 2
BOUND_SLACK = 1.0 + 1e-3
L_MIN_OK = 2.0 ** -80
FFN_ROWS = 256
FFN_HALO = BF16_ROWS
ROW_CHUNK = 256


def _cparams(sem):
    return pltpu.CompilerParams(dimension_semantics=sem, vmem_limit_bytes=VMEM_LIMIT)


def _dot(a, b):
    return jnp.dot(a, b, preferred_element_type=F32)


def _dot_nt(a, b):
    return lax.dot_general(a, b, (((1,), (1,)), ((), ())), preferred_element_type=F32)


def _pair_swap(y, axis):
    n = y.shape[axis]
    idx = lax.broadcasted_iota(jnp.int32, y.shape, axis)
    nxt = pltpu.roll(y, n - 1, axis=axis)
    prv = pltpu.roll(y, 1, axis=axis)
    return jnp.where(idx % 2 == 0, nxt, prv)


def _rotate(y, c, s, axis):
    return y * c + _pair_swap(y, axis) * s


def _layer_norm(z, g, b):
    mu = jnp.mean(z, axis=-1, keepdims=True)
    zc = z - mu
    var = jnp.mean(zc * zc, axis=-1, keepdims=True)
    return zc * lax.rsqrt(var + EPS) * g + b


def _slot_norm_rows(kb, n_slots):
    rows = []
    for sl in range(n_slots):
        kf = kb[:, sl * LANES:(sl + 1) * LANES].astype(F32)
        n2 = jnp.sum(kf * kf, axis=-1, keepdims=True)
        rows.append(jnp.broadcast_to(jnp.max(n2, axis=0, keepdims=True), (1, LANES)))
    return rows[0] if n_slots == 1 else jnp.concatenate(rows, axis=0)


def _store_vt(out_ref, yt, heads):
    ones = jnp.ones((BF16_ROWS, yt.shape[1]), out_ref.dtype)
    for h in range(heads):
        out_ref[0, h * V_ROWS:h * V_ROWS + HEAD_DIM, :] = yt[h * HEAD_DIM:(h + 1) * HEAD_DIM].astype(out_ref.dtype)
        out_ref[0, h * V_ROWS + HEAD_DIM:(h + 1) * V_ROWS, :] = ones


def _proj_body(*refs, silu, bias, resid_ln, gate_row, two):
    it = iter(refs)
    x_ref = next(it)
    w_ref = next(it)
    x2_ref = next(it) if two else None
    w2_ref = next(it) if two else None
    bias_ref = next(it) if bias else None
    mod_ref = next(it) if resid_ln else None
    xres_ref = next(it) if resid_ln else None
    lng_ref = next(it) if resid_ln else None
    lnb_ref = next(it) if resid_ln else None
    out_ref = next(it)
    x = x_ref[0]
    if silu:
        x = x * jax.nn.sigmoid(x)
    y = _dot(x.astype(BF16), w_ref[...].astype(BF16))
    if two:
        y = y + _dot(x2_ref[0], w2_ref[...])
    if bias:
        y = y + bias_ref[...]
    if resid_ln:
        g = mod_ref[0, gate_row:gate_row + 1, :]
        y = _layer_norm(DEEPNORM_ALPHA * xres_ref[0] + g * y, lng_ref[...], lnb_ref[...])
    out_ref[0] = y.astype(out_ref.dtype)


def _proj(x, w, *, name, tm, tn, x2=None, w2=None, silu=False, bias=None, mod=None, mod_row=None,
          gate_row=None, resid=None, ln=None, out_dtype=F32):
    b, t, k = x.shape
    n = w.shape[1]
    assert t % tm == 0 and n % tn == 0
    resid_ln = resid is not None
    in_specs = [pl.BlockSpec((1, tm, k), lambda bi, i, j: (bi, i, 0)),
                pl.BlockSpec((k, tn), lambda bi, i, j: (0, j))]
    args = [x, w]
    if x2 is not None:
        k2 = x2.shape[2]
        in_specs += [pl.BlockSpec((1, tm, k2), lambda bi, i, j: (bi, i, 0)),
                     pl.BlockSpec((k2, tn), lambda bi, i, j: (0, j))]
        args += [x2, w2]
    if bias is not None:
        in_specs.append(pl.BlockSpec((1, tn), lambda bi, i, j: (0, j)))
        args.append(bias.reshape(1, n).astype(F32))
    if resid_ln:
        mod_map = (lambda bi, i, j: (bi, 0, 0)) if mod_row is None else (lambda bi, i, j: (mod_row, 0, 0))
        in_specs += [pl.BlockSpec((1, N_MOD, D_MODEL), mod_map),
                     pl.BlockSpec((1, tm, n), lambda bi, i, j: (bi, i, 0)),
                     pl.BlockSpec((1, n), lambda bi, i, j: (0, 0)),
                     pl.BlockSpec((1, n), lambda bi, i, j: (0, 0))]
        args += [mod, resid, ln[0].reshape(1, n).astype(F32), ln[1].reshape(1, n).astype(F32)]
    body = functools.partial(_proj_body, silu=silu, bias=bias is not None, resid_ln=resid_ln, gate_row=gate_row,
                             two=x2 is not None)
    return pl.pallas_call(
        body, grid=(b, t // tm, n // tn), in_specs=in_specs,
        out_specs=pl.BlockSpec((1, tm, tn), lambda bi, i, j: (bi, i, j)),
        out_shape=jax.ShapeDtypeStruct((b, t, n), out_dtype), name=name,
        compiler_params=_cparams(("parallel", "parallel", "arbitrary")),
    )(*args)


def _modvec(c, c_ctx, w_ada, b_ada):
    cond = jnp.zeros((1, MOD_ROWS, D_MODEL), F32).at[0, :BATCH].set(c).at[0, BATCH].set(c_ctx)
    m = _proj(cond, w_ada, name="adaln", tm=MOD_ROWS, tn=1024, silu=True, bias=b_ada)
    return m.reshape(MOD_ROWS, N_MOD, D_MODEL)


def _l0_proj_body(x_ref, mod_ref, wqa_ref, cqa_ref, sqa_ref, wka_ref, cka_ref, ska_ref, wqb_ref,
                  qa_ref, ka_ref, kna_ref, va_ref, qb_ref, kb_ref, knb_ref, vb_ref):
    sc = mod_ref[0, SC1:SC1 + 1, :]
    sh = mod_ref[0, SH1:SH1 + 1, :]
    xs = (x_ref[0] * (1.0 + sc) + sh).astype(BF16)

    cq, sq = cqa_ref[...], sqa_ref[...]
    zeros = jnp.zeros((HALF, xs.shape[0]), qa_ref.dtype)
    yt_a = _dot_nt(wqa_ref[...], xs)
    yt = yt_a[:A_HEADS * HEAD_DIM]
    for h in range(A_HEADS):
        seg = yt[h * HEAD_DIM:(h + 1) * HEAD_DIM]
        ms = jnp.mean(seg * seg, axis=0, keepdims=True)
        q = (_rotate(seg, cq, sq, 0) * lax.rsqrt(ms + EPS)).astype(qa_ref.dtype)
        half = h // A_GROUP
        qa_ref[0, h * LANES + half * HALF:h * LANES + (half + 1) * HALF, :] = q
        qa_ref[0, h * LANES + (1 - half) * HALF:h * LANES + (2 - half) * HALF, :] = zeros

    y_k = _dot(xs, wka_ref[...])
    y = y_k[:, :LANES]
    lo = lax.broadcasted_iota(jnp.int32, y.shape, 1) < HALF
    ysq = y * y
    ms_lo = jnp.sum(jnp.where(lo, ysq, 0.0), axis=-1, keepdims=True) * (1.0 / HEAD_DIM)
    ms_hi = jnp.sum(jnp.where(lo, 0.0, ysq), axis=-1, keepdims=True) * (1.0 / HEAD_DIM)
    rn = jnp.where(lo, lax.rsqrt(ms_lo + EPS), lax.rsqrt(ms_hi + EPS))
    ka = (_rotate(y, cka_ref[...], ska_ref[...], 1) * rn).astype(ka_ref.dtype)
    ka_ref[0] = ka
    kna_ref[0, 0] = jnp.broadcast_to(_slot_norm_rows(ka, 1), kna_ref.shape[2:])

    _store_vt(va_ref, yt_a[A_HEADS * HEAD_DIM:], A_KV_HEADS)

    yt_b = _dot_nt(wqb_ref[...], xs)
    yt = yt_b[:B_HEADS * HEAD_DIM] * (HEAD_SCALE * LOG2E)
    for h in range(B_HEADS):
        half = h % 2
        qb_ref[0, h * LANES + half * HALF:h * LANES + (half + 1) * HALF, :] = (
            yt[h * HEAD_DIM:(h + 1) * HEAD_DIM].astype(qb_ref.dtype))
        qb_ref[0, h * LANES + (1 - half) * HALF:h * LANES + (2 - half) * HALF, :] = zeros
    kb = y_k[:, LANES:].astype(kb_ref.dtype)
    kb_ref[0] = kb
    n_pairs = B_HEADS // 2
    knb_ref[0, 0] = jnp.concatenate([_slot_norm_rows(kb, n_pairs), jnp.zeros((8 - n_pairs, LANES), F32)], axis=0)
    _store_vt(vb_ref, yt_b[B_HEADS * HEAD_DIM:], B_HEADS)


def _l0_proj(x, mod, mod_row, weights, tables, *, tm, name):
    b, t, d = x.shape
    wqa, wka, wqb = weights
    cqa, sqa, cka, ska = tables
    mod_map = (lambda bi, i: (bi, 0, 0)) if mod_row is None else (lambda bi, i: (mod_row, 0, 0))

    def full(a):
        return pl.BlockSpec(a.shape, lambda bi, i: (0,) * a.ndim)

    in_specs = [pl.BlockSpec((1, tm, d), lambda bi, i: (bi, i, 0)),
                pl.BlockSpec((1, N_MOD, d), mod_map),
                full(wqa), pl.BlockSpec((HEAD_DIM, tm), lambda bi, i: (0, i)), pl.BlockSpec((HEAD_DIM, tm), lambda bi, i: (0, i)),
                full(wka), pl.BlockSpec((tm, LANES), lambda bi, i: (i, 0)), pl.BlockSpec((tm, LANES), lambda bi, i: (i, 0)),
                full(wqb)]
    nb = B_HEADS * HEAD_DIM

    def nat(n):
        return jax.ShapeDtypeStruct((b, t, n), BF16), pl.BlockSpec((1, tm, n), lambda bi, i: (bi, i, 0))

    def tr(n):
        return jax.ShapeDtypeStruct((b, n, t), BF16), pl.BlockSpec((1, n, tm), lambda bi, i: (bi, 0, i))

    kn = (jax.ShapeDtypeStruct((b, t // tm, 8, LANES), F32), pl.BlockSpec((1, 1, 8, LANES), lambda bi, i: (bi, i, 0, 0)))
    outs = [tr(A_HEADS * LANES), nat(LANES), kn, tr(A_KV_HEADS * V_ROWS), tr(B_HEADS * LANES), nat(nb), kn, tr(B_HEADS * V_ROWS)]
    return pl.pallas_call(
        _l0_proj_body, grid=(b, t // tm), in_specs=in_specs,
        out_specs=[o[1] for o in outs], out_shape=[o[0] for o in outs], name=name,
        compiler_params=_cparams(("parallel", "parallel")),
    )(x, mod, wqa, cqa, sqa, wka, cka, ska, wqb)


def _l1_proj_body(*refs, with_q):
    it = iter(refs)
    x_ref, mod_ref, wdn_ref, gq_ref, gkv_ref = (next(it) for _ in range(5))
    wq_ref, cq_ref, sq_ref = (next(it) for _ in range(3)) if with_q else (None, None, None)
    wk_ref, ck_ref, sk_ref, wv_ref = (next(it) for _ in range(4))
    q_ref = next(it) if with_q else None
    k_ref, kn_ref, v_ref = next(it), next(it), next(it)

    sc = mod_ref[0, SC1:SC1 + 1, :]
    sh = mod_ref[0, SH1:SH1 + 1, :]
    xs = (x_ref[0] * (1.0 + sc) + sh).astype(BF16)

    def rms(v, g_ref):
        ms = jnp.mean(v * v, axis=-1, keepdims=True)
        return (v * lax.rsqrt(ms + EPS) * g_ref[...]).astype(BF16)

    n_q, n_kv = MLA_Q_LORA, MLA_KV_LORA
    c_kv = rms(_dot(xs, wdn_ref[:, n_q:n_q + n_kv]), gkv_ref)
    k_r = _rotate(_dot(xs, wdn_ref[:, n_q + n_kv:]), ck_ref[...], sk_ref[...], 1)

    if with_q:
        c_q = rms(_dot(xs, wdn_ref[:, :n_q]), gq_ref)
        cq = jnp.concatenate([cq_ref[...]] * (ROW_CHUNK // LANES), axis=0)
        sq = jnp.concatenate([sq_ref[...]] * (ROW_CHUNK // LANES), axis=0)
        for r in range(MLA_HEADS * LANES // ROW_CHUNK):
            rows = slice(r * ROW_CHUNK, (r + 1) * ROW_CHUNK)
            q_ref[0, rows, :] = _rotate(_dot_nt(wq_ref[rows, :], c_q), cq, sq, 0).astype(q_ref.dtype)

    k_r_tiled = jnp.concatenate([k_r] * (ROW_CHUNK // LANES), axis=-1)
    for j in range(MLA_HEADS * LANES // ROW_CHUNK):
        cols = slice(j * ROW_CHUNK, (j + 1) * ROW_CHUNK)
        kb = (_dot(c_kv, wk_ref[:, cols]) + k_r_tiled).astype(k_ref.dtype)
        k_ref[0, :, cols] = kb
        n_sl = ROW_CHUNK // LANES
        kn_ref[0, 0, j * n_sl:(j + 1) * n_sl, :] = _slot_norm_rows(kb, n_sl)

    _store_vt(v_ref, _dot_nt(wv_ref[...], c_kv), MLA_HEADS)


def _l1_proj(x, mod, mod_row, weights, gains, q_tables, k_tables, *, tm, with_q, name):
    b, t, d = x.shape
    wdn, wq, wk, wv = weights
    mod_map = (lambda bi, i: (bi, 0, 0)) if mod_row is None else (lambda bi, i: (mod_row, 0, 0))

    def full(a):
        return pl.BlockSpec(a.shape, lambda bi, i: (0,) * a.ndim)

    in_specs = [pl.BlockSpec((1, tm, d), lambda bi, i: (bi, i, 0)), pl.BlockSpec((1, N_MOD, d), mod_map),
                full(wdn), full(gains[0]), full(gains[1])]
    args = [x, mod, wdn, gains[0], gains[1]]
    if with_q:
        in_specs += [full(wq), pl.BlockSpec((LANES, tm), lambda bi, i: (0, i)), pl.BlockSpec((LANES, tm), lambda bi, i: (0, i))]
        args += [wq, q_tables[0], q_tables[1]]
    in_specs += [full(wk), pl.BlockSpec((tm, LANES), lambda bi, i: (i, 0)), pl.BlockSpec((tm, LANES), lambda bi, i: (i, 0)), full(wv)]
    args += [wk, k_tables[0], k_tables[1], wv]
    out_shape, out_specs = [], []
    if with_q:
        out_shape.append(jax.ShapeDtypeStruct((b, MLA_HEADS * LANES, t), BF16))
        out_specs.append(pl.BlockSpec((1, MLA_HEADS * LANES, tm), lambda bi, i: (bi, 0, i)))
    out_shape += [jax.ShapeDtypeStruct((b, t, MLA_HEADS * LANES), BF16),
                  jax.ShapeDtypeStruct((b, t // tm, MLA_HEADS, LANES), F32),
                  jax.ShapeDtypeStruct((b, MLA_HEADS * V_ROWS, t), BF16)]
    out_specs += [pl.BlockSpec((1, tm, MLA_HEADS * LANES), lambda bi, i: (bi, i, 0)),
                  pl.BlockSpec((1, 1, MLA_HEADS, LANES), lambda bi, i: (bi, i, 0, 0)),
                  pl.BlockSpec((1, MLA_HEADS * V_ROWS, tm), lambda bi, i: (bi, 0, i))]
    return pl.pallas_call(
        functools.partial(_l1_proj_body, with_q=with_q), grid=(b, t // tm), in_specs=in_specs,
        out_specs=out_specs, out_shape=out_shape, name=name,
        compiler_params=_cparams(("parallel", "parallel")),
    )(*args)


def _dense_body(*refs, heads, k_slot, v_row, nk, has_ctx):
    if has_ctx:
        qt_ref, k_ref, vt_ref, kc_ref, vct_ref, out_ref, m_ref, l_ref, acc_ref = refs
    else:
        qt_ref, k_ref, vt_ref, out_ref, m_ref, l_ref, acc_ref = refs
    ki = pl.program_id(2)
    last = nk if has_ctx else nk - 1

    @pl.when(ki == 0)
    def _():
        m_ref[...] = jnp.full(m_ref.shape, MASK_VALUE, F32)
        l_ref[...] = jnp.zeros(l_ref.shape, F32)
        acc_ref[...] = jnp.zeros(acc_ref.shape, F32)

    def step(kr, vr):
        def scores(h):
            ks = k_slot[h]
            return _dot(kr[0, :, ks * LANES:(ks + 1) * LANES], qt_ref[0, h * LANES:(h + 1) * LANES, :])

        pending = [scores(h) for h in range(min(AHEAD, heads))]
        for h in range(heads):
            s = pending.pop(0)
            if h + AHEAD < heads:
                pending.append(scores(h + AHEAD))
            rows = slice(h * HEAD_DIM, (h + 1) * HEAD_DIM)
            m_prev = m_ref[h:h + 1, :]
            m_new = jnp.maximum(m_prev, jnp.max(s, axis=0, keepdims=True))
            alpha = jnp.exp2(m_prev - m_new)
            p = jnp.exp2(s - m_new).astype(BF16)
            pv = _dot(vr[0, v_row[h]:v_row[h] + V_ROWS, :], p)
            m_ref[h:h + 1, :] = m_new
            l_ref[h:h + 1, :] = alpha * l_ref[h:h + 1, :] + pv[HEAD_DIM:HEAD_DIM + 1]
            acc_ref[rows, :] = alpha * acc_ref[rows, :] + pv[:HEAD_DIM]

    if has_ctx:
        @pl.when(ki < nk)
        def _():
            step(k_ref, vt_ref)

        @pl.when(ki == nk)
        def _():
            step(kc_ref, vct_ref)
    else:
        step(k_ref, vt_ref)

    @pl.when(ki == last)
    def _():
        for h in range(heads):
            rows = slice(h * HEAD_DIM, (h + 1) * HEAD_DIM)
            acc_ref[rows, :] = acc_ref[rows, :] * (1.0 / l_ref[h:h + 1, :])
        out_ref[0] = acc_ref[...].T.astype(out_ref.dtype)


def _dense_attn(qt, k, vt, kc=None, vct=None, *, heads, k_slot, v_row, tq, tk, name):
    b, nq, lq = qt.shape
    _, lk, nkw = k.shape
    nv = vt.shape[1]
    assert nq == heads * LANES and lq % tq == 0 and lk % tk == 0
    nk = lk // tk
    has_ctx = kc is not None
    steps = nk + (1 if has_ctx else 0)
    in_specs = [
        pl.BlockSpec((1, nq, tq), lambda bi, qi, ki: (bi, 0, qi)),
        pl.BlockSpec((1, tk, nkw), lambda bi, qi, ki: (bi, jnp.minimum(ki, nk - 1), 0)),
        pl.BlockSpec((1, nv, tk), lambda bi, qi, ki: (bi, 0, jnp.minimum(ki, nk - 1))),
    ]
    args = [qt, k, vt]
    if has_ctx:
        lc = kc.shape[1]
        in_specs += [pl.BlockSpec((1, lc, nkw), lambda bi, qi, ki: (bi, 0, 0)),
                     pl.BlockSpec((1, nv, lc), lambda bi, qi, ki: (bi, 0, 0))]
        args += [kc, vct]
    body = functools.partial(_dense_body, heads=heads, k_slot=tuple(k_slot), v_row=tuple(v_row),
                             nk=nk, has_ctx=has_ctx)
    return pl.pallas_call(
        body, grid=(b, lq // tq, steps), in_specs=in_specs,
        out_specs=pl.BlockSpec((1, tq, heads * HEAD_DIM), lambda bi, qi, ki: (bi, qi, 0)),
        out_shape=jax.ShapeDtypeStruct((b, lq, heads * HEAD_DIM), BF16),
        scratch_shapes=[pltpu.VMEM((heads, tq), F32), pltpu.VMEM((heads, tq), F32),
                        pltpu.VMEM((heads * HEAD_DIM, tq), F32)],
        name=name, compiler_params=_cparams(("parallel", "parallel", "arbitrary")),
    )(*args)


def _dense_fast_body(qt_ref, k_ref, vt_ref, kc_ref, vct_ref, knl_ref, knc_ref, out_ref, lmin_ref,
                     b_ref, acc_ref, o_ref, *, heads, k_slot, v_row, nk):
    ki = pl.program_id(2)
    tq = qt_ref.shape[2]

    @pl.when(ki == 0)
    def _():
        kn = jnp.sqrt(jnp.maximum(jnp.max(knl_ref[0], axis=0), knc_ref[0, 0])) * BOUND_SLACK
        for h in range(heads):
            qf = qt_ref[0, h * LANES:(h + 1) * LANES, :].astype(F32)
            qn = jnp.sqrt(jnp.sum(qf * qf, axis=0, keepdims=True))
            krow = kn[k_slot[h]:k_slot[h] + 1, :]
            b_ref[h:h + 1, :] = qn * jnp.concatenate([krow] * (tq // LANES), axis=1)
        acc_ref[...] = jnp.zeros(acc_ref.shape, F32)

    def step(kr, vr):
        def scores(h):
            ks = k_slot[h]
            return _dot(kr[0, :, ks * LANES:(ks + 1) * LANES], qt_ref[0, h * LANES:(h + 1) * LANES, :])

        pending = [scores(h) for h in range(min(AHEAD, heads))]
        for h in range(heads):
            s = pending.pop(0)
            if h + AHEAD < heads:
                pending.append(scores(h + AHEAD))
            p = jnp.exp2(s - b_ref[h:h + 1, :]).astype(BF16)
            rows = slice(h * V_ROWS, (h + 1) * V_ROWS)
            acc_ref[rows, :] += _dot(vr[0, v_row[h]:v_row[h] + V_ROWS, :], p)

    @pl.when(ki < nk)
    def _():
        step(k_ref, vt_ref)

    @pl.when(ki == nk)
    def _():
        step(kc_ref, vct_ref)
        lmin = None
        for h in range(heads):
            l = acc_ref[h * V_ROWS + HEAD_DIM:h * V_ROWS + HEAD_DIM + 1, :]
            lmin = l if lmin is None else jnp.minimum(lmin, l)
            o_ref[h * HEAD_DIM:(h + 1) * HEAD_DIM, :] = acc_ref[h * V_ROWS:h * V_ROWS + HEAD_DIM, :] * (1.0 / l)
        out_ref[0] = o_ref[...].T.astype(out_ref.dtype)
        lmin_ref[0, 0] = jnp.broadcast_to(jnp.min(lmin, axis=1, keepdims=True), lmin_ref.shape[2:])


def _dense_attn_fast(qt, k, vt, kc, vct, knl, knc, *, heads, k_slot, v_row, tq, tk, name):
    b, nq, lq = qt.shape
    _, lk, nkw = k.shape
    nv = vt.shape[1]
    lc = kc.shape[1]
    nk = lk // tk
    assert nq == heads * LANES and lq % tq == 0 and lk % tk == 0 and knl.shape[1] == nk
    n_slots = knl.shape[2]
    in_specs = [
        pl.BlockSpec((1, nq, tq), lambda bi, qi, ki: (bi, 0, qi)),
        pl.BlockSpec((1, tk, nkw), lambda bi, qi, ki: (bi, jnp.minimum(ki, nk - 1), 0)),
        pl.BlockSpec((1, nv, tk), lambda bi, qi, ki: (bi, 0, jnp.minimum(ki, nk - 1))),
        pl.BlockSpec((1, lc, nkw), lambda bi, qi, ki: (bi, 0, 0)),
        pl.BlockSpec((1, nv, lc), lambda bi, qi, ki: (bi, 0, 0)),
        pl.BlockSpec((1, nk, n_slots, LANES), lambda bi, qi, ki: (bi, 0, 0, 0)),
        pl.BlockSpec((1, 1, n_slots, LANES), lambda bi, qi, ki: (bi, 0, 0, 0)),
    ]
    body = functools.partial(_dense_fast_body, heads=heads, k_slot=tuple(k_slot), v_row=tuple(v_row), nk=nk)
    return pl.pallas_call(
        body, grid=(b, lq // tq, nk + 1), in_specs=in_specs,
        out_specs=[pl.BlockSpec((1, tq, heads * HEAD_DIM), lambda bi, qi, ki: (bi, qi, 0)),
                   pl.BlockSpec((1, 1, 8, LANES), lambda bi, qi, ki: (bi, qi, 0, 0))],
        out_shape=[jax.ShapeDtypeStruct((b, lq, heads * HEAD_DIM), BF16),
                   jax.ShapeDtypeStruct((b, lq // tq, 8, LANES), F32)],
        scratch_shapes=[pltpu.VMEM((heads, tq), F32), pltpu.VMEM((heads * V_ROWS, tq), F32),
                        pltpu.VMEM((heads * HEAD_DIM, tq), F32)],
        name=name, compiler_params=_cparams(("parallel", "parallel", "arbitrary")),
    )(qt, k, vt, kc, vct, knl, knc)


def _attend(qt, k, vt, kc, vct, knl, knc, *, name, **kw):
    y, lmin = _dense_attn_fast(qt, k, vt, kc, vct, knl, knc, name=name + "_fast", **kw)
    return lax.cond(jnp.min(lmin) > L_MIN_OK, lambda: y,
                    lambda: _dense_attn(qt, k, vt, kc, vct, name=name, **kw))


NA_QROWS = 4
NA_TQ = NA_QROWS * GRID_W
NA_WROWS = 12
NA_WBLOCKS = NA_WROWS // NA_QROWS
NA_STEPS = GRID_H // NA_QROWS


def _natten_body(qt_ref, k0_ref, k1_ref, k2_ref, v0_ref, v1_ref, v2_ref, kc_ref, vct_ref, bias_ref,
                 out_ref, o_ref):
    k_refs = (k0_ref, k1_ref, k2_ref)
    v_refs = (v0_ref, v1_ref, v2_ref)

    def scores(h):
        lanes = slice((h // 2) * LANES, (h // 2 + 1) * LANES)
        qq = qt_ref[0, h * LANES:(h + 1) * LANES, :]
        return [_dot(kc_ref[0, :, lanes], qq)] + [_dot(k_refs[blk][0, :, lanes], qq) for blk in range(NA_WBLOCKS)]

    pending = [scores(h) for h in range(AHEAD)]
    for h in range(B_HEADS):
        s_all = pending.pop(0)
        if h + AHEAD < B_HEADS:
            pending.append(scores(h + AHEAD))
        vrows = slice(h * V_ROWS, (h + 1) * V_ROWS)
        s_all = [s_all[0]] + [s_all[1 + blk] + bias_ref[0, h, blk * NA_TQ:(blk + 1) * NA_TQ, :]
                              for blk in range(NA_WBLOCKS)]
        m = s_all[0].max(axis=0, keepdims=True)
        for s in s_all[1:]:
            m = jnp.maximum(m, s.max(axis=0, keepdims=True))
        pv = _dot(vct_ref[0, vrows, :], jnp.exp2(s_all[0] - m).astype(BF16))
        for blk in range(NA_WBLOCKS):
            pv = pv + _dot(v_refs[blk][0, vrows, :], jnp.exp2(s_all[1 + blk] - m).astype(BF16))
        o_ref[h * HEAD_DIM:(h + 1) * HEAD_DIM, :] = pv[:HEAD_DIM] * (1.0 / pv[HEAD_DIM:HEAD_DIM + 1])
    out_ref[0] = o_ref[...].T.astype(out_ref.dtype)


def _natten_fast_body(qt_ref, k0_ref, k1_ref, k2_ref, v0_ref, v1_ref, v2_ref, kc_ref, vct_ref, bias_ref,
                      knl_ref, knc_ref, bmax_ref, out_ref, lmin_ref, o_ref, kall_ref, vall_ref):
    kall_ref[0:CTX_LEN, :] = kc_ref[0]
    vall_ref[:, 0:CTX_LEN] = vct_ref[0]
    for blk, (kr, vr) in enumerate(((k0_ref, v0_ref), (k1_ref, v1_ref), (k2_ref, v2_ref))):
        kall_ref[CTX_LEN + blk * NA_TQ:CTX_LEN + (blk + 1) * NA_TQ, :] = kr[0]
        vall_ref[:, CTX_LEN + blk * NA_TQ:CTX_LEN + (blk + 1) * NA_TQ] = vr[0]
    kn = jnp.sqrt(jnp.maximum(jnp.max(knl_ref[0], axis=0), knc_ref[0, 0])) * BOUND_SLACK

    def scores(h):
        lanes = slice((h // 2) * LANES, (h // 2 + 1) * LANES)
        return _dot(kall_ref[:, lanes], qt_ref[0, h * LANES:(h + 1) * LANES, :])

    pending = [scores(h) for h in range(AHEAD)]
    lmin = None
    for h in range(B_HEADS):
        s = pending.pop(0)
        if h + AHEAD < B_HEADS:
            pending.append(scores(h + AHEAD))
        qf = qt_ref[0, h * LANES:(h + 1) * LANES, :].astype(F32)
        qn = jnp.sqrt(jnp.sum(qf * qf, axis=0, keepdims=True))
        krow = kn[h // 2:h // 2 + 1, :]
        shift = qn * jnp.concatenate([krow] * (NA_TQ // LANES), axis=1) + bmax_ref[h:h + 1, :]
        p = jnp.concatenate([jnp.exp2(s[:CTX_LEN] - shift).astype(BF16),
                             jnp.exp2(s[CTX_LEN:] + (bias_ref[0, h] - shift)).astype(BF16)], axis=0)
        pv = _dot(vall_ref[h * V_ROWS:(h + 1) * V_ROWS, :], p)
        l = pv[HEAD_DIM:HEAD_DIM + 1]
        lmin = l if lmin is None else jnp.minimum(lmin, l)
        o_ref[h * HEAD_DIM:(h + 1) * HEAD_DIM, :] = pv[:HEAD_DIM] * (1.0 / l)
    out_ref[0] = o_ref[...].T.astype(out_ref.dtype)
    lmin_ref[0, 0] = jnp.broadcast_to(jnp.min(lmin, axis=1, keepdims=True), lmin_ref.shape[2:])


def _natten_call(body, name, qt, k, vt, kc, vct, bias, extra=(), with_lmin=False, gather=False):
    b = qt.shape[0]
    nw = B_HEADS * HEAD_DIM
    nv = B_HEADS * V_ROWS

    def wstart(i):
        return jnp.clip(i - 1, 0, NA_STEPS - NA_WBLOCKS)

    def cls(i):
        return jnp.where(i == 0, 0, jnp.where(i == NA_STEPS - 1, 2, 1))

    in_specs = [pl.BlockSpec((1, B_HEADS * LANES, NA_TQ), lambda bi, i: (bi, 0, i))]
    in_specs += [pl.BlockSpec((1, NA_TQ, nw), functools.partial(lambda bi, i, j: (bi, wstart(i) + j, 0), j=j))
                 for j in range(NA_WBLOCKS)]
    in_specs += [pl.BlockSpec((1, nv, NA_TQ), functools.partial(lambda bi, i, j: (bi, 0, wstart(i) + j), j=j))
                 for j in range(NA_WBLOCKS)]
    in_specs += [pl.BlockSpec((1, CTX_LEN, nw), lambda bi, i: (bi, 0, 0)),
                 pl.BlockSpec((1, nv, CTX_LEN), lambda bi, i: (bi, 0, 0)),
                 pl.BlockSpec((1, B_HEADS, NA_WROWS * GRID_W, NA_TQ), lambda bi, i: (cls(i), 0, 0, 0))]
    for a, per_batch in extra:
        blk = (1,) + a.shape[1:] if per_batch else a.shape
        nz = len(blk) - 1
        in_specs.append(pl.BlockSpec(blk, (lambda bi, i, nz=nz: (bi,) + (0,) * nz) if per_batch
                                     else (lambda bi, i, nz=nz: (0,) * (nz + 1))))
    out_specs = [pl.BlockSpec((1, NA_TQ, nw), lambda bi, i: (bi, i, 0))]
    out_shape = [jax.ShapeDtypeStruct((b, SEQ, nw), BF16)]
    if with_lmin:
        out_specs.append(pl.BlockSpec((1, 1, 8, LANES), lambda bi, i: (bi, i, 0, 0)))
        out_shape.append(jax.ShapeDtypeStruct((b, NA_STEPS, 8, LANES), F32))
    return pl.pallas_call(
        body, grid=(b, NA_STEPS), in_specs=in_specs, out_specs=out_specs, out_shape=out_shape,
        scratch_shapes=[pltpu.VMEM((nw, NA_TQ), F32)] + ([pltpu.VMEM((CTX_LEN + NA_WROWS * GRID_W, nw), BF16),
                                                          pltpu.VMEM((nv, CTX_LEN + NA_WROWS * GRID_W), BF16)] if gather else []),
        name=name, compiler_params=_cparams(("parallel", "arbitrary")),
    )(qt, k, k, k, vt, vt, vt, kc, vct, bias, *[a for a, _ in extra])


def _natten(qt, k, vt, kc, vct, rpb, knl, knc):
    bias = _natten_bias(rpb)
    bmax = jnp.maximum(jnp.max(rpb.astype(F32), axis=(1, 2)), 0.0) * LOG2E
    bmax = jnp.broadcast_to(bmax[:, None], (B_HEADS, NA_TQ))
    y, lmin = _natten_call(_natten_fast_body, "natten_fast", qt, k, vt, kc, vct, bias,
                           extra=((knl, True), (knc, True), (bmax, False)), with_lmin=True, gather=True)
    return lax.cond(jnp.min(lmin) > L_MIN_OK, lambda: y,
                    lambda: _natten_call(_natten_body, "natten", qt, k, vt, kc, vct, bias)[0])


def _natten_bias(rpb):
    v = rpb.astype(F32) * LOG2E
    h, nr, _ = v.shape
    period = 2 * GRID_W
    row = jnp.zeros((h, nr, period), F32)
    row = row.at[..., :NA_COLS].set(v[..., NA_COLS - 1::-1])
    row = row.at[..., period - (NA_COLS - 1):].set(v[..., :NA_COLS - 1:-1])
    toep = jnp.tile(row, (1, 1, GRID_W))[..., :GRID_W * (period - 1)]
    toep = toep.reshape(h, nr, GRID_W, period - 1)[..., :GRID_W]
    col = np.arange(GRID_W)
    cs = np.clip(col - NA_COLS // 2, 0, GRID_W - NA_COLS)
    col_ok = (col[:, None] >= cs[None, :]) & (col[:, None] < cs[None, :] + NA_COLS)
    masked = 2.0 * MASK_VALUE
    toep = jnp.where(jnp.asarray(col_ok), toep, masked)
    dead = jnp.full((h, GRID_W, GRID_W), masked, F32)
    tables = []
    for r0 in (0, 2 * NA_QROWS, GRID_H - NA_QROWS):
        ws = min(max(r0 - NA_QROWS, 0), GRID_H - NA_WROWS)
        key_rows = []
        for ki in range(NA_WROWS):
            blocks = []
            for qi in range(NA_QROWS):
                k_row, q_row = ws + ki, r0 + qi
                rs = min(max(q_row - NA_ROWS // 2, 0), GRID_H - NA_ROWS)
                blocks.append(toep[:, k_row - q_row + NA_ROWS - 1] if rs <= k_row < rs + NA_ROWS else dead)
            key_rows.append(jnp.concatenate(blocks, axis=-1))
        tables.append(jnp.concatenate(key_rows, axis=-2))
    return jnp.stack(tables)


def _ffn_body(x_ref, xp_ref, xn_ref, mod_ref, wg_ref, wv_ref, wd_ref, cw_ref, cb_ref, lng_ref, lnb_ref,
              out_ref, hs_ref, g_ref, *, tm, rc, n_tiles):
    i = pl.program_id(1)
    sc = mod_ref[0, SC2:SC2 + 1, :]
    sh = mod_ref[0, SH2:SH2 + 1, :]
    gate_mod = mod_ref[0, G2:G2 + 1, :]

    def modulate(v):
        return v * (1.0 + sc) + sh

    hp = jnp.where(i > 0, modulate(xp_ref[0]), 0.0)
    hn = jnp.where(i < n_tiles - 1, modulate(xn_ref[0]), 0.0)
    hs_ref[0:FFN_HALO, :] = hp.astype(BF16)
    hs_ref[FFN_HALO:FFN_HALO + tm, :] = modulate(x_ref[0]).astype(BF16)
    hs_ref[FFN_HALO + tm:, :] = hn.astype(BF16)

    def up(c):
        lo = c * rc
        g_ref[c] = _dot(hs_ref[lo:lo + rc + 2 * FFN_HALO, :], wg_ref[...])
        return _dot(hs_ref[FFN_HALO + lo:FFN_HALO + lo + rc, :], wv_ref[...])

    n_chunks = tm // rc
    val_next = up(0)
    for c in range(n_chunks):
        val = val_next
        if c + 1 < n_chunks:
            val_next = up(c + 1)
        gate = (cw_ref[0:1, :] * g_ref[c, FFN_HALO - 1:FFN_HALO - 1 + rc, :]
                + cw_ref[1:2, :] * g_ref[c, FFN_HALO:FFN_HALO + rc, :]
                + cw_ref[2:3, :] * g_ref[c, FFN_HALO + 1:FFN_HALO + 1 + rc, :]
                + cb_ref[...])
        act = gate * jax.nn.sigmoid(gate) * val
        y = _dot(act.astype(BF16), wd_ref[...])
        rows = slice(c * rc, (c + 1) * rc)
        out_ref[0, rows, :] = _layer_norm(DEEPNORM_ALPHA * x_ref[0, rows, :] + gate_mod * y, lng_ref[...], lnb_ref[...])


def _ffn(x, mod, mod_row, wup, wd, conv_w, conv_b, ln_g, ln_b, *, tm, name):
    b, t, d = x.shape
    n_tiles = t // tm
    rc = min(FFN_ROWS, tm)
    hb = tm // FFN_HALO
    n_hblocks = t // FFN_HALO
    mod_map = (lambda bi, i: (bi, 0, 0)) if mod_row is None else (lambda bi, i: (mod_row, 0, 0))

    def resident(shape, col=0):
        return pl.BlockSpec(shape, lambda bi, i: (0, col), pipeline_mode=pl.Buffered(1))

    in_specs = [
        pl.BlockSpec((1, tm, d), lambda bi, i: (bi, i, 0)),
        pl.BlockSpec((1, FFN_HALO, d), lambda bi, i: (bi, jnp.maximum(i * hb - 1, 0), 0)),
        pl.BlockSpec((1, FFN_HALO, d), lambda bi, i: (bi, jnp.minimum((i + 1) * hb, n_hblocks - 1), 0)),
        pl.BlockSpec((1, N_MOD, d), mod_map),
        resident((d, D_FF), 0), resident((d, D_FF), 1), resident((D_FF, d)),
        resident((3, D_FF)), resident((1, D_FF)), resident((1, d)), resident((1, d)),
    ]
    body = functools.partial(_ffn_body, tm=tm, rc=rc, n_tiles=n_tiles)
    return pl.pallas_call(
        body, grid=(b, n_tiles), in_specs=in_specs,
        out_specs=pl.BlockSpec((1, tm, d), lambda bi, i: (bi, i, 0)),
        out_shape=jax.ShapeDtypeStruct((b, t, d), F32),
        scratch_shapes=[pltpu.VMEM((tm + 2 * FFN_HALO, d), BF16),
                        pltpu.VMEM((tm // rc, rc + 2 * FFN_HALO, D_FF), F32)],
        name=name, compiler_params=_cparams(("parallel", "parallel")),
    )(x, x, x, mod, wup, wup, wd, conv_w.astype(F32), conv_b.reshape(1, D_FF).astype(F32),
      ln_g.reshape(1, d).astype(F32), ln_b.reshape(1, d).astype(F32))


def _rope_cos_sin(rot_dim):
    pos = jnp.arange(SEQ, dtype=jnp.int32)
    rows = (pos // GRID_W).astype(F32)
    cols = (pos % GRID_W).astype(F32)
    axis_dim = rot_dim // 2
    inv = ROPE_THETA ** (-jnp.arange(0, axis_dim, 2, dtype=F32) / axis_dim)
    ang = jnp.concatenate([rows[:, None] * inv, cols[:, None] * inv], axis=-1)
    cos = jnp.repeat(jnp.cos(ang), 2, axis=-1)
    sin = jnp.repeat(jnp.sin(ang), 2, axis=-1) * jnp.tile(jnp.array([-1.0, 1.0], F32), rot_dim // 2)
    return cos, sin


def _swap_pairs(g):
    return g.reshape(-1, 2)[:, ::-1].reshape(-1)


def _gqa_tables(gain, scale, length, rope):
    if rope:
        cos, sin = _rope_cos_sin(HEAD_DIM)
    else:
        cos, sin = jnp.ones((length, HEAD_DIM), F32), jnp.zeros((length, HEAD_DIM), F32)
    c = cos * (gain * scale)
    s = sin * (_swap_pairs(gain) * scale)
    return jnp.concatenate([c, c], axis=-1), jnp.concatenate([s, s], axis=-1)


def _mla_tables(scale, length, rope):
    c = jnp.full((length, LANES), scale, F32)
    s = jnp.zeros((length, LANES), F32)
    if rope:
        cos, sin = _rope_cos_sin(MLA_ROPE_DIM)
        lo, hi = MLA_NOPE_DIM, MLA_NOPE_DIM + MLA_ROPE_DIM
        c = c.at[:, lo:hi].set(cos * scale)
        s = s.at[:, lo:hi].set(sin * scale)
    return c, s


def _slot_rows(w, heads, width, half_of):
    k = w.shape[0]
    out = jnp.zeros((heads * LANES, k), w.dtype)
    for h in range(heads):
        off = h * LANES + half_of(h) * HALF
        out = out.at[off:off + width].set(w[:, h * width:(h + 1) * width].T)
    return out


def _layer0_mixer(x, xc, mod, w_in, q_gain, k_gain, rpb):
    w = w_in.astype(BF16)
    aq, akv, bw = A_HEADS * HEAD_DIM, A_KV_HEADS * HEAD_DIM, B_HEADS * HEAD_DIM
    o = 0
    w_qa = w[:, o:o + aq]; o += aq
    w_ka = w[:, o:o + akv]; o += akv
    w_va = w[:, o:o + akv]; o += akv
    w_qb = w[:, o:o + bw]; o += bw
    w_kb = w[:, o:o + bw]; o += bw
    w_vb = w[:, o:o + bw]
    wqa = jnp.concatenate([w_qa, w_va], axis=1).T
    wqb = jnp.concatenate([w_qb, w_vb], axis=1).T
    wka = jnp.concatenate([w_ka, w_kb], axis=1)
    weights = (wqa, wka, wqb)
    qscale = HEAD_SCALE * LOG2E

    def tables(length, rope):
        cq, sq = _gqa_tables(q_gain, qscale, length, rope)
        ck, sk = _gqa_tables(k_gain, 1.0, length, rope)
        return cq[:, :HEAD_DIM].T, sq[:, :HEAD_DIM].T, ck, sk

    qa_t, ka, kna, va_t, qb_t, kb, knb, vb_t = _l0_proj(x, mod, None, weights, tables(SEQ, True), tm=TM, name="l0_proj")
    qac_t, kac, knac, vac_t, qbc_t, kbc, knbc, vbc_t = _l0_proj(xc, mod, CTX_MOD_ROW, weights, tables(CTX_LEN, False),
                                                    tm=TMC, name="l0_proj_ctx")

    a_kslot = [0] * A_HEADS
    a_vrow = [(h // A_GROUP) * V_ROWS for h in range(A_HEADS)]
    b_kslot = [h // 2 for h in range(B_HEADS)]
    b_vrow = [h * V_ROWS for h in range(B_HEADS)]
    ya = _attend(qa_t, ka, va_t, kac, vac_t, kna, knac, heads=A_HEADS, k_slot=a_kslot, v_row=a_vrow,
                 tq=TQ, tk=TK, name="gqa")
    yb = _natten(qb_t, kb, vb_t, kbc, vbc_t, rpb, knb, knbc)
    yac = _dense_attn(qac_t, kac, vac_t, heads=A_HEADS, k_slot=a_kslot, v_row=a_vrow,
                      tq=CTX_LEN, tk=CTX_LEN, name="gqa_ctx")
    ybc = _dense_attn(qbc_t, kbc, vbc_t, heads=B_HEADS, k_slot=b_kslot, v_row=b_vrow,
                      tq=CTX_LEN, tk=CTX_LEN, name="nbr_ctx")
    return (ya, yb), (yac, ybc)


def _layer1_mixer(x, xc, mod, w_in, cq_gain, ckv_gain, w_uq, w_ukv):
    qk_dim = MLA_NOPE_DIM + MLA_ROPE_DIM
    kv_dim = MLA_NOPE_DIM + MLA_V_DIM
    n_in = MLA_Q_LORA + MLA_KV_LORA
    wdn = jnp.zeros((D_MODEL, n_in + LANES), BF16)
    wdn = wdn.at[:, :n_in].set(w_in[:, :n_in].astype(BF16))
    wdn = wdn.at[:, n_in + MLA_NOPE_DIM:n_in + qk_dim].set(w_in[:, n_in:].astype(BF16))
    wq = _slot_rows(w_uq.astype(BF16), MLA_HEADS, qk_dim, lambda h: 0)
    wkv = w_ukv.astype(BF16).reshape(MLA_KV_LORA, MLA_HEADS, kv_dim)
    wk = _slot_rows(wkv[:, :, :MLA_NOPE_DIM].reshape(MLA_KV_LORA, -1), MLA_HEADS, MLA_NOPE_DIM, lambda h: 0).T
    wv = wkv[:, :, MLA_NOPE_DIM:].reshape(MLA_KV_LORA, MLA_HEADS * MLA_V_DIM).T
    weights = (wdn, wq, wk, wv)
    gains = (cq_gain.reshape(1, -1).astype(F32), ckv_gain.reshape(1, -1).astype(F32))

    cq, sq = _mla_tables(MLA_SCALE * LOG2E, SEQ, True)
    qt, kk, kn, vt = _l1_proj(x, mod, None, weights, gains, (cq.T, sq.T), _mla_tables(1.0, SEQ, True),
                          tm=TM, with_q=True, name="l1_proj")
    kkc, knc, vtc = _l1_proj(xc, mod, CTX_MOD_ROW, weights, gains, None, _mla_tables(1.0, CTX_LEN, False),
                        tm=TMC, with_q=False, name="l1_proj_ctx")
    return _attend(qt, kk, vt, kkc, vtc, kn, knc, heads=MLA_HEADS, k_slot=list(range(MLA_HEADS)),
                   v_row=[h * V_ROWS for h in range(MLA_HEADS)], tq=TQ, tk=TK, name="mla")


def _post_mixer(x, y, mod, mod_row, w_out, ln1, w_up, conv_w, conv_b, w_down, ln2, tm, tag):
    wo = w_out.astype(BF16)
    if isinstance(y, tuple):
        n0 = y[0].shape[2]
        extra = dict(x2=y[1], w2=wo[n0:])
        y, wo = y[0], wo[:n0]
    else:
        extra = {}
    x = _proj(y, wo, name="out_ln" + tag, tm=tm, tn=D_MODEL, mod=mod, mod_row=mod_row, gate_row=G1, resid=x,
              ln=ln1, **extra)
    return _ffn(x, mod, mod_row, w_up.astype(BF16), w_down.astype(BF16), conv_w, conv_b, ln2[0], ln2[1],
                tm=tm, name="ffn" + tag)


def kernel(x, c, ctx, c_ctx, l0_w_ada, l0_b_ada, l0_w_in, l0_q_gain, l0_k_gain, l0_rpb, l0_w_out, l0_ln1_g, l0_ln1_b, l0_w_up, l0_conv_w, l0_conv_b, l0_w_down, l0_ln2_g, l0_ln2_b, l1_w_ada, l1_b_ada, l1_w_in, l1_cq_gain, l1_ckv_gain, l1_w_uq, l1_w_ukv, l1_w_out, l1_ln1_g, l1_ln1_b, l1_w_up, l1_conv_w, l1_conv_b, l1_w_down, l1_ln2_g, l1_ln2_b):
    xc = ctx
    mod = _modvec(c, c_ctx, l0_w_ada, l0_b_ada)
    y, yc = _layer0_mixer(x, xc, mod, l0_w_in, l0_q_gain, l0_k_gain, l0_rpb)
    post0 = (l0_w_out, (l0_ln1_g, l0_ln1_b), l0_w_up, l0_conv_w, l0_conv_b, l0_w_down, (l0_ln2_g, l0_ln2_b))
    x = _post_mixer(x, y, mod, None, *post0, tm=TM, tag="0")
    xc = _post_mixer(xc, yc, mod, CTX_MOD_ROW, *post0, tm=TMC, tag="0_ctx")
    mod = _modvec(c, c_ctx, l1_w_ada, l1_b_ada)
    y = _layer1_mixer(x, xc, mod, l1_w_in, l1_cq_gain, l1_ckv_gain, l1_w_uq, l1_w_ukv)
    post1 = (l1_w_out, (l1_ln1_g, l1_ln1_b), l1_w_up, l1_conv_w, l1_conv_b, l1_w_down, (l1_ln2_g, l1_ln2_b))
    return _post_mixer(x, y, mod, None, *post1, tm=TM, tag="1")
```

```python
import functools
import math

import numpy as np
import jax
import jax.numpy as jnp
from jax import lax
from jax.experimental import pallas as pl
from jax.experimental.pallas import tpu as pltpu

D_MODEL = 1024
BATCH = 4
SEQ = 4096
DEPTH = 2
GRID_W = 64
GRID_H = SEQ // GRID_W
CTX_LEN = 256
HEAD_DIM = 64
A_HEADS = 8
A_KV_HEADS = 2
A_GROUP = A_HEADS // A_KV_HEADS
B_HEADS = 8
NA_ROWS = 8
NA_COLS = 16
ROPE_THETA = 10000.0
MLA_HEADS = 16
MLA_Q_LORA = 768
MLA_KV_LORA = 256
MLA_NOPE_DIM = 64
MLA_ROPE_DIM = 32
MLA_V_DIM = 64
D_FF = 2816
N_MOD = 6
EPS = 1e-6
DEEPNORM_ALPHA = (2 * DEPTH) ** 0.25
HEAD_SCALE = HEAD_DIM ** -0.5
MLA_SCALE = (MLA_NOPE_DIM + MLA_ROPE_DIM) ** -0.5
LOG2E = math.log2(math.e)

LANES = 128
SUBLANES = 8
HALF = LANES // 2
BF16_ROWS = 16
V_ROWS = HEAD_DIM + BF16_ROWS
MASK_VALUE = -1e30
VMEM_LIMIT = 56 * 1024 * 1024

SH1, SC1, G1, SH2, SC2, G2 = range(N_MOD)

F32 = jnp.float32
BF16 = jnp.bfloat16

MOD_ROWS = 16
CTX_MOD_ROW = BATCH
TM = 512
TM_WIDE = 1024
TMC = CTX_LEN
TQ = 1024
TK = 512
AHEAD = 2
BOUND_SLACK = 1.0 + 1e-3
L_MIN_OK = 2.0 ** -80
FFN_ROWS = 256
FFN_HALO = BF16_ROWS
ROW_CHUNK = 256


def _cparams(sem):
    return pltpu.CompilerParams(dimension_semantics=sem, vmem_limit_bytes=VMEM_LIMIT)


def _dot(a, b):
    return jnp.dot(a, b, preferred_element_type=F32)


def _dot_nt(a, b):
    return lax.dot_general(a, b, (((1,), (1,)), ((), ())), preferred_element_type=F32)


def _pair_swap(y, axis):
    n = y.shape[axis]
    idx = lax.broadcasted_iota(jnp.int32, y.shape, axis)
    nxt = pltpu.roll(y, n - 1, axis=axis)
    prv = pltpu.roll(y, 1, axis=axis)
    return jnp.where(idx % 2 == 0, nxt, prv)


def _rotate(y, c, s, axis):
    return y * c + _pair_swap(y, axis) * s


def _layer_norm(z, g, b):
    mu = jnp.mean(z, axis=-1, keepdims=True)
    zc = z - mu
    var = jnp.mean(zc * zc, axis=-1, keepdims=True)
    return zc * lax.rsqrt(var + EPS) * g + b


def _slot_norm_rows(kb, n_slots):
    rows = []
    for sl in range(n_slots):
        kf = kb[:, sl * LANES:(sl + 1) * LANES].astype(F32)
        n2 = jnp.sum(kf * kf, axis=-1, keepdims=True)
        rows.append(jnp.broadcast_to(jnp.max(n2, axis=0, keepdims=True), (1, LANES)))
    return rows[0] if n_slots == 1 else jnp.concatenate(rows, axis=0)


def _store_vt(out_ref, yt, heads):
    ones = jnp.ones((BF16_ROWS, yt.shape[1]), out_ref.dtype)
    for h in range(heads):
        out_ref[0, h * V_ROWS:h * V_ROWS + HEAD_DIM, :] = yt[h * HEAD_DIM:(h + 1) * HEAD_DIM].astype(out_ref.dtype)
        out_ref[0, h * V_ROWS + HEAD_DIM:(h + 1) * V_ROWS, :] = ones


def _proj_body(*refs, silu, bias, resid_ln, gate_row, two):
    it = iter(refs)
    x_ref = next(it)
    w_ref = next(it)
    x2_ref = next(it) if two else None
    w2_ref = next(it) if two else None
    bias_ref = next(it) if bias else None
    mod_ref = next(it) if resid_ln else None
    xres_ref = next(it) if resid_ln else None
    lng_ref = next(it) if resid_ln else None
    lnb_ref = next(it) if resid_ln else None
    out_ref = next(it)
    x = x_ref[0]
    if silu:
        x = x * jax.nn.sigmoid(x)
    y = _dot(x.astype(BF16), w_ref[...].astype(BF16))
    if two:
        y = y + _dot(x2_ref[0], w2_ref[...])
    if bias:
        y = y + bias_ref[...]
    if resid_ln:
        g = mod_ref[0, gate_row:gate_row + 1, :]
        y = _layer_norm(DEEPNORM_ALPHA * xres_ref[0] + g * y, lng_ref[...], lnb_ref[...])
    out_ref[0] = y.astype(out_ref.dtype)


def _proj(x, w, *, name, tm, tn, x2=None, w2=None, silu=False, bias=None, mod=None, mod_row=None,
          gate_row=None, resid=None, ln=None, out_dtype=F32):
    b, t, k = x.shape
    n = w.shape[1]
    assert t % tm == 0 and n % tn == 0
    resid_ln = resid is not None
    in_specs = [pl.BlockSpec((1, tm, k), lambda bi, i, j: (bi, i, 0)),
                pl.BlockSpec((k, tn), lambda bi, i, j: (0, j))]
    args = [x, w]
    if x2 is not None:
        k2 = x2.shape[2]
        in_specs += [pl.BlockSpec((1, tm, k2), lambda bi, i, j: (bi, i, 0)),
                     pl.BlockSpec((k2, tn), lambda bi, i, j: (0, j))]
        args += [x2, w2]
    if bias is not None:
        in_specs.append(pl.BlockSpec((1, tn), lambda bi, i, j: (0, j)))
        args.append(bias.reshape(1, n).astype(F32))
    if resid_ln:
        mod_map = (lambda bi, i, j: (bi, 0, 0)) if mod_row is None else (lambda bi, i, j: (mod_row, 0, 0))
        in_specs += [pl.BlockSpec((1, N_MOD, D_MODEL), mod_map),
                     pl.BlockSpec((1, tm, n), lambda bi, i, j: (bi, i, 0)),
                     pl.BlockSpec((1, n), lambda bi, i, j: (0, 0)),
                     pl.BlockSpec((1, n), lambda bi, i, j: (0, 0))]
        args += [mod, resid, ln[0].reshape(1, n).astype(F32), ln[1].reshape(1, n).astype(F32)]
    body = functools.partial(_proj_body, silu=silu, bias=bias is not None, resid_ln=resid_ln, gate_row=gate_row,
                             two=x2 is not None)
    return pl.pallas_call(
        body, grid=(b, t // tm, n // tn), in_specs=in_specs,
        out_specs=pl.BlockSpec((1, tm, tn), lambda bi, i, j: (bi, i, j)),
        out_shape=jax.ShapeDtypeStruct((b, t, n), out_dtype), name=name,
        compiler_params=_cparams(("parallel", "parallel", "arbitrary")),
    )(*args)


def _modvec(c, c_ctx, w_ada, b_ada):
    cond = jnp.zeros((1, MOD_ROWS, D_MODEL), F32).at[0, :BATCH].set(c).at[0, BATCH].set(c_ctx)
    m = _proj(cond, w_ada, name="adaln", tm=MOD_ROWS, tn=1024, silu=True, bias=b_ada)
    return m.reshape(MOD_ROWS, N_MOD, D_MODEL)


def _l0_proj_body(x_ref, mod_ref, wqa_ref, cqa_ref, sqa_ref, wka_ref, cka_ref, ska_ref, wqb_ref,
                  qa_ref, ka_ref, kna_ref, va_ref, qb_ref, kb_ref, knb_ref, vb_ref):
    sc = mod_ref[0, SC1:SC1 + 1, :]
    sh = mod_ref[0, SH1:SH1 + 1, :]
    xs = (x_ref[0] * (1.0 + sc) + sh).astype(BF16)

    cq, sq = cqa_ref[...], sqa_ref[...]
    zeros = jnp.zeros((HALF, xs.shape[0]), qa_ref.dtype)
    yt_a = _dot_nt(wqa_ref[...], xs)
    yt = yt_a[:A_HEADS * HEAD_DIM]
    for h in range(A_HEADS):
        seg = yt[h * HEAD_DIM:(h + 1) * HEAD_DIM]
        ms = jnp.mean(seg * seg, axis=0, keepdims=True)
        q = (_rotate(seg, cq, sq, 0) * lax.rsqrt(ms + EPS)).astype(qa_ref.dtype)
        half = h // A_GROUP
        qa_ref[0, h * LANES + half * HALF:h * LANES + (half + 1) * HALF, :] = q
        qa_ref[0, h * LANES + (1 - half) * HALF:h * LANES + (2 - half) * HALF, :] = zeros

    y_k = _dot(xs, wka_ref[...])
    y = y_k[:, :LANES]
    lo = lax.broadcasted_iota(jnp.int32, y.shape, 1) < HALF
    ysq = y * y
    ms_lo = jnp.sum(jnp.where(lo, ysq, 0.0), axis=-1, keepdims=True) * (1.0 / HEAD_DIM)
    ms_hi = jnp.sum(jnp.where(lo, 0.0, ysq), axis=-1, keepdims=True) * (1.0 / HEAD_DIM)
    rn = jnp.where(lo, lax.rsqrt(ms_lo + EPS), lax.rsqrt(ms_hi + EPS))
    ka = (_rotate(y, cka_ref[...], ska_ref[...], 1) * rn).astype(ka_ref.dtype)
    ka_ref[0] = ka
    kna_ref[0, 0] = jnp.broadcast_to(_slot_norm_rows(ka, 1), kna_ref.shape[2:])

    _store_vt(va_ref, yt_a[A_HEADS * HEAD_DIM:], A_KV_HEADS)

    yt_b = _dot_nt(wqb_ref[...], xs)
    yt = yt_b[:B_HEADS * HEAD_DIM] * (HEAD_SCALE * LOG2E)
    for h in range(B_HEADS):
        half = h % 2
        qb_ref[0, h * LANES + half * HALF:h * LANES + (half + 1) * HALF, :] = (
            yt[h * HEAD_DIM:(h + 1) * HEAD_DIM].astype(qb_ref.dtype))
        qb_ref[0, h * LANES + (1 - half) * HALF:h * LANES + (2 - half) * HALF, :] = zeros
    kb = y_k[:, LANES:].astype(kb_ref.dtype)
    kb_ref[0] = kb
    n_pairs = B_HEADS // 2
    knb_ref[0, 0] = jnp.concatenate([_slot_norm_rows(kb, n_pairs), jnp.zeros((SUBLANES - n_pairs, LANES), F32)], axis=0)
    _store_vt(vb_ref, yt_b[B_HEADS * HEAD_DIM:], B_HEADS)


def _l0_proj(x, mod, mod_row, weights, tables, *, tm, name):
    b, t, d = x.shape
    wqa, wka, wqb = weights
    cqa, sqa, cka, ska = tables
    mod_map = (lambda bi, i: (bi, 0, 0)) if mod_row is None else (lambda bi, i: (mod_row, 0, 0))

    def full(a):
        return pl.BlockSpec(a.shape, lambda bi, i: (0,) * a.ndim)

    in_specs = [pl.BlockSpec((1, tm, d), lambda bi, i: (bi, i, 0)),
                pl.BlockSpec((1, N_MOD, d), mod_map),
                full(wqa), pl.BlockSpec((HEAD_DIM, tm), lambda bi, i: (0, i)), pl.BlockSpec((HEAD_DIM, tm), lambda bi, i: (0, i)),
                full(wka), pl.BlockSpec((tm, LANES), lambda bi, i: (i, 0)), pl.BlockSpec((tm, LANES), lambda bi, i: (i, 0)),
                full(wqb)]
    nb = B_HEADS * HEAD_DIM

    def nat(n):
        return jax.ShapeDtypeStruct((b, t, n), BF16), pl.BlockSpec((1, tm, n), lambda bi, i: (bi, i, 0))

    def tr(n):
        return jax.ShapeDtypeStruct((b, n, t), BF16), pl.BlockSpec((1, n, tm), lambda bi, i: (bi, 0, i))

    kn = (jax.ShapeDtypeStruct((b, t // tm, SUBLANES, LANES), F32), pl.BlockSpec((1, 1, SUBLANES, LANES), lambda bi, i: (bi, i, 0, 0)))
    outs = [tr(A_HEADS * LANES), nat(LANES), kn, tr(A_KV_HEADS * V_ROWS), tr(B_HEADS * LANES), nat(nb), kn, tr(B_HEADS * V_ROWS)]
    return pl.pallas_call(
        _l0_proj_body, grid=(b, t // tm), in_specs=in_specs,
        out_specs=[o[1] for o in outs], out_shape=[o[0] for o in outs], name=name,
        compiler_params=_cparams(("parallel", "parallel")),
    )(x, mod, wqa, cqa, sqa, wka, cka, ska, wqb)


def _l1_proj_body(*refs, with_q):
    it = iter(refs)
    x_ref, mod_ref, wdn_ref, gq_ref, gkv_ref = (next(it) for _ in range(5))
    wq_ref, cq_ref, sq_ref = (next(it) for _ in range(3)) if with_q else (None, None, None)
    wk_ref, ck_ref, sk_ref, wv_ref = (next(it) for _ in range(4))
    q_ref = next(it) if with_q else None
    k_ref, kn_ref, v_ref = next(it), next(it), next(it)

    sc = mod_ref[0, SC1:SC1 + 1, :]
    sh = mod_ref[0, SH1:SH1 + 1, :]
    xs = (x_ref[0] * (1.0 + sc) + sh).astype(BF16)

    def rms(v, g_ref):
        ms = jnp.mean(v * v, axis=-1, keepdims=True)
        return (v * lax.rsqrt(ms + EPS) * g_ref[...]).astype(BF16)

    n_q, n_kv = MLA_Q_LORA, MLA_KV_LORA
    c_kv = rms(_dot(xs, wdn_ref[:, n_q:n_q + n_kv]), gkv_ref)
    k_r = _rotate(_dot(xs, wdn_ref[:, n_q + n_kv:]), ck_ref[...], sk_ref[...], 1)

    if with_q:
        c_q = rms(_dot(xs, wdn_ref[:, :n_q]), gq_ref)
        cq = jnp.concatenate([cq_ref[...]] * (ROW_CHUNK // LANES), axis=0)
        sq = jnp.concatenate([sq_ref[...]] * (ROW_CHUNK // LANES), axis=0)
        for r in range(MLA_HEADS * LANES // ROW_CHUNK):
            rows = slice(r * ROW_CHUNK, (r + 1) * ROW_CHUNK)
            q_ref[0, rows, :] = _rotate(_dot_nt(wq_ref[rows, :], c_q), cq, sq, 0).astype(q_ref.dtype)

    k_r_tiled = jnp.concatenate([k_r] * (ROW_CHUNK // LANES), axis=-1)
    for j in range(MLA_HEADS * LANES // ROW_CHUNK):
        cols = slice(j * ROW_CHUNK, (j + 1) * ROW_CHUNK)
        kb = (_dot(c_kv, wk_ref[:, cols]) + k_r_tiled).astype(k_ref.dtype)
        k_ref[0, :, cols] = kb
        n_sl = ROW_CHUNK // LANES
        kn_ref[0, 0, j * n_sl:(j + 1) * n_sl, :] = _slot_norm_rows(kb, n_sl)

    _store_vt(v_ref, _dot_nt(wv_ref[...], c_kv), MLA_HEADS)


def _l1_proj(x, mod, mod_row, weights, gains, q_tables, k_tables, *, tm, with_q, name):
    b, t, d = x.shape
    wdn, wq, wk, wv = weights
    mod_map = (lambda bi, i: (bi, 0, 0)) if mod_row is None else (lambda bi, i: (mod_row, 0, 0))

    def full(a):
        return pl.BlockSpec(a.shape, lambda bi, i: (0,) * a.ndim)

    in_specs = [pl.BlockSpec((1, tm, d), lambda bi, i: (bi, i, 0)), pl.BlockSpec((1, N_MOD, d), mod_map),
                full(wdn), full(gains[0]), full(gains[1])]
    args = [x, mod, wdn, gains[0], gains[1]]
    if with_q:
        in_specs += [full(wq), pl.BlockSpec((LANES, tm), lambda bi, i: (0, i)), pl.BlockSpec((LANES, tm), lambda bi, i: (0, i))]
        args += [wq, q_tables[0], q_tables[1]]
    in_specs += [full(wk), pl.BlockSpec((tm, LANES), lambda bi, i: (i, 0)), pl.BlockSpec((tm, LANES), lambda bi, i: (i, 0)), full(wv)]
    args += [wk, k_tables[0], k_tables[1], wv]
    out_shape, out_specs = [], []
    if with_q:
        out_shape.append(jax.ShapeDtypeStruct((b, MLA_HEADS * LANES, t), BF16))
        out_specs.append(pl.BlockSpec((1, MLA_HEADS * LANES, tm), lambda bi, i: (bi, 0, i)))
    out_shape += [jax.ShapeDtypeStruct((b, t, MLA_HEADS * LANES), BF16),
                  jax.ShapeDtypeStruct((b, t // tm, MLA_HEADS, LANES), F32),
                  jax.ShapeDtypeStruct((b, MLA_HEADS * V_ROWS, t), BF16)]
    out_specs += [pl.BlockSpec((1, tm, MLA_HEADS * LANES), lambda bi, i: (bi, i, 0)),
                  pl.BlockSpec((1, 1, MLA_HEADS, LANES), lambda bi, i: (bi, i, 0, 0)),
                  pl.BlockSpec((1, MLA_HEADS * V_ROWS, tm), lambda bi, i: (bi, 0, i))]
    return pl.pallas_call(
        functools.partial(_l1_proj_body, with_q=with_q), grid=(b, t // tm), in_specs=in_specs,
        out_specs=out_specs, out_shape=out_shape, name=name,
        compiler_params=_cparams(("parallel", "parallel")),
    )(*args)


def _dense_body(*refs, heads, k_slot, v_row, nk, has_ctx):
    if has_ctx:
        qt_ref, k_ref, vt_ref, kc_ref, vct_ref, out_ref, m_ref, l_ref, acc_ref = refs
    else:
        qt_ref, k_ref, vt_ref, out_ref, m_ref, l_ref, acc_ref = refs
    ki = pl.program_id(2)
    last = nk if has_ctx else nk - 1

    @pl.when(ki == 0)
    def _():
        m_ref[...] = jnp.full(m_ref.shape, MASK_VALUE, F32)
        l_ref[...] = jnp.zeros(l_ref.shape, F32)
        acc_ref[...] = jnp.zeros(acc_ref.shape, F32)

    def step(kr, vr):
        def scores(h):
            ks = k_slot[h]
            return _dot(kr[0, :, ks * LANES:(ks + 1) * LANES], qt_ref[0, h * LANES:(h + 1) * LANES, :])

        pending = [scores(h) for h in range(min(AHEAD, heads))]
        for h in range(heads):
            s = pending.pop(0)
            if h + AHEAD < heads:
                pending.append(scores(h + AHEAD))
            rows = slice(h * HEAD_DIM, (h + 1) * HEAD_DIM)
            m_prev = m_ref[h:h + 1, :]
            m_new = jnp.maximum(m_prev, jnp.max(s, axis=0, keepdims=True))
            alpha = jnp.exp2(m_prev - m_new)
            p = jnp.exp2(s - m_new).astype(BF16)
            pv = _dot(vr[0, v_row[h]:v_row[h] + V_ROWS, :], p)
            m_ref[h:h + 1, :] = m_new
            l_ref[h:h + 1, :] = alpha * l_ref[h:h + 1, :] + pv[HEAD_DIM:HEAD_DIM + 1]
            acc_ref[rows, :] = alpha * acc_ref[rows, :] + pv[:HEAD_DIM]

    if has_ctx:
        @pl.when(ki < nk)
        def _():
            step(k_ref, vt_ref)

        @pl.when(ki == nk)
        def _():
            step(kc_ref, vct_ref)
    else:
        step(k_ref, vt_ref)

    @pl.when(ki == last)
    def _():
        for h in range(heads):
            rows = slice(h * HEAD_DIM, (h + 1) * HEAD_DIM)
            acc_ref[rows, :] = acc_ref[rows, :] * (1.0 / l_ref[h:h + 1, :])
        out_ref[0] = acc_ref[...].T.astype(out_ref.dtype)


def _dense_attn(qt, k, vt, kc=None, vct=None, *, heads, k_slot, v_row, tq, tk, name):
    b, nq, lq = qt.shape
    _, lk, nkw = k.shape
    nv = vt.shape[1]
    assert nq == heads * LANES and lq % tq == 0 and lk % tk == 0
    nk = lk // tk
    has_ctx = kc is not None
    steps = nk + (1 if has_ctx else 0)
    in_specs = [
        pl.BlockSpec((1, nq, tq), lambda bi, qi, ki: (bi, 0, qi)),
        pl.BlockSpec((1, tk, nkw), lambda bi, qi, ki: (bi, jnp.minimum(ki, nk - 1), 0)),
        pl.BlockSpec((1, nv, tk), lambda bi, qi, ki: (bi, 0, jnp.minimum(ki, nk - 1))),
    ]
    args = [qt, k, vt]
    if has_ctx:
        lc = kc.shape[1]
        in_specs += [pl.BlockSpec((1, lc, nkw), lambda bi, qi, ki: (bi, 0, 0)),
                     pl.BlockSpec((1, nv, lc), lambda bi, qi, ki: (bi, 0, 0))]
        args += [kc, vct]
    body = functools.partial(_dense_body, heads=heads, k_slot=tuple(k_slot), v_row=tuple(v_row),
                             nk=nk, has_ctx=has_ctx)
    return pl.pallas_call(
        body, grid=(b, lq // tq, steps), in_specs=in_specs,
        out_specs=pl.BlockSpec((1, tq, heads * HEAD_DIM), lambda bi, qi, ki: (bi, qi, 0)),
        out_shape=jax.ShapeDtypeStruct((b, lq, heads * HEAD_DIM), BF16),
        scratch_shapes=[pltpu.VMEM((heads, tq), F32), pltpu.VMEM((heads, tq), F32),
                        pltpu.VMEM((heads * HEAD_DIM, tq), F32)],
        name=name, compiler_params=_cparams(("parallel", "parallel", "arbitrary")),
    )(*args)


def _dense_fast_body(qt_ref, k_ref, vt_ref, kc_ref, vct_ref, knl_ref, knc_ref, out_ref, lmin_ref,
                     b_ref, acc_ref, o_ref, *, heads, k_slot, v_row, nk):
    ki = pl.program_id(2)
    tq = qt_ref.shape[2]

    @pl.when(ki == 0)
    def _():
        kn = jnp.sqrt(jnp.maximum(jnp.max(knl_ref[0], axis=0), knc_ref[0, 0])) * BOUND_SLACK
        for h in range(heads):
            qf = qt_ref[0, h * LANES:(h + 1) * LANES, :].astype(F32)
            qn = jnp.sqrt(jnp.sum(qf * qf, axis=0, keepdims=True))
            krow = kn[k_slot[h]:k_slot[h] + 1, :]
            b_ref[h:h + 1, :] = qn * jnp.concatenate([krow] * (tq // LANES), axis=1)
        acc_ref[...] = jnp.zeros(acc_ref.shape, F32)

    def step(kr, vr):
        def scores(h):
            ks = k_slot[h]
            return _dot(kr[0, :, ks * LANES:(ks + 1) * LANES], qt_ref[0, h * LANES:(h + 1) * LANES, :])

        pending = [scores(h) for h in range(min(AHEAD, heads))]
        for h in range(heads):
            s = pending.pop(0)
            if h + AHEAD < heads:
                pending.append(scores(h + AHEAD))
            p = jnp.exp2(s - b_ref[h:h + 1, :]).astype(BF16)
            rows = slice(h * V_ROWS, (h + 1) * V_ROWS)
            acc_ref[rows, :] += _dot(vr[0, v_row[h]:v_row[h] + V_ROWS, :], p)

    @pl.when(ki < nk)
    def _():
        step(k_ref, vt_ref)

    @pl.when(ki == nk)
    def _():
        step(kc_ref, vct_ref)
        lmin = None
        for h in range(heads):
            l = acc_ref[h * V_ROWS + HEAD_DIM:h * V_ROWS + HEAD_DIM + 1, :]
            lmin = l if lmin is None else jnp.minimum(lmin, l)
            o_ref[h * HEAD_DIM:(h + 1) * HEAD_DIM, :] = acc_ref[h * V_ROWS:h * V_ROWS + HEAD_DIM, :] * (1.0 / l)
        out_ref[0] = o_ref[...].T.astype(out_ref.dtype)
        lmin_ref[0, 0] = jnp.broadcast_to(jnp.min(lmin, axis=1, keepdims=True), lmin_ref.shape[2:])


def _dense_attn_fast(qt, k, vt, kc, vct, knl, knc, *, heads, k_slot, v_row, tq, tk, name):
    b, nq, lq = qt.shape
    _, lk, nkw = k.shape
    nv = vt.shape[1]
    lc = kc.shape[1]
    nk = lk // tk
    assert nq == heads * LANES and lq % tq == 0 and lk % tk == 0
    n_tiles, n_slots = knl.shape[1:3]
    in_specs = [
        pl.BlockSpec((1, nq, tq), lambda bi, qi, ki: (bi, 0, qi)),
        pl.BlockSpec((1, tk, nkw), lambda bi, qi, ki: (bi, jnp.minimum(ki, nk - 1), 0)),
        pl.BlockSpec((1, nv, tk), lambda bi, qi, ki: (bi, 0, jnp.minimum(ki, nk - 1))),
        pl.BlockSpec((1, lc, nkw), lambda bi, qi, ki: (bi, 0, 0)),
        pl.BlockSpec((1, nv, lc), lambda bi, qi, ki: (bi, 0, 0)),
        pl.BlockSpec((1, n_tiles, n_slots, LANES), lambda bi, qi, ki: (bi, 0, 0, 0)),
        pl.BlockSpec((1, 1, n_slots, LANES), lambda bi, qi, ki: (bi, 0, 0, 0)),
    ]
    body = functools.partial(_dense_fast_body, heads=heads, k_slot=tuple(k_slot), v_row=tuple(v_row), nk=nk)
    return pl.pallas_call(
        body, grid=(b, lq // tq, nk + 1), in_specs=in_specs,
        out_specs=[pl.BlockSpec((1, tq, heads * HEAD_DIM), lambda bi, qi, ki: (bi, qi, 0)),
                   pl.BlockSpec((1, 1, SUBLANES, LANES), lambda bi, qi, ki: (bi, qi, 0, 0))],
        out_shape=[jax.ShapeDtypeStruct((b, lq, heads * HEAD_DIM), BF16),
                   jax.ShapeDtypeStruct((b, lq // tq, SUBLANES, LANES), F32)],
        scratch_shapes=[pltpu.VMEM((heads, tq), F32), pltpu.VMEM((heads * V_ROWS, tq), F32),
                        pltpu.VMEM((heads * HEAD_DIM, tq), F32)],
        name=name, compiler_params=_cparams(("parallel", "parallel", "arbitrary")),
    )(qt, k, vt, kc, vct, knl, knc)


def _attend(qt, k, vt, kc, vct, knl, knc, *, name, **kw):
    y, lmin = _dense_attn_fast(qt, k, vt, kc, vct, knl, knc, name=name + "_fast", **kw)
    return lax.cond(jnp.min(lmin) > L_MIN_OK, lambda: y,
                    lambda: _dense_attn(qt, k, vt, kc, vct, name=name, **kw))


NA_QROWS = 4
NA_TQ = NA_QROWS * GRID_W
NA_WROWS = 12
NA_WBLOCKS = NA_WROWS // NA_QROWS
NA_STEPS = GRID_H // NA_QROWS


def _natten_body(qt_ref, k0_ref, k1_ref, k2_ref, v0_ref, v1_ref, v2_ref, kc_ref, vct_ref, bias_ref,
                 out_ref, o_ref):
    k_refs = (k0_ref, k1_ref, k2_ref)
    v_refs = (v0_ref, v1_ref, v2_ref)

    def scores(h):
        lanes = slice((h // 2) * LANES, (h // 2 + 1) * LANES)
        qq = qt_ref[0, h * LANES:(h + 1) * LANES, :]
        return [_dot(kc_ref[0, :, lanes], qq)] + [_dot(k_refs[blk][0, :, lanes], qq) for blk in range(NA_WBLOCKS)]

    pending = [scores(h) for h in range(AHEAD)]
    for h in range(B_HEADS):
        s_all = pending.pop(0)
        if h + AHEAD < B_HEADS:
            pending.append(scores(h + AHEAD))
        vrows = slice(h * V_ROWS, (h + 1) * V_ROWS)
        s_all = [s_all[0]] + [s_all[1 + blk] + bias_ref[0, h, blk * NA_TQ:(blk + 1) * NA_TQ, :]
                              for blk in range(NA_WBLOCKS)]
        m = s_all[0].max(axis=0, keepdims=True)
        for s in s_all[1:]:
            m = jnp.maximum(m, s.max(axis=0, keepdims=True))
        pv = _dot(vct_ref[0, vrows, :], jnp.exp2(s_all[0] - m).astype(BF16))
        for blk in range(NA_WBLOCKS):
            pv = pv + _dot(v_refs[blk][0, vrows, :], jnp.exp2(s_all[1 + blk] - m).astype(BF16))
        o_ref[h * HEAD_DIM:(h + 1) * HEAD_DIM, :] = pv[:HEAD_DIM] * (1.0 / pv[HEAD_DIM:HEAD_DIM + 1])
    out_ref[0] = o_ref[...].T.astype(out_ref.dtype)


def _natten_fast_body(qt_ref, k0_ref, k1_ref, k2_ref, v0_ref, v1_ref, v2_ref, kc_ref, vct_ref, bias_ref,
                      knl_ref, knc_ref, bmax_ref, out_ref, lmin_ref, o_ref, kall_ref, vall_ref):
    kall_ref[0:CTX_LEN, :] = kc_ref[0]
    vall_ref[:, 0:CTX_LEN] = vct_ref[0]
    for blk, (kr, vr) in enumerate(((k0_ref, v0_ref), (k1_ref, v1_ref), (k2_ref, v2_ref))):
        kall_ref[CTX_LEN + blk * NA_TQ:CTX_LEN + (blk + 1) * NA_TQ, :] = kr[0]
        vall_ref[:, CTX_LEN + blk * NA_TQ:CTX_LEN + (blk + 1) * NA_TQ] = vr[0]
    kn = jnp.sqrt(jnp.maximum(jnp.max(knl_ref[0], axis=0), knc_ref[0, 0])) * BOUND_SLACK

    def scores(h):
        lanes = slice((h // 2) * LANES, (h // 2 + 1) * LANES)
        return _dot(kall_ref[:, lanes], qt_ref[0, h * LANES:(h + 1) * LANES, :])

    pending = [scores(h) for h in range(AHEAD)]
    lmin = None
    for h in range(B_HEADS):
        s = pending.pop(0)
        if h + AHEAD < B_HEADS:
            pending.append(scores(h + AHEAD))
        qf = qt_ref[0, h * LANES:(h + 1) * LANES, :].astype(F32)
        qn = jnp.sqrt(jnp.sum(qf * qf, axis=0, keepdims=True))
        krow = kn[h // 2:h // 2 + 1, :]
        shift = qn * jnp.concatenate([krow] * (NA_TQ // LANES), axis=1) + bmax_ref[h:h + 1, :]
        p = jnp.concatenate([jnp.exp2(s[:CTX_LEN] - shift).astype(BF16),
                             jnp.exp2(s[CTX_LEN:] + (bias_ref[0, h] - shift)).astype(BF16)], axis=0)
        pv = _dot(vall_ref[h * V_ROWS:(h + 1) * V_ROWS, :], p)
        l = pv[HEAD_DIM:HEAD_DIM + 1]
        lmin = l if lmin is None else jnp.minimum(lmin, l)
        o_ref[h * HEAD_DIM:(h + 1) * HEAD_DIM, :] = pv[:HEAD_DIM] * (1.0 / l)
    out_ref[0] = o_ref[...].T.astype(out_ref.dtype)
    lmin_ref[0, 0] = jnp.broadcast_to(jnp.min(lmin, axis=1, keepdims=True), lmin_ref.shape[2:])


def _natten_call(body, name, qt, k, vt, kc, vct, bias, extra=(), with_lmin=False, gather=False):
    b = qt.shape[0]
    nw = B_HEADS * HEAD_DIM
    nv = B_HEADS * V_ROWS

    def wstart(i):
        return jnp.clip(i - 1, 0, NA_STEPS - NA_WBLOCKS)

    def cls(i):
        return jnp.where(i == 0, 0, jnp.where(i == NA_STEPS - 1, 2, 1))

    in_specs = [pl.BlockSpec((1, B_HEADS * LANES, NA_TQ), lambda bi, i: (bi, 0, i))]
    in_specs += [pl.BlockSpec((1, NA_TQ, nw), functools.partial(lambda bi, i, j: (bi, wstart(i) + j, 0), j=j))
                 for j in range(NA_WBLOCKS)]
    in_specs += [pl.BlockSpec((1, nv, NA_TQ), functools.partial(lambda bi, i, j: (bi, 0, wstart(i) + j), j=j))
                 for j in range(NA_WBLOCKS)]
    in_specs += [pl.BlockSpec((1, CTX_LEN, nw), lambda bi, i: (bi, 0, 0)),
                 pl.BlockSpec((1, nv, CTX_LEN), lambda bi, i: (bi, 0, 0)),
                 pl.BlockSpec((1, B_HEADS, NA_WROWS * GRID_W, NA_TQ), lambda bi, i: (cls(i), 0, 0, 0))]
    for a, per_batch in extra:
        blk = (1,) + a.shape[1:] if per_batch else a.shape
        nz = len(blk) - 1
        in_specs.append(pl.BlockSpec(blk, (lambda bi, i, nz=nz: (bi,) + (0,) * nz) if per_batch
                                     else (lambda bi, i, nz=nz: (0,) * (nz + 1))))
    out_specs = [pl.BlockSpec((1, NA_TQ, nw), lambda bi, i: (bi, i, 0))]
    out_shape = [jax.ShapeDtypeStruct((b, SEQ, nw), BF16)]
    if with_lmin:
        out_specs.append(pl.BlockSpec((1, 1, SUBLANES, LANES), lambda bi, i: (bi, i, 0, 0)))
        out_shape.append(jax.ShapeDtypeStruct((b, NA_STEPS, SUBLANES, LANES), F32))
    return pl.pallas_call(
        body, grid=(b, NA_STEPS), in_specs=in_specs, out_specs=out_specs, out_shape=out_shape,
        scratch_shapes=[pltpu.VMEM((nw, NA_TQ), F32)] + ([pltpu.VMEM((CTX_LEN + NA_WROWS * GRID_W, nw), BF16),
                                                          pltpu.VMEM((nv, CTX_LEN + NA_WROWS * GRID_W), BF16)] if gather else []),
        name=name, compiler_params=_cparams(("parallel", "arbitrary")),
    )(qt, k, k, k, vt, vt, vt, kc, vct, bias, *[a for a, _ in extra])


def _natten(qt, k, vt, kc, vct, rpb, knl, knc):
    bias = _natten_bias(rpb)
    bmax = jnp.maximum(jnp.max(rpb.astype(F32), axis=(1, 2)), 0.0) * LOG2E
    bmax = jnp.broadcast_to(bmax[:, None], (B_HEADS, NA_TQ))
    y, lmin = _natten_call(_natten_fast_body, "natten_fast", qt, k, vt, kc, vct, bias,
                           extra=((knl, True), (knc, True), (bmax, False)), with_lmin=True, gather=True)
    return lax.cond(jnp.min(lmin) > L_MIN_OK, lambda: y,
                    lambda: _natten_call(_natten_body, "natten", qt, k, vt, kc, vct, bias)[0])


def _natten_bias(rpb):
    v = rpb.astype(F32) * LOG2E
    h, nr, _ = v.shape
    period = 2 * GRID_W
    row = jnp.zeros((h, nr, period), F32)
    row = row.at[..., :NA_COLS].set(v[..., NA_COLS - 1::-1])
    row = row.at[..., period - (NA_COLS - 1):].set(v[..., :NA_COLS - 1:-1])
    toep = jnp.tile(row, (1, 1, GRID_W))[..., :GRID_W * (period - 1)]
    toep = toep.reshape(h, nr, GRID_W, period - 1)[..., :GRID_W]
    col = np.arange(GRID_W)
    cs = np.clip(col - NA_COLS // 2, 0, GRID_W - NA_COLS)
    col_ok = (col[:, None] >= cs[None, :]) & (col[:, None] < cs[None, :] + NA_COLS)
    masked = 2.0 * MASK_VALUE
    toep = jnp.where(jnp.asarray(col_ok), toep, masked)
    dead = jnp.full((h, GRID_W, GRID_W), masked, F32)
    tables = []
    for r0 in (0, 2 * NA_QROWS, GRID_H - NA_QROWS):
        ws = min(max(r0 - NA_QROWS, 0), GRID_H - NA_WROWS)
        key_rows = []
        for ki in range(NA_WROWS):
            blocks = []
            for qi in range(NA_QROWS):
                k_row, q_row = ws + ki, r0 + qi
                rs = min(max(q_row - NA_ROWS // 2, 0), GRID_H - NA_ROWS)
                blocks.append(toep[:, k_row - q_row + NA_ROWS - 1] if rs <= k_row < rs + NA_ROWS else dead)
            key_rows.append(jnp.concatenate(blocks, axis=-1))
        tables.append(jnp.concatenate(key_rows, axis=-2))
    return jnp.stack(tables)


def _ffn_body(x_ref, xp_ref, xn_ref, mod_ref, wg_ref, wv_ref, wd_ref, cw_ref, cb_ref, lng_ref, lnb_ref,
              out_ref, hs_ref, g_ref, *, tm, rc, n_tiles):
    i = pl.program_id(1)
    sc = mod_ref[0, SC2:SC2 + 1, :]
    sh = mod_ref[0, SH2:SH2 + 1, :]
    gate_mod = mod_ref[0, G2:G2 + 1, :]

    def modulate(v):
        return v * (1.0 + sc) + sh

    hp = jnp.where(i > 0, modulate(xp_ref[0]), 0.0)
    hn = jnp.where(i < n_tiles - 1, modulate(xn_ref[0]), 0.0)
    hs_ref[0:FFN_HALO, :] = hp.astype(BF16)
    hs_ref[FFN_HALO:FFN_HALO + tm, :] = modulate(x_ref[0]).astype(BF16)
    hs_ref[FFN_HALO + tm:, :] = hn.astype(BF16)

    def up(c):
        lo = c * rc
        g_ref[c] = _dot(hs_ref[lo:lo + rc + 2 * FFN_HALO, :], wg_ref[...])
        return _dot(hs_ref[FFN_HALO + lo:FFN_HALO + lo + rc, :], wv_ref[...])

    n_chunks = tm // rc
    val_next = up(0)
    for c in range(n_chunks):
        val = val_next
        if c + 1 < n_chunks:
            val_next = up(c + 1)
        gate = (cw_ref[0:1, :] * g_ref[c, FFN_HALO - 1:FFN_HALO - 1 + rc, :]
                + cw_ref[1:2, :] * g_ref[c, FFN_HALO:FFN_HALO + rc, :]
                + cw_ref[2:3, :] * g_ref[c, FFN_HALO + 1:FFN_HALO + 1 + rc, :]
                + cb_ref[...])
        act = gate * jax.nn.sigmoid(gate) * val
        y = _dot(act.astype(BF16), wd_ref[...])
        rows = slice(c * rc, (c + 1) * rc)
        out_ref[0, rows, :] = _layer_norm(DEEPNORM_ALPHA * x_ref[0, rows, :] + gate_mod * y, lng_ref[...], lnb_ref[...])


def _ffn(x, mod, mod_row, wup, wd, conv_w, conv_b, ln_g, ln_b, *, tm, name):
    b, t, d = x.shape
    n_tiles = t // tm
    rc = min(FFN_ROWS, tm)
    hb = tm // FFN_HALO
    n_hblocks = t // FFN_HALO
    mod_map = (lambda bi, i: (bi, 0, 0)) if mod_row is None else (lambda bi, i: (mod_row, 0, 0))

    def resident(shape, col=0):
        return pl.BlockSpec(shape, lambda bi, i: (0, col), pipeline_mode=pl.Buffered(1))

    in_specs = [
        pl.BlockSpec((1, tm, d), lambda bi, i: (bi, i, 0)),
        pl.BlockSpec((1, FFN_HALO, d), lambda bi, i: (bi, jnp.maximum(i * hb - 1, 0), 0)),
        pl.BlockSpec((1, FFN_HALO, d), lambda bi, i: (bi, jnp.minimum((i + 1) * hb, n_hblocks - 1), 0)),
        pl.BlockSpec((1, N_MOD, d), mod_map),
        resident((d, D_FF), 0), resident((d, D_FF), 1), resident((D_FF, d)),
        resident((3, D_FF)), resident((1, D_FF)), resident((1, d)), resident((1, d)),
    ]
    body = functools.partial(_ffn_body, tm=tm, rc=rc, n_tiles=n_tiles)
    return pl.pallas_call(
        body, grid=(b, n_tiles), in_specs=in_specs,
        out_specs=pl.BlockSpec((1, tm, d), lambda bi, i: (bi, i, 0)),
        out_shape=jax.ShapeDtypeStruct((b, t, d), F32),
        scratch_shapes=[pltpu.VMEM((tm + 2 * FFN_HALO, d), BF16),
                        pltpu.VMEM((tm // rc, rc + 2 * FFN_HALO, D_FF), F32)],
        name=name, compiler_params=_cparams(("parallel", "parallel")),
    )(x, x, x, mod, wup, wup, wd, conv_w.astype(F32), conv_b.reshape(1, D_FF).astype(F32),
      ln_g.reshape(1, d).astype(F32), ln_b.reshape(1, d).astype(F32))


def _rope_cos_sin(rot_dim):
    pos = jnp.arange(SEQ, dtype=jnp.int32)
    rows = (pos // GRID_W).astype(F32)
    cols = (pos % GRID_W).astype(F32)
    axis_dim = rot_dim // 2
    inv = ROPE_THETA ** (-jnp.arange(0, axis_dim, 2, dtype=F32) / axis_dim)
    ang = jnp.concatenate([rows[:, None] * inv, cols[:, None] * inv], axis=-1)
    cos = jnp.repeat(jnp.cos(ang), 2, axis=-1)
    sin = jnp.repeat(jnp.sin(ang), 2, axis=-1) * jnp.tile(jnp.array([-1.0, 1.0], F32), rot_dim // 2)
    return cos, sin


def _swap_pairs(g):
    return g.reshape(-1, 2)[:, ::-1].reshape(-1)


def _gqa_tables(gain, scale, length, rope):
    if rope:
        cos, sin = _rope_cos_sin(HEAD_DIM)
    else:
        cos, sin = jnp.ones((length, HEAD_DIM), F32), jnp.zeros((length, HEAD_DIM), F32)
    c = cos * (gain * scale)
    s = sin * (_swap_pairs(gain) * scale)
    return jnp.concatenate([c, c], axis=-1), jnp.concatenate([s, s], axis=-1)


def _mla_tables(scale, length, rope):
    c = jnp.full((length, LANES), scale, F32)
    s = jnp.zeros((length, LANES), F32)
    if rope:
        cos, sin = _rope_cos_sin(MLA_ROPE_DIM)
        lo, hi = MLA_NOPE_DIM, MLA_NOPE_DIM + MLA_ROPE_DIM
        c = c.at[:, lo:hi].set(cos * scale)
        s = s.at[:, lo:hi].set(sin * scale)
    return c, s


def _slot_rows(w, heads, width, half_of):
    k = w.shape[0]
    out = jnp.zeros((heads * LANES, k), w.dtype)
    for h in range(heads):
        off = h * LANES + half_of(h) * HALF
        out = out.at[off:off + width].set(w[:, h * width:(h + 1) * width].T)
    return out


def _layer0_mixer(x, xc, mod, w_in, q_gain, k_gain, rpb):
    w = w_in.astype(BF16)
    aq, akv, bw = A_HEADS * HEAD_DIM, A_KV_HEADS * HEAD_DIM, B_HEADS * HEAD_DIM
    o = 0
    w_qa = w[:, o:o + aq]; o += aq
    w_ka = w[:, o:o + akv]; o += akv
    w_va = w[:, o:o + akv]; o += akv
    w_qb = w[:, o:o + bw]; o += bw
    w_kb = w[:, o:o + bw]; o += bw
    w_vb = w[:, o:o + bw]
    wqa = jnp.concatenate([w_qa, w_va], axis=1).T
    wqb = jnp.concatenate([w_qb, w_vb], axis=1).T
    wka = jnp.concatenate([w_ka, w_kb], axis=1)
    weights = (wqa, wka, wqb)
    qscale = HEAD_SCALE * LOG2E

    def tables(length, rope):
        cq, sq = _gqa_tables(q_gain, qscale, length, rope)
        ck, sk = _gqa_tables(k_gain, 1.0, length, rope)
        return cq[:, :HEAD_DIM].T, sq[:, :HEAD_DIM].T, ck, sk

    qa_t, ka, kna, va_t, qb_t, kb, knb, vb_t = _l0_proj(x, mod, None, weights, tables(SEQ, True), tm=TM_WIDE, name="l0_proj")
    qac_t, kac, knac, vac_t, qbc_t, kbc, knbc, vbc_t = _l0_proj(xc, mod, CTX_MOD_ROW, weights, tables(CTX_LEN, False),
                                                    tm=TMC, name="l0_proj_ctx")

    a_kslot = [0] * A_HEADS
    a_vrow = [(h // A_GROUP) * V_ROWS for h in range(A_HEADS)]
    b_kslot = [h // 2 for h in range(B_HEADS)]
    b_vrow = [h * V_ROWS for h in range(B_HEADS)]
    ya = _attend(qa_t, ka, va_t, kac, vac_t, kna, knac, heads=A_HEADS, k_slot=a_kslot, v_row=a_vrow,
                 tq=TQ, tk=TK, name="gqa")
    yb = _natten(qb_t, kb, vb_t, kbc, vbc_t, rpb, knb, knbc)
    yac = _dense_attn(qac_t, kac, vac_t, heads=A_HEADS, k_slot=a_kslot, v_row=a_vrow,
                      tq=CTX_LEN, tk=CTX_LEN, name="gqa_ctx")
    ybc = _dense_attn(qbc_t, kbc, vbc_t, heads=B_HEADS, k_slot=b_kslot, v_row=b_vrow,
                      tq=CTX_LEN, tk=CTX_LEN, name="nbr_ctx")
    return (ya, yb), (yac, ybc)


def _layer1_mixer(x, xc, mod, w_in, cq_gain, ckv_gain, w_uq, w_ukv):
    qk_dim = MLA_NOPE_DIM + MLA_ROPE_DIM
    kv_dim = MLA_NOPE_DIM + MLA_V_DIM
    n_in = MLA_Q_LORA + MLA_KV_LORA
    wdn = jnp.zeros((D_MODEL, n_in + LANES), BF16)
    wdn = wdn.at[:, :n_in].set(w_in[:, :n_in].astype(BF16))
    wdn = wdn.at[:, n_in + MLA_NOPE_DIM:n_in + qk_dim].set(w_in[:, n_in:].astype(BF16))
    wq = _slot_rows(w_uq.astype(BF16), MLA_HEADS, qk_dim, lambda h: 0)
    wkv = w_ukv.astype(BF16).reshape(MLA_KV_LORA, MLA_HEADS, kv_dim)
    wk = _slot_rows(wkv[:, :, :MLA_NOPE_DIM].reshape(MLA_KV_LORA, -1), MLA_HEADS, MLA_NOPE_DIM, lambda h: 0).T
    wv = wkv[:, :, MLA_NOPE_DIM:].reshape(MLA_KV_LORA, MLA_HEADS * MLA_V_DIM).T
    weights = (wdn, wq, wk, wv)
    gains = (cq_gain.reshape(1, -1).astype(F32), ckv_gain.reshape(1, -1).astype(F32))

    cq, sq = _mla_tables(MLA_SCALE * LOG2E, SEQ, True)
    qt, kk, kn, vt = _l1_proj(x, mod, None, weights, gains, (cq.T, sq.T), _mla_tables(1.0, SEQ, True),
                          tm=TM, with_q=True, name="l1_proj")
    kkc, knc, vtc = _l1_proj(xc, mod, CTX_MOD_ROW, weights, gains, None, _mla_tables(1.0, CTX_LEN, False),
                        tm=TMC, with_q=False, name="l1_proj_ctx")
    return _attend(qt, kk, vt, kkc, vtc, kn, knc, heads=MLA_HEADS, k_slot=list(range(MLA_HEADS)),
                   v_row=[h * V_ROWS for h in range(MLA_HEADS)], tq=TQ, tk=TK, name="mla")


def _post_mixer(x, y, mod, mod_row, w_out, ln1, w_up, conv_w, conv_b, w_down, ln2, tm, tag):
    wo = w_out.astype(BF16)
    if isinstance(y, tuple):
        n0 = y[0].shape[2]
        extra = dict(x2=y[1], w2=wo[n0:])
        y, wo = y[0], wo[:n0]
    else:
        extra = {}
    x = _proj(y, wo, name="out_ln" + tag, tm=min(TM_WIDE, x.shape[1]), tn=D_MODEL, mod=mod, mod_row=mod_row,
              gate_row=G1, resid=x, ln=ln1, **extra)
    return _ffn(x, mod, mod_row, w_up.astype(BF16), w_down.astype(BF16), conv_w, conv_b, ln2[0], ln2[1],
                tm=tm, name="ffn" + tag)


def kernel(x, c, ctx, c_ctx, l0_w_ada, l0_b_ada, l0_w_in, l0_q_gain, l0_k_gain, l0_rpb, l0_w_out, l0_ln1_g, l0_ln1_b, l0_w_up, l0_conv_w, l0_conv_b, l0_w_down, l0_ln2_g, l0_ln2_b, l1_w_ada, l1_b_ada, l1_w_in, l1_cq_gain, l1_ckv_gain, l1_w_uq, l1_w_ukv, l1_w_out, l1_ln1_g, l1_ln1_b, l1_w_up, l1_conv_w, l1_conv_b, l1_w_down, l1_ln2_g, l1_ln2_b):
    xc = ctx
    mod = _modvec(c, c_ctx, l0_w_ada, l0_b_ada)
    y, yc = _layer0_mixer(x, xc, mod, l0_w_in, l0_q_gain, l0_k_gain, l0_rpb)
    post0 = (l0_w_out, (l0_ln1_g, l0_ln1_b), l0_w_up, l0_conv_w, l0_conv_b, l0_w_down, (l0_ln2_g, l0_ln2_b))
    x = _post_mixer(x, y, mod, None, *post0, tm=TM, tag="0")
    xc = _post_mixer(xc, yc, mod, CTX_MOD_ROW, *post0, tm=TMC, tag="0_ctx")
    mod = _modvec(c, c_ctx, l1_w_ada, l1_b_ada)
    y = _layer1_mixer(x, xc, mod, l1_w_in, l1_cq_gain, l1_ckv_gain, l1_w_uq, l1_w_ukv)
    post1 = (l1_w_out, (l1_ln1_g, l1_ln1_b), l1_w_up, l1_conv_w, l1_conv_b, l1_w_down, (l1_ln2_g, l1_ln2_b))
    return _post_mixer(x, y, mod, None, *post1, tm=TM, tag="1")
```

```python
import functools
import math

import numpy as np
import jax
import jax.numpy as jnp
from jax import lax
from jax.experimental import pallas as pl
from jax.experimental.pallas import tpu as pltpu

D_MODEL = 1024
BATCH = 4
SEQ = 4096
DEPTH = 2
GRID_W = 64
GRID_H = SEQ // GRID_W
CTX_LEN = 256
HEAD_DIM = 64
A_HEADS = 8
A_KV_HEADS = 2
A_GROUP = A_HEADS // A_KV_HEADS
B_HEADS = 8
NA_ROWS = 8
NA_COLS = 16
ROPE_THETA = 10000.0
MLA_HEADS = 16
MLA_Q_LORA = 768
MLA_KV_LORA = 256
MLA_NOPE_DIM = 64
MLA_ROPE_DIM = 32
MLA_V_DIM = 64
D_FF = 2816
N_MOD = 6
EPS = 1e-6
DEEPNORM_ALPHA = (2 * DEPTH) ** 0.25
HEAD_SCALE = HEAD_DIM ** -0.5
MLA_SCALE = (MLA_NOPE_DIM + MLA_ROPE_DIM) ** -0.5
LOG2E = math.log2(math.e)

LANES = 128
SUBLANES = 8
HALF = LANES // 2
BF16_ROWS = 16
V_ROWS = HEAD_DIM + BF16_ROWS
MASK_VALUE = -1e30
VMEM_LIMIT = 56 * 1024 * 1024

SH1, SC1, G1, SH2, SC2, G2 = range(N_MOD)

F32 = jnp.float32
BF16 = jnp.bfloat16

MOD_ROWS = 16
CTX_MOD_ROW = BATCH
TM = 512
TM_WIDE = 1024
TMC = CTX_LEN
TQ = 1024
TK = 512
AHEAD = 2
BOUND_SLACK = 1.0 + 1e-3
L_MIN_OK = 2.0 ** -80
FFN_ROWS = 256
FFN_HALO = BF16_ROWS
ROW_CHUNK = 256


def _cparams(sem):
    return pltpu.CompilerParams(dimension_semantics=sem, vmem_limit_bytes=VMEM_LIMIT)


def _dot(a, b):
    return jnp.dot(a, b, preferred_element_type=F32)


def _dot_nt(a, b):
    return lax.dot_general(a, b, (((1,), (1,)), ((), ())), preferred_element_type=F32)


def _pair_swap(y, axis):
    n = y.shape[axis]
    idx = lax.broadcasted_iota(jnp.int32, y.shape, axis)
    nxt = pltpu.roll(y, n - 1, axis=axis)
    prv = pltpu.roll(y, 1, axis=axis)
    return jnp.where(idx % 2 == 0, nxt, prv)


def _rotate(y, c, s, axis):
    return y * c + _pair_swap(y, axis) * s


def _layer_norm(z, g, b):
    mu = jnp.mean(z, axis=-1, keepdims=True)
    zc = z - mu
    var = jnp.mean(zc * zc, axis=-1, keepdims=True)
    return zc * lax.rsqrt(var + EPS) * g + b


def _slot_norm_rows(kb, n_slots):
    rows = []
    for sl in range(n_slots):
        kf = kb[:, sl * LANES:(sl + 1) * LANES].astype(F32)
        n2 = jnp.sum(kf * kf, axis=-1, keepdims=True)
        rows.append(jnp.broadcast_to(jnp.max(n2, axis=0, keepdims=True), (1, LANES)))
    return rows[0] if n_slots == 1 else jnp.concatenate(rows, axis=0)


def _store_vt(out_ref, yt, heads):
    ones = jnp.ones((BF16_ROWS, yt.shape[1]), out_ref.dtype)
    for h in range(heads):
        out_ref[0, h * V_ROWS:h * V_ROWS + HEAD_DIM, :] = yt[h * HEAD_DIM:(h + 1) * HEAD_DIM].astype(out_ref.dtype)
        out_ref[0, h * V_ROWS + HEAD_DIM:(h + 1) * V_ROWS, :] = ones


def _proj_body(*refs, silu, bias, resid_ln, gate_row, two):
    it = iter(refs)
    x_ref = next(it)
    w_ref = next(it)
    x2_ref = next(it) if two else None
    w2_ref = next(it) if two else None
    bias_ref = next(it) if bias else None
    mod_ref = next(it) if resid_ln else None
    xres_ref = next(it) if resid_ln else None
    lng_ref = next(it) if resid_ln else None
    lnb_ref = next(it) if resid_ln else None
    out_ref = next(it)
    x = x_ref[0]
    if silu:
        x = x * jax.nn.sigmoid(x)
    y = _dot(x.astype(BF16), w_ref[...].astype(BF16))
    if two:
        y = y + _dot(x2_ref[0], w2_ref[...])
    if bias:
        y = y + bias_ref[...]
    if resid_ln:
        g = mod_ref[0, gate_row:gate_row + 1, :]
        y = _layer_norm(DEEPNORM_ALPHA * xres_ref[0] + g * y, lng_ref[...], lnb_ref[...])
    out_ref[0] = y.astype(out_ref.dtype)


def _proj(x, w, *, name, tm, tn, x2=None, w2=None, silu=False, bias=None, mod=None, mod_row=None,
          gate_row=None, resid=None, ln=None, out_dtype=F32):
    b, t, k = x.shape
    n = w.shape[1]
    assert t % tm == 0 and n % tn == 0
    resid_ln = resid is not None
    in_specs = [pl.BlockSpec((1, tm, k), lambda bi, i, j: (bi, i, 0)),
                pl.BlockSpec((k, tn), lambda bi, i, j: (0, j))]
    args = [x, w]
    if x2 is not None:
        k2 = x2.shape[2]
        in_specs += [pl.BlockSpec((1, tm, k2), lambda bi, i, j: (bi, i, 0)),
                     pl.BlockSpec((k2, tn), lambda bi, i, j: (0, j))]
        args += [x2, w2]
    if bias is not None:
        in_specs.append(pl.BlockSpec((1, tn), lambda bi, i, j: (0, j)))
        args.append(bias.reshape(1, n).astype(F32))
    if resid_ln:
        mod_map = (lambda bi, i, j: (bi, 0, 0)) if mod_row is None else (lambda bi, i, j: (mod_row, 0, 0))
        in_specs += [pl.BlockSpec((1, N_MOD, D_MODEL), mod_map),
                     pl.BlockSpec((1, tm, n), lambda bi, i, j: (bi, i, 0)),
                     pl.BlockSpec((1, n), lambda bi, i, j: (0, 0)),
                     pl.BlockSpec((1, n), lambda bi, i, j: (0, 0))]
        args += [mod, resid, ln[0].reshape(1, n).astype(F32), ln[1].reshape(1, n).astype(F32)]
    body = functools.partial(_proj_body, silu=silu, bias=bias is not None, resid_ln=resid_ln, gate_row=gate_row,
                             two=x2 is not None)
    return pl.pallas_call(
        body, grid=(b, t // tm, n // tn), in_specs=in_specs,
        out_specs=pl.BlockSpec((1, tm, tn), lambda bi, i, j: (bi, i, j)),
        out_shape=jax.ShapeDtypeStruct((b, t, n), out_dtype), name=name,
        compiler_params=_cparams(("parallel", "parallel", "arbitrary")),
    )(*args)


def _modvec(c, c_ctx, w_ada, b_ada):
    cond = jnp.zeros((1, MOD_ROWS, D_MODEL), F32).at[0, :BATCH].set(c).at[0, BATCH].set(c_ctx)
    m = _proj(cond, w_ada, name="adaln", tm=MOD_ROWS, tn=1024, silu=True, bias=b_ada)
    return m.reshape(MOD_ROWS, N_MOD, D_MODEL)


def _l0_proj_body(x_ref, mod_ref, wqa_ref, cqa_ref, sqa_ref, wka_ref, cka_ref, ska_ref, wqb_ref,
                  qa_ref, ka_ref, kna_ref, va_ref, qb_ref, kb_ref, knb_ref, vb_ref):
    sc = mod_ref[0, SC1:SC1 + 1, :]
    sh = mod_ref[0, SH1:SH1 + 1, :]
    xs = (x_ref[0] * (1.0 + sc) + sh).astype(BF16)

    cq, sq = cqa_ref[...], sqa_ref[...]
    zeros = jnp.zeros((HALF, xs.shape[0]), qa_ref.dtype)
    yt_a = _dot_nt(wqa_ref[...], xs)
    yt = yt_a[:A_HEADS * HEAD_DIM]
    for h in range(A_HEADS):
        seg = yt[h * HEAD_DIM:(h + 1) * HEAD_DIM]
        ms = jnp.mean(seg * seg, axis=0, keepdims=True)
        q = (_rotate(seg, cq, sq, 0) * lax.rsqrt(ms + EPS)).astype(qa_ref.dtype)
        half = h // A_GROUP
        qa_ref[0, h * LANES + half * HALF:h * LANES + (half + 1) * HALF, :] = q
        qa_ref[0, h * LANES + (1 - half) * HALF:h * LANES + (2 - half) * HALF, :] = zeros

    y_k = _dot(xs, wka_ref[...])
    y = y_k[:, :LANES]
    lo = lax.broadcasted_iota(jnp.int32, y.shape, 1) < HALF
    ysq = y * y
    ms_lo = jnp.sum(jnp.where(lo, ysq, 0.0), axis=-1, keepdims=True) * (1.0 / HEAD_DIM)
    ms_hi = jnp.sum(jnp.where(lo, 0.0, ysq), axis=-1, keepdims=True) * (1.0 / HEAD_DIM)
    rn = jnp.where(lo, lax.rsqrt(ms_lo + EPS), lax.rsqrt(ms_hi + EPS))
    ka = (_rotate(y, cka_ref[...], ska_ref[...], 1) * rn).astype(ka_ref.dtype)
    ka_ref[0] = ka
    kna_ref[0, 0] = jnp.broadcast_to(_slot_norm_rows(ka, 1), kna_ref.shape[2:])

    _store_vt(va_ref, yt_a[A_HEADS * HEAD_DIM:], A_KV_HEADS)

    yt_b = _dot_nt(wqb_ref[...], xs)
    yt = yt_b[:B_HEADS * HEAD_DIM] * (HEAD_SCALE * LOG2E)
    for h in range(B_HEADS):
        half = h % 2
        qb_ref[0, h * LANES + half * HALF:h * LANES + (half + 1) * HALF, :] = (
            yt[h * HEAD_DIM:(h + 1) * HEAD_DIM].astype(qb_ref.dtype))
        qb_ref[0, h * LANES + (1 - half) * HALF:h * LANES + (2 - half) * HALF, :] = zeros
    kb = y_k[:, LANES:].astype(kb_ref.dtype)
    kb_ref[0] = kb
    n_pairs = B_HEADS // 2
    knb_ref[0, 0] = jnp.concatenate([_slot_norm_rows(kb, n_pairs), jnp.zeros((SUBLANES - n_pairs, LANES), F32)], axis=0)
    _store_vt(vb_ref, yt_b[B_HEADS * HEAD_DIM:], B_HEADS)


def _l0_proj(x, mod, mod_row, weights, tables, *, tm, name):
    b, t, d = x.shape
    wqa, wka, wqb = weights
    cqa, sqa, cka, ska = tables
    mod_map = (lambda bi, i: (bi, 0, 0)) if mod_row is None else (lambda bi, i: (mod_row, 0, 0))

    def full(a):
        return pl.BlockSpec(a.shape, lambda bi, i: (0,) * a.ndim)

    in_specs = [pl.BlockSpec((1, tm, d), lambda bi, i: (bi, i, 0)),
                pl.BlockSpec((1, N_MOD, d), mod_map),
                full(wqa), pl.BlockSpec((HEAD_DIM, tm), lambda bi, i: (0, i)), pl.BlockSpec((HEAD_DIM, tm), lambda bi, i: (0, i)),
                full(wka), pl.BlockSpec((tm, LANES), lambda bi, i: (i, 0)), pl.BlockSpec((tm, LANES), lambda bi, i: (i, 0)),
                full(wqb)]
    nb = B_HEADS * HEAD_DIM

    def nat(n):
        return jax.ShapeDtypeStruct((b, t, n), BF16), pl.BlockSpec((1, tm, n), lambda bi, i: (bi, i, 0))

    def tr(n):
        return jax.ShapeDtypeStruct((b, n, t), BF16), pl.BlockSpec((1, n, tm), lambda bi, i: (bi, 0, i))

    kn = (jax.ShapeDtypeStruct((b, t // tm, SUBLANES, LANES), F32), pl.BlockSpec((1, 1, SUBLANES, LANES), lambda bi, i: (bi, i, 0, 0)))
    outs = [tr(A_HEADS * LANES), nat(LANES), kn, tr(A_KV_HEADS * V_ROWS), tr(B_HEADS * LANES), nat(nb), kn, tr(B_HEADS * V_ROWS)]
    return pl.pallas_call(
        _l0_proj_body, grid=(b, t // tm), in_specs=in_specs,
        out_specs=[o[1] for o in outs], out_shape=[o[0] for o in outs], name=name,
        compiler_params=_cparams(("parallel", "parallel")),
    )(x, mod, wqa, cqa, sqa, wka, cka, ska, wqb)


def _l1_proj_body(*refs, with_q):
    it = iter(refs)
    x_ref, mod_ref, wdn_ref, gq_ref, gkv_ref = (next(it) for _ in range(5))
    wq_ref, cq_ref, sq_ref = (next(it) for _ in range(3)) if with_q else (None, None, None)
    wk_ref, ck_ref, sk_ref, wv_ref = (next(it) for _ in range(4))
    q_ref = next(it) if with_q else None
    k_ref, kn_ref, v_ref = next(it), next(it), next(it)

    sc = mod_ref[0, SC1:SC1 + 1, :]
    sh = mod_ref[0, SH1:SH1 + 1, :]
    xs = (x_ref[0] * (1.0 + sc) + sh).astype(BF16)

    def rms(v, g_ref):
        ms = jnp.mean(v * v, axis=-1, keepdims=True)
        return (v * lax.rsqrt(ms + EPS) * g_ref[...]).astype(BF16)

    n_q, n_kv = MLA_Q_LORA, MLA_KV_LORA
    c_kv = rms(_dot(xs, wdn_ref[:, n_q:n_q + n_kv]), gkv_ref)
    k_r = _rotate(_dot(xs, wdn_ref[:, n_q + n_kv:]), ck_ref[...], sk_ref[...], 1)

    if with_q:
        c_q = rms(_dot(xs, wdn_ref[:, :n_q]), gq_ref)
        cq = jnp.concatenate([cq_ref[...]] * (ROW_CHUNK // LANES), axis=0)
        sq = jnp.concatenate([sq_ref[...]] * (ROW_CHUNK // LANES), axis=0)
        for r in range(MLA_HEADS * LANES // ROW_CHUNK):
            rows = slice(r * ROW_CHUNK, (r + 1) * ROW_CHUNK)
            q_ref[0, rows, :] = _rotate(_dot_nt(wq_ref[rows, :], c_q), cq, sq, 0).astype(q_ref.dtype)

    k_r_tiled = jnp.concatenate([k_r] * (ROW_CHUNK // LANES), axis=-1)
    for j in range(MLA_HEADS * LANES // ROW_CHUNK):
        cols = slice(j * ROW_CHUNK, (j + 1) * ROW_CHUNK)
        kb = (_dot(c_kv, wk_ref[:, cols]) + k_r_tiled).astype(k_ref.dtype)
        k_ref[0, :, cols] = kb
        n_sl = ROW_CHUNK // LANES
        kn_ref[0, 0, j * n_sl:(j + 1) * n_sl, :] = _slot_norm_rows(kb, n_sl)

    _store_vt(v_ref, _dot_nt(wv_ref[...], c_kv), MLA_HEADS)


def _l1_proj(x, mod, mod_row, weights, gains, q_tables, k_tables, *, tm, with_q, name):
    b, t, d = x.shape
    wdn, wq, wk, wv = weights
    mod_map = (lambda bi, i: (bi, 0, 0)) if mod_row is None else (lambda bi, i: (mod_row, 0, 0))

    def full(a):
        return pl.BlockSpec(a.shape, lambda bi, i: (0,) * a.ndim)

    in_specs = [pl.BlockSpec((1, tm, d), lambda bi, i: (bi, i, 0)), pl.BlockSpec((1, N_MOD, d), mod_map),
                full(wdn), full(gains[0]), full(gains[1])]
    args = [x, mod, wdn, gains[0], gains[1]]
    if with_q:
        in_specs += [full(wq), pl.BlockSpec((LANES, tm), lambda bi, i: (0, i)), pl.BlockSpec((LANES, tm), lambda bi, i: (0, i))]
        args += [wq, q_tables[0], q_tables[1]]
    in_specs += [full(wk), pl.BlockSpec((tm, LANES), lambda bi, i: (i, 0)), pl.BlockSpec((tm, LANES), lambda bi, i: (i, 0)), full(wv)]
    args += [wk, k_tables[0], k_tables[1], wv]
    out_shape, out_specs = [], []
    if with_q:
        out_shape.append(jax.ShapeDtypeStruct((b, MLA_HEADS * LANES, t), BF16))
        out_specs.append(pl.BlockSpec((1, MLA_HEADS * LANES, tm), lambda bi, i: (bi, 0, i)))
    out_shape += [jax.ShapeDtypeStruct((b, t, MLA_HEADS * LANES), BF16),
                  jax.ShapeDtypeStruct((b, t // tm, MLA_HEADS, LANES), F32),
                  jax.ShapeDtypeStruct((b, MLA_HEADS * V_ROWS, t), BF16)]
    out_specs += [pl.BlockSpec((1, tm, MLA_HEADS * LANES), lambda bi, i: (bi, i, 0)),
                  pl.BlockSpec((1, 1, MLA_HEADS, LANES), lambda bi, i: (bi, i, 0, 0)),
                  pl.BlockSpec((1, MLA_HEADS * V_ROWS, tm), lambda bi, i: (bi, 0, i))]
    return pl.pallas_call(
        functools.partial(_l1_proj_body, with_q=with_q), grid=(b, t // tm), in_specs=in_specs,
        out_specs=out_specs, out_shape=out_shape, name=name,
        compiler_params=_cparams(("parallel", "parallel")),
    )(*args)


def _dense_body(*refs, heads, k_slot, v_row, nk, has_ctx):
    if has_ctx:
        qt_ref, k_ref, vt_ref, kc_ref, vct_ref, out_ref, m_ref, l_ref, acc_ref = refs
    else:
        qt_ref, k_ref, vt_ref, out_ref, m_ref, l_ref, acc_ref = refs
    ki = pl.program_id(2)
    last = nk if has_ctx else nk - 1

    @pl.when(ki == 0)
    def _():
        m_ref[...] = jnp.full(m_ref.shape, MASK_VALUE, F32)
        l_ref[...] = jnp.zeros(l_ref.shape, F32)
        acc_ref[...] = jnp.zeros(acc_ref.shape, F32)

    def step(kr, vr):
        def scores(h):
            ks = k_slot[h]
            return _dot(kr[0, :, ks * LANES:(ks + 1) * LANES], qt_ref[0, h * LANES:(h + 1) * LANES, :])

        pending = [scores(h) for h in range(min(AHEAD, heads))]
        for h in range(heads):
            s = pending.pop(0)
            if h + AHEAD < heads:
                pending.append(scores(h + AHEAD))
            rows = slice(h * HEAD_DIM, (h + 1) * HEAD_DIM)
            m_prev = m_ref[h:h + 1, :]
            m_new = jnp.maximum(m_prev, jnp.max(s, axis=0, keepdims=True))
            alpha = jnp.exp2(m_prev - m_new)
            p = jnp.exp2(s - m_new).astype(BF16)
            pv = _dot(vr[0, v_row[h]:v_row[h] + V_ROWS, :], p)
            m_ref[h:h + 1, :] = m_new
            l_ref[h:h + 1, :] = alpha * l_ref[h:h + 1, :] + pv[HEAD_DIM:HEAD_DIM + 1]
            acc_ref[rows, :] = alpha * acc_ref[rows, :] + pv[:HEAD_DIM]

    if has_ctx:
        @pl.when(ki < nk)
        def _():
            step(k_ref, vt_ref)

        @pl.when(ki == nk)
        def _():
            step(kc_ref, vct_ref)
    else:
        step(k_ref, vt_ref)

    @pl.when(ki == last)
    def _():
        for h in range(heads):
            rows = slice(h * HEAD_DIM, (h + 1) * HEAD_DIM)
            acc_ref[rows, :] = acc_ref[rows, :] * (1.0 / l_ref[h:h + 1, :])
        out_ref[0] = acc_ref[...].T.astype(out_ref.dtype)


def _dense_attn(qt, k, vt, kc=None, vct=None, *, heads, k_slot, v_row, tq, tk, name):
    b, nq, lq = qt.shape
    _, lk, nkw = k.shape
    nv = vt.shape[1]
    assert nq == heads * LANES and lq % tq == 0 and lk % tk == 0
    nk = lk // tk
    has_ctx = kc is not None
    steps = nk + (1 if has_ctx else 0)
    in_specs = [
        pl.BlockSpec((1, nq, tq), lambda bi, qi, ki: (bi, 0, qi)),
        pl.BlockSpec((1, tk, nkw), lambda bi, qi, ki: (bi, jnp.minimum(ki, nk - 1), 0)),
        pl.BlockSpec((1, nv, tk), lambda bi, qi, ki: (bi, 0, jnp.minimum(ki, nk - 1))),
    ]
    args = [qt, k, vt]
    if has_ctx:
        lc = kc.shape[1]
        in_specs += [pl.BlockSpec((1, lc, nkw), lambda bi, qi, ki: (bi, 0, 0)),
                     pl.BlockSpec((1, nv, lc), lambda bi, qi, ki: (bi, 0, 0))]
        args += [kc, vct]
    body = functools.partial(_dense_body, heads=heads, k_slot=tuple(k_slot), v_row=tuple(v_row),
                             nk=nk, has_ctx=has_ctx)
    return pl.pallas_call(
        body, grid=(b, lq // tq, steps), in_specs=in_specs,
        out_specs=pl.BlockSpec((1, tq, heads * HEAD_DIM), lambda bi, qi, ki: (bi, qi, 0)),
        out_shape=jax.ShapeDtypeStruct((b, lq, heads * HEAD_DIM), BF16),
        scratch_shapes=[pltpu.VMEM((heads, tq), F32), pltpu.VMEM((heads, tq), F32),
                        pltpu.VMEM((heads * HEAD_DIM, tq), F32)],
        name=name, compiler_params=_cparams(("parallel", "parallel", "arbitrary")),
    )(*args)


def _dense_fast_body(qt_ref, k_ref, vt_ref, kc_ref, vct_ref, knl_ref, knc_ref, out_ref, lmin_ref,
                     b_ref, acc_ref, o_ref, *, heads, k_slot, v_row, nk):
    ki = pl.program_id(2)
    tq = qt_ref.shape[2]

    @pl.when(ki == 0)
    def _():
        kn = jnp.sqrt(jnp.maximum(jnp.max(knl_ref[0], axis=0), knc_ref[0, 0])) * BOUND_SLACK
        for h in range(heads):
            qf = qt_ref[0, h * LANES:(h + 1) * LANES, :].astype(F32)
            qn = jnp.sqrt(jnp.sum(qf * qf, axis=0, keepdims=True))
            krow = kn[k_slot[h]:k_slot[h] + 1, :]
            b_ref[h:h + 1, :] = qn * jnp.concatenate([krow] * (tq // LANES), axis=1)
        acc_ref[...] = jnp.zeros(acc_ref.shape, F32)

    def step(kr, vr):
        def scores(h):
            ks = k_slot[h]
            return _dot(kr[0, :, ks * LANES:(ks + 1) * LANES], qt_ref[0, h * LANES:(h + 1) * LANES, :])

        pending = [scores(h) for h in range(min(AHEAD, heads))]
        for h in range(heads):
            s = pending.pop(0)
            if h + AHEAD < heads:
                pending.append(scores(h + AHEAD))
            p = jnp.exp2(s - b_ref[h:h + 1, :]).astype(BF16)
            rows = slice(h * V_ROWS, (h + 1) * V_ROWS)
            acc_ref[rows, :] += _dot(vr[0, v_row[h]:v_row[h] + V_ROWS, :], p)

    @pl.when(ki < nk)
    def _():
        step(k_ref, vt_ref)

    @pl.when(ki == nk)
    def _():
        step(kc_ref, vct_ref)
        lmin = None
        for h in range(heads):
            l = acc_ref[h * V_ROWS + HEAD_DIM:h * V_ROWS + HEAD_DIM + 1, :]
            lmin = l if lmin is None else jnp.minimum(lmin, l)
            o_ref[h * HEAD_DIM:(h + 1) * HEAD_DIM, :] = acc_ref[h * V_ROWS:h * V_ROWS + HEAD_DIM, :] * (1.0 / l)
        out_ref[0] = o_ref[...].T.astype(out_ref.dtype)
        lmin_ref[0, 0] = jnp.broadcast_to(jnp.min(lmin, axis=1, keepdims=True), lmin_ref.shape[2:])


def _dense_attn_fast(qt, k, vt, kc, vct, knl, knc, *, heads, k_slot, v_row, tq, tk, name):
    b, nq, lq = qt.shape
    _, lk, nkw = k.shape
    nv = vt.shape[1]
    lc = kc.shape[1]
    nk = lk // tk
    assert nq == heads * LANES and lq % tq == 0 and lk % tk == 0
    n_tiles, n_slots = knl.shape[1:3]
    in_specs = [
        pl.BlockSpec((1, nq, tq), lambda bi, qi, ki: (bi, 0, qi)),
        pl.BlockSpec((1, tk, nkw), lambda bi, qi, ki: (bi, jnp.minimum(ki, nk - 1), 0)),
        pl.BlockSpec((1, nv, tk), lambda bi, qi, ki: (bi, 0, jnp.minimum(ki, nk - 1))),
        pl.BlockSpec((1, lc, nkw), lambda bi, qi, ki: (bi, 0, 0)),
        pl.BlockSpec((1, nv, lc), lambda bi, qi, ki: (bi, 0, 0)),
        pl.BlockSpec((1, n_tiles, n_slots, LANES), lambda bi, qi, ki: (bi, 0, 0, 0)),
        pl.BlockSpec((1, 1, n_slots, LANES), lambda bi, qi, ki: (bi, 0, 0, 0)),
    ]
    body = functools.partial(_dense_fast_body, heads=heads, k_slot=tuple(k_slot), v_row=tuple(v_row), nk=nk)
    return pl.pallas_call(
        body, grid=(b, lq // tq, nk + 1), in_specs=in_specs,
        out_specs=[pl.BlockSpec((1, tq, heads * HEAD_DIM), lambda bi, qi, ki: (bi, qi, 0)),
                   pl.BlockSpec((1, 1, SUBLANES, LANES), lambda bi, qi, ki: (bi, qi, 0, 0))],
        out_shape=[jax.ShapeDtypeStruct((b, lq, heads * HEAD_DIM), BF16),
                   jax.ShapeDtypeStruct((b, lq // tq, SUBLANES, LANES), F32)],
        scratch_shapes=[pltpu.VMEM((heads, tq), F32), pltpu.VMEM((heads * V_ROWS, tq), F32),
                        pltpu.VMEM((heads * HEAD_DIM, tq), F32)],
        name=name, compiler_params=_cparams(("parallel", "parallel", "arbitrary")),
    )(qt, k, vt, kc, vct, knl, knc)


def _attend(qt, k, vt, kc, vct, knl, knc, *, name, **kw):
    y, lmin = _dense_attn_fast(qt, k, vt, kc, vct, knl, knc, name=name + "_fast", **kw)
    return lax.cond(jnp.min(lmin) > L_MIN_OK, lambda: y,
                    lambda: _dense_attn(qt, k, vt, kc, vct, name=name, **kw))


NA_QROWS = 4
NA_TQ = NA_QROWS * GRID_W
NA_WROWS = 12
NA_WBLOCKS = NA_WROWS // NA_QROWS
NA_STEPS = GRID_H // NA_QROWS


def _natten_body(qt_ref, k0_ref, k1_ref, k2_ref, v0_ref, v1_ref, v2_ref, kc_ref, vct_ref, bias_ref,
                 out_ref, o_ref):
    k_refs = (k0_ref, k1_ref, k2_ref)
    v_refs = (v0_ref, v1_ref, v2_ref)

    def scores(h):
        lanes = slice((h // 2) * LANES, (h // 2 + 1) * LANES)
        qq = qt_ref[0, h * LANES:(h + 1) * LANES, :]
        return [_dot(kc_ref[0, :, lanes], qq)] + [_dot(k_refs[blk][0, :, lanes], qq) for blk in range(NA_WBLOCKS)]

    pending = [scores(h) for h in range(AHEAD)]
    for h in range(B_HEADS):
        s_all = pending.pop(0)
        if h + AHEAD < B_HEADS:
            pending.append(scores(h + AHEAD))
        vrows = slice(h * V_ROWS, (h + 1) * V_ROWS)
        s_all = [s_all[0]] + [s_all[1 + blk] + bias_ref[0, h, blk * NA_TQ:(blk + 1) * NA_TQ, :]
                              for blk in range(NA_WBLOCKS)]
        m = s_all[0].max(axis=0, keepdims=True)
        for s in s_all[1:]:
            m = jnp.maximum(m, s.max(axis=0, keepdims=True))
        pv = _dot(vct_ref[0, vrows, :], jnp.exp2(s_all[0] - m).astype(BF16))
        for blk in range(NA_WBLOCKS):
            pv = pv + _dot(v_refs[blk][0, vrows, :], jnp.exp2(s_all[1 + blk] - m).astype(BF16))
        o_ref[h * HEAD_DIM:(h + 1) * HEAD_DIM, :] = pv[:HEAD_DIM] * (1.0 / pv[HEAD_DIM:HEAD_DIM + 1])
    out_ref[0] = o_ref[...].T.astype(out_ref.dtype)


def _natten_fast_body(qt_ref, k0_ref, k1_ref, k2_ref, v0_ref, v1_ref, v2_ref, kc_ref, vct_ref, bias_ref,
                      knl_ref, knc_ref, bmax_ref, out_ref, lmin_ref, o_ref, kall_ref, vall_ref):
    kall_ref[0:CTX_LEN, :] = kc_ref[0]
    vall_ref[:, 0:CTX_LEN] = vct_ref[0]
    for blk, (kr, vr) in enumerate(((k0_ref, v0_ref), (k1_ref, v1_ref), (k2_ref, v2_ref))):
        kall_ref[CTX_LEN + blk * NA_TQ:CTX_LEN + (blk + 1) * NA_TQ, :] = kr[0]
        vall_ref[:, CTX_LEN + blk * NA_TQ:CTX_LEN + (blk + 1) * NA_TQ] = vr[0]
    kn = jnp.sqrt(jnp.maximum(jnp.max(knl_ref[0], axis=0), knc_ref[0, 0])) * BOUND_SLACK

    def scores(h):
        lanes = slice((h // 2) * LANES, (h // 2 + 1) * LANES)
        return _dot(kall_ref[:, lanes], qt_ref[0, h * LANES:(h + 1) * LANES, :])

    pending = [scores(h) for h in range(AHEAD)]
    lmin = None
    for h in range(B_HEADS):
        s = pending.pop(0)
        if h + AHEAD < B_HEADS:
            pending.append(scores(h + AHEAD))
        qf = qt_ref[0, h * LANES:(h + 1) * LANES, :].astype(F32)
        qn = jnp.sqrt(jnp.sum(qf * qf, axis=0, keepdims=True))
        krow = kn[h // 2:h // 2 + 1, :]
        shift = qn * jnp.concatenate([krow] * (NA_TQ // LANES), axis=1) + bmax_ref[h:h + 1, :]
        p = jnp.concatenate([jnp.exp2(s[:CTX_LEN] - shift).astype(BF16),
                             jnp.exp2(s[CTX_LEN:] + (bias_ref[0, h] - shift)).astype(BF16)], axis=0)
        pv = _dot(vall_ref[h * V_ROWS:(h + 1) * V_ROWS, :], p)
        l = pv[HEAD_DIM:HEAD_DIM + 1]
        lmin = l if lmin is None else jnp.minimum(lmin, l)
        o_ref[h * HEAD_DIM:(h + 1) * HEAD_DIM, :] = pv[:HEAD_DIM] * (1.0 / l)
    out_ref[0] = o_ref[...].T.astype(out_ref.dtype)
    lmin_ref[0, 0] = jnp.broadcast_to(jnp.min(lmin, axis=1, keepdims=True), lmin_ref.shape[2:])


def _natten_call(body, name, qt, k, vt, kc, vct, bias, extra=(), with_lmin=False, gather=False):
    b = qt.shape[0]
    nw = B_HEADS * HEAD_DIM
    nv = B_HEADS * V_ROWS

    def wstart(i):
        return jnp.clip(i - 1, 0, NA_STEPS - NA_WBLOCKS)

    def cls(i):
        return jnp.where(i == 0, 0, jnp.where(i == NA_STEPS - 1, 2, 1))

    in_specs = [pl.BlockSpec((1, B_HEADS * LANES, NA_TQ), lambda bi, i: (bi, 0, i))]
    in_specs += [pl.BlockSpec((1, NA_TQ, nw), functools.partial(lambda bi, i, j: (bi, wstart(i) + j, 0), j=j))
                 for j in range(NA_WBLOCKS)]
    in_specs += [pl.BlockSpec((1, nv, NA_TQ), functools.partial(lambda bi, i, j: (bi, 0, wstart(i) + j), j=j))
                 for j in range(NA_WBLOCKS)]
    in_specs += [pl.BlockSpec((1, CTX_LEN, nw), lambda bi, i: (bi, 0, 0)),
                 pl.BlockSpec((1, nv, CTX_LEN), lambda bi, i: (bi, 0, 0)),
                 pl.BlockSpec((1, B_HEADS, NA_WROWS * GRID_W, NA_TQ), lambda bi, i: (cls(i), 0, 0, 0))]
    for a, per_batch in extra:
        blk = (1,) + a.shape[1:] if per_batch else a.shape
        nz = len(blk) - 1
        in_specs.append(pl.BlockSpec(blk, (lambda bi, i, nz=nz: (bi,) + (0,) * nz) if per_batch
                                     else (lambda bi, i, nz=nz: (0,) * (nz + 1))))
    out_specs = [pl.BlockSpec((1, NA_TQ, nw), lambda bi, i: (bi, i, 0))]
    out_shape = [jax.ShapeDtypeStruct((b, SEQ, nw), BF16)]
    if with_lmin:
        out_specs.append(pl.BlockSpec((1, 1, SUBLANES, LANES), lambda bi, i: (bi, i, 0, 0)))
        out_shape.append(jax.ShapeDtypeStruct((b, NA_STEPS, SUBLANES, LANES), F32))
    return pl.pallas_call(
        body, grid=(b, NA_STEPS), in_specs=in_specs, out_specs=out_specs, out_shape=out_shape,
        scratch_shapes=[pltpu.VMEM((nw, NA_TQ), F32)] + ([pltpu.VMEM((CTX_LEN + NA_WROWS * GRID_W, nw), BF16),
                                                          pltpu.VMEM((nv, CTX_LEN + NA_WROWS * GRID_W), BF16)] if gather else []),
        name=name, compiler_params=_cparams(("parallel", "arbitrary")),
    )(qt, k, k, k, vt, vt, vt, kc, vct, bias, *[a for a, _ in extra])


def _natten(qt, k, vt, kc, vct, rpb, knl, knc):
    bias = _natten_bias(rpb)
    bmax = jnp.maximum(jnp.max(rpb.astype(F32), axis=(1, 2)), 0.0) * LOG2E
    bmax = jnp.broadcast_to(bmax[:, None], (B_HEADS, NA_TQ))
    y, lmin = _natten_call(_natten_fast_body, "natten_fast", qt, k, vt, kc, vct, bias,
                           extra=((knl, True), (knc, True), (bmax, False)), with_lmin=True, gather=True)
    return lax.cond(jnp.min(lmin) > L_MIN_OK, lambda: y,
                    lambda: _natten_call(_natten_body, "natten", qt, k, vt, kc, vct, bias)[0])


def _natten_bias(rpb):
    v = rpb.astype(F32) * LOG2E
    h, nr, _ = v.shape
    period = 2 * GRID_W
    row = jnp.zeros((h, nr, period), F32)
    row = row.at[..., :NA_COLS].set(v[..., NA_COLS - 1::-1])
    row = row.at[..., period - (NA_COLS - 1):].set(v[..., :NA_COLS - 1:-1])
    toep = jnp.tile(row, (1, 1, GRID_W))[..., :GRID_W * (period - 1)]
    toep = toep.reshape(h, nr, GRID_W, period - 1)[..., :GRID_W]
    col = np.arange(GRID_W)
    cs = np.clip(col - NA_COLS // 2, 0, GRID_W - NA_COLS)
    col_ok = (col[:, None] >= cs[None, :]) & (col[:, None] < cs[None, :] + NA_COLS)
    masked = 2.0 * MASK_VALUE
    toep = jnp.where(jnp.asarray(col_ok), toep, masked)
    dead = jnp.full((h, GRID_W, GRID_W), masked, F32)
    tables = []
    for r0 in (0, 2 * NA_QROWS, GRID_H - NA_QROWS):
        ws = min(max(r0 - NA_QROWS, 0), GRID_H - NA_WROWS)
        key_rows = []
        for ki in range(NA_WROWS):
            blocks = []
            for qi in range(NA_QROWS):
                k_row, q_row = ws + ki, r0 + qi
                rs = min(max(q_row - NA_ROWS // 2, 0), GRID_H - NA_ROWS)
                blocks.append(toep[:, k_row - q_row + NA_ROWS - 1] if rs <= k_row < rs + NA_ROWS else dead)
            key_rows.append(jnp.concatenate(blocks, axis=-1))
        tables.append(jnp.concatenate(key_rows, axis=-2))
    return jnp.stack(tables)


def _ffn_body(x_ref, xp_ref, xn_ref, mod_ref, wg_ref, wv_ref, wd_ref, cw_ref, cb_ref, lng_ref, lnb_ref,
              out_ref, hs_ref, g_ref, *, tm, rc, n_tiles):
    i = pl.program_id(1)
    sc = mod_ref[0, SC2:SC2 + 1, :]
    sh = mod_ref[0, SH2:SH2 + 1, :]
    gate_mod = mod_ref[0, G2:G2 + 1, :]

    def modulate(v):
        return v * (1.0 + sc) + sh

    hp = jnp.where(i > 0, modulate(xp_ref[0]), 0.0)
    hn = jnp.where(i < n_tiles - 1, modulate(xn_ref[0]), 0.0)
    hs_ref[0:FFN_HALO, :] = hp.astype(BF16)
    hs_ref[FFN_HALO:FFN_HALO + tm, :] = modulate(x_ref[0]).astype(BF16)
    hs_ref[FFN_HALO + tm:, :] = hn.astype(BF16)

    def up(c):
        lo = c * rc
        g_ref[c] = _dot(hs_ref[lo:lo + rc + 2 * FFN_HALO, :], wg_ref[...])
        return _dot(hs_ref[FFN_HALO + lo:FFN_HALO + lo + rc, :], wv_ref[...])

    n_chunks = tm // rc
    val_next = up(0)
    for c in range(n_chunks):
        val = val_next
        if c + 1 < n_chunks:
            val_next = up(c + 1)
        gate = (cw_ref[0:1, :] * g_ref[c, FFN_HALO - 1:FFN_HALO - 1 + rc, :]
                + cw_ref[1:2, :] * g_ref[c, FFN_HALO:FFN_HALO + rc, :]
                + cw_ref[2:3, :] * g_ref[c, FFN_HALO + 1:FFN_HALO + 1 + rc, :]
                + cb_ref[...])
        act = gate * jax.nn.sigmoid(gate) * val
        y = _dot(act.astype(BF16), wd_ref[...])
        rows = slice(c * rc, (c + 1) * rc)
        out_ref[0, rows, :] = _layer_norm(DEEPNORM_ALPHA * x_ref[0, rows, :] + gate_mod * y, lng_ref[...], lnb_ref[...])


def _ffn(x, mod, mod_row, wup, wd, conv_w, conv_b, ln_g, ln_b, *, tm, name):
    b, t, d = x.shape
    n_tiles = t // tm
    rc = min(FFN_ROWS, tm)
    hb = tm // FFN_HALO
    n_hblocks = t // FFN_HALO
    mod_map = (lambda bi, i: (bi, 0, 0)) if mod_row is None else (lambda bi, i: (mod_row, 0, 0))

    def resident(shape, col=0):
        return pl.BlockSpec(shape, lambda bi, i: (0, col), pipeline_mode=pl.Buffered(1))

    in_specs = [
        pl.BlockSpec((1, tm, d), lambda bi, i: (bi, i, 0)),
        pl.BlockSpec((1, FFN_HALO, d), lambda bi, i: (bi, jnp.maximum(i * hb - 1, 0), 0)),
        pl.BlockSpec((1, FFN_HALO, d), lambda bi, i: (bi, jnp.minimum((i + 1) * hb, n_hblocks - 1), 0)),
        pl.BlockSpec((1, N_MOD, d), mod_map),
        resident((d, D_FF), 0), resident((d, D_FF), 1), resident((D_FF, d)),
        resident((3, D_FF)), resident((1, D_FF)), resident((1, d)), resident((1, d)),
    ]
    body = functools.partial(_ffn_body, tm=tm, rc=rc, n_tiles=n_tiles)
    return pl.pallas_call(
        body, grid=(b, n_tiles), in_specs=in_specs,
        out_specs=pl.BlockSpec((1, tm, d), lambda bi, i: (bi, i, 0)),
        out_shape=jax.ShapeDtypeStruct((b, t, d), F32),
        scratch_shapes=[pltpu.VMEM((tm + 2 * FFN_HALO, d), BF16),
                        pltpu.VMEM((tm // rc, rc + 2 * FFN_HALO, D_FF), F32)],
        name=name, compiler_params=_cparams(("parallel", "parallel")),
    )(x, x, x, mod, wup, wup, wd, conv_w.astype(F32), conv_b.reshape(1, D_FF).astype(F32),
      ln_g.reshape(1, d).astype(F32), ln_b.reshape(1, d).astype(F32))


def _rope_cos_sin(rot_dim):
    pos = jnp.arange(SEQ, dtype=jnp.int32)
    rows = (pos // GRID_W).astype(F32)
    cols = (pos % GRID_W).astype(F32)
    axis_dim = rot_dim // 2
    inv = ROPE_THETA ** (-jnp.arange(0, axis_dim, 2, dtype=F32) / axis_dim)
    ang = jnp.concatenate([rows[:, None] * inv, cols[:, None] * inv], axis=-1)
    cos = jnp.repeat(jnp.cos(ang), 2, axis=-1)
    sin = jnp.repeat(jnp.sin(ang), 2, axis=-1) * jnp.tile(jnp.array([-1.0, 1.0], F32), rot_dim // 2)
    return cos, sin


def _swap_pairs(g):
    return g.reshape(-1, 2)[:, ::-1].reshape(-1)


def _gqa_tables(gain, scale, length, rope):
    if rope:
        cos, sin = _rope_cos_sin(HEAD_DIM)
    else:
        cos, sin = jnp.ones((length, HEAD_DIM), F32), jnp.zeros((length, HEAD_DIM), F32)
    c = cos * (gain * scale)
    s = sin * (_swap_pairs(gain) * scale)
    return jnp.concatenate([c, c], axis=-1), jnp.concatenate([s, s], axis=-1)


def _mla_tables(scale, length, rope):
    c = jnp.full((length, LANES), scale, F32)
    s = jnp.zeros((length, LANES), F32)
    if rope:
        cos, sin = _rope_cos_sin(MLA_ROPE_DIM)
        lo, hi = MLA_NOPE_DIM, MLA_NOPE_DIM + MLA_ROPE_DIM
        c = c.at[:, lo:hi].set(cos * scale)
        s = s.at[:, lo:hi].set(sin * scale)
    return c, s


def _slot_cols(w, heads, width):
    k = w.shape[0]
    return jnp.pad(w.reshape(k, heads, width), ((0, 0), (0, 0), (0, LANES - width))).reshape(k, heads * LANES)


def _layer0_mixer(x, xc, mod, w_in, q_gain, k_gain, rpb):
    w = w_in.astype(BF16)
    aq, akv, bw = A_HEADS * HEAD_DIM, A_KV_HEADS * HEAD_DIM, B_HEADS * HEAD_DIM
    o = 0
    w_qa = w[:, o:o + aq]; o += aq
    w_ka = w[:, o:o + akv]; o += akv
    w_va = w[:, o:o + akv]; o += akv
    w_qb = w[:, o:o + bw]; o += bw
    w_kb = w[:, o:o + bw]; o += bw
    w_vb = w[:, o:o + bw]
    wqa = jnp.concatenate([w_qa, w_va], axis=1).T
    wqb = jnp.concatenate([w_qb, w_vb], axis=1).T
    wka = jnp.concatenate([w_ka, w_kb], axis=1)
    weights = (wqa, wka, wqb)
    qscale = HEAD_SCALE * LOG2E

    def tables(length, rope):
        cq, sq = _gqa_tables(q_gain, qscale, length, rope)
        ck, sk = _gqa_tables(k_gain, 1.0, length, rope)
        return cq[:, :HEAD_DIM].T, sq[:, :HEAD_DIM].T, ck, sk

    qa_t, ka, kna, va_t, qb_t, kb, knb, vb_t = _l0_proj(x, mod, None, weights, tables(SEQ, True), tm=TM_WIDE, name="l0_proj")
    qac_t, kac, knac, vac_t, qbc_t, kbc, knbc, vbc_t = _l0_proj(xc, mod, CTX_MOD_ROW, weights, tables(CTX_LEN, False),
                                                    tm=TMC, name="l0_proj_ctx")

    a_kslot = [0] * A_HEADS
    a_vrow = [(h // A_GROUP) * V_ROWS for h in range(A_HEADS)]
    b_kslot = [h // 2 for h in range(B_HEADS)]
    b_vrow = [h * V_ROWS for h in range(B_HEADS)]
    ya = _attend(qa_t, ka, va_t, kac, vac_t, kna, knac, heads=A_HEADS, k_slot=a_kslot, v_row=a_vrow,
                 tq=TQ, tk=TK, name="gqa")
    yb = _natten(qb_t, kb, vb_t, kbc, vbc_t, rpb, knb, knbc)
    yac = _dense_attn(qac_t, kac, vac_t, heads=A_HEADS, k_slot=a_kslot, v_row=a_vrow,
                      tq=CTX_LEN, tk=CTX_LEN, name="gqa_ctx")
    ybc = _dense_attn(qbc_t, kbc, vbc_t, heads=B_HEADS, k_slot=b_kslot, v_row=b_vrow,
                      tq=CTX_LEN, tk=CTX_LEN, name="nbr_ctx")
    return (ya, yb), (yac, ybc)


def _layer1_mixer(x, xc, mod, w_in, cq_gain, ckv_gain, w_uq, w_ukv):
    qk_dim = MLA_NOPE_DIM + MLA_ROPE_DIM
    kv_dim = MLA_NOPE_DIM + MLA_V_DIM
    n_in = MLA_Q_LORA + MLA_KV_LORA
    w_in16 = w_in.astype(BF16)
    wdn = jnp.concatenate([w_in16[:, :n_in], jnp.zeros((D_MODEL, MLA_NOPE_DIM), BF16), w_in16[:, n_in:],
                           jnp.zeros((D_MODEL, LANES - qk_dim), BF16)], axis=1)
    wq = _slot_cols(w_uq.astype(BF16), MLA_HEADS, qk_dim).T
    wkv = w_ukv.astype(BF16).reshape(MLA_KV_LORA, MLA_HEADS, kv_dim)
    wk = _slot_cols(wkv[:, :, :MLA_NOPE_DIM].reshape(MLA_KV_LORA, -1), MLA_HEADS, MLA_NOPE_DIM)
    wv = wkv[:, :, MLA_NOPE_DIM:].reshape(MLA_KV_LORA, MLA_HEADS * MLA_V_DIM).T
    weights = (wdn, wq, wk, wv)
    gains = (cq_gain.reshape(1, -1).astype(F32), ckv_gain.reshape(1, -1).astype(F32))

    cq, sq = _mla_tables(MLA_SCALE * LOG2E, SEQ, True)
    qt, kk, kn, vt = _l1_proj(x, mod, None, weights, gains, (cq.T, sq.T), _mla_tables(1.0, SEQ, True),
                          tm=TM, with_q=True, name="l1_proj")
    kkc, knc, vtc = _l1_proj(xc, mod, CTX_MOD_ROW, weights, gains, None, _mla_tables(1.0, CTX_LEN, False),
                        tm=TMC, with_q=False, name="l1_proj_ctx")
    return _attend(qt, kk, vt, kkc, vtc, kn, knc, heads=MLA_HEADS, k_slot=list(range(MLA_HEADS)),
                   v_row=[h * V_ROWS for h in range(MLA_HEADS)], tq=TQ, tk=TK, name="mla")


def _post_mixer(x, y, mod, mod_row, w_out, ln1, w_up, conv_w, conv_b, w_down, ln2, tm, tag):
    wo = w_out.astype(BF16)
    if isinstance(y, tuple):
        n0 = y[0].shape[2]
        extra = dict(x2=y[1], w2=wo[n0:])
        y, wo = y[0], wo[:n0]
    else:
        extra = {}
    x = _proj(y, wo, name="out_ln" + tag, tm=min(TM_WIDE, x.shape[1]), tn=D_MODEL, mod=mod, mod_row=mod_row,
              gate_row=G1, resid=x, ln=ln1, **extra)
    return _ffn(x, mod, mod_row, w_up.astype(BF16), w_down.astype(BF16), conv_w, conv_b, ln2[0], ln2[1],
                tm=tm, name="ffn" + tag)


def kernel(x, c, ctx, c_ctx, l0_w_ada, l0_b_ada, l0_w_in, l0_q_gain, l0_k_gain, l0_rpb, l0_w_out, l0_ln1_g, l0_ln1_b, l0_w_up, l0_conv_w, l0_conv_b, l0_w_down, l0_ln2_g, l0_ln2_b, l1_w_ada, l1_b_ada, l1_w_in, l1_cq_gain, l1_ckv_gain, l1_w_uq, l1_w_ukv, l1_w_out, l1_ln1_g, l1_ln1_b, l1_w_up, l1_conv_w, l1_conv_b, l1_w_down, l1_ln2_g, l1_ln2_b):
    xc = ctx
    mod = _modvec(c, c_ctx, l0_w_ada, l0_b_ada)
    y, yc = _layer0_mixer(x, xc, mod, l0_w_in, l0_q_gain, l0_k_gain, l0_rpb)
    post0 = (l0_w_out, (l0_ln1_g, l0_ln1_b), l0_w_up, l0_conv_w, l0_conv_b, l0_w_down, (l0_ln2_g, l0_ln2_b))
    x = _post_mixer(x, y, mod, None, *post0, tm=TM, tag="0")
    xc = _post_mixer(xc, yc, mod, CTX_MOD_ROW, *post0, tm=TMC, tag="0_ctx")
    mod = _modvec(c, c_ctx, l1_w_ada, l1_b_ada)
    y = _layer1_mixer(x, xc, mod, l1_w_in, l1_cq_gain, l1_ckv_gain, l1_w_uq, l1_w_ukv)
    post1 = (l1_w_out, (l1_ln1_g, l1_ln1_b), l1_w_up, l1_conv_w, l1_conv_b, l1_w_down, (l1_ln2_g, l1_ln2_b))
    return _post_mixer(x, y, mod, None, *post1, tm=TM, tag="1")
```

```python
import functools
import math

import numpy as np
import jax
import jax.numpy as jnp
from jax import lax
from jax.experimental import pallas as pl
from jax.experimental.pallas import tpu as pltpu

D_MODEL = 1024
BATCH = 4
SEQ = 4096
DEPTH = 2
GRID_W = 64
GRID_H = SEQ // GRID_W
CTX_LEN = 256
HEAD_DIM = 64
A_HEADS = 8
A_KV_HEADS = 2
A_GROUP = A_HEADS // A_KV_HEADS
B_HEADS = 8
NA_ROWS = 8
NA_COLS = 16
ROPE_THETA = 10000.0
MLA_HEADS = 16
MLA_Q_LORA = 768
MLA_KV_LORA = 256
MLA_NOPE_DIM = 64
MLA_ROPE_DIM = 32
MLA_V_DIM = 64
D_FF = 2816
N_MOD = 6
EPS = 1e-6
DEEPNORM_ALPHA = (2 * DEPTH) ** 0.25
HEAD_SCALE = HEAD_DIM ** -0.5
MLA_SCALE = (MLA_NOPE_DIM + MLA_ROPE_DIM) ** -0.5
LOG2E = math.log2(math.e)

LANES = 128
SUBLANES = 8
HALF = LANES // 2
BF16_ROWS = 16
V_ROWS = HEAD_DIM + BF16_ROWS
MASK_VALUE = -1e30
VMEM_LIMIT = 56 * 1024 * 1024

SH1, SC1, G1, SH2, SC2, G2 = range(N_MOD)

F32 = jnp.float32
BF16 = jnp.bfloat16

MOD_ROWS = 16
CTX_MOD_ROW = BATCH
TM = 512
TM_WIDE = 1024
TMC = CTX_LEN
TQ = 1024
TK = 512
AHEAD = 2
BOUND_SLACK = 1.0 + 1e-3
L_MIN_OK = 2.0 ** -80
FFN_ROWS = 256
FFN_HALO = BF16_ROWS
ROW_CHUNK = 256


def _cparams(sem):
    return pltpu.CompilerParams(dimension_semantics=sem, vmem_limit_bytes=VMEM_LIMIT)


def _dot(a, b):
    return jnp.dot(a, b, preferred_element_type=F32)


def _dot_nt(a, b):
    return lax.dot_general(a, b, (((1,), (1,)), ((), ())), preferred_element_type=F32)


def _pair_swap(y, axis):
    n = y.shape[axis]
    idx = lax.broadcasted_iota(jnp.int32, y.shape, axis)
    nxt = pltpu.roll(y, n - 1, axis=axis)
    prv = pltpu.roll(y, 1, axis=axis)
    return jnp.where(idx % 2 == 0, nxt, prv)


def _rotate(y, c, s, axis):
    return y * c + _pair_swap(y, axis) * s


def _layer_norm(z, g, b):
    mu = jnp.mean(z, axis=-1, keepdims=True)
    zc = z - mu
    var = jnp.mean(zc * zc, axis=-1, keepdims=True)
    return zc * lax.rsqrt(var + EPS) * g + b


def _slot_norm_rows(kb, n_slots):
    rows = []
    for sl in range(n_slots):
        kf = kb[:, sl * LANES:(sl + 1) * LANES].astype(F32)
        n2 = jnp.sum(kf * kf, axis=-1, keepdims=True)
        rows.append(jnp.broadcast_to(jnp.max(n2, axis=0, keepdims=True), (1, LANES)))
    return rows[0] if n_slots == 1 else jnp.concatenate(rows, axis=0)


def _store_vt(out_ref, yt, heads):
    ones = jnp.ones((BF16_ROWS, yt.shape[1]), out_ref.dtype)
    for h in range(heads):
        out_ref[0, h * V_ROWS:h * V_ROWS + HEAD_DIM, :] = yt[h * HEAD_DIM:(h + 1) * HEAD_DIM].astype(out_ref.dtype)
        out_ref[0, h * V_ROWS + HEAD_DIM:(h + 1) * V_ROWS, :] = ones


def _proj_body(*refs, silu, bias, resid_ln, gate_row, two):
    it = iter(refs)
    x_ref = next(it)
    w_ref = next(it)
    x2_ref = next(it) if two else None
    w2_ref = next(it) if two else None
    bias_ref = next(it) if bias else None
    mod_ref = next(it) if resid_ln else None
    xres_ref = next(it) if resid_ln else None
    lng_ref = next(it) if resid_ln else None
    lnb_ref = next(it) if resid_ln else None
    out_ref = next(it)
    x = x_ref[0]
    if silu:
        x = x * jax.nn.sigmoid(x)
    y = _dot(x.astype(BF16), w_ref[...].astype(BF16))
    if two:
        y = y + _dot(x2_ref[0], w2_ref[...])
    if bias:
        y = y + bias_ref[...]
    if resid_ln:
        g = mod_ref[0, gate_row:gate_row + 1, :]
        y = _layer_norm(DEEPNORM_ALPHA * xres_ref[0] + g * y, lng_ref[...], lnb_ref[...])
    out_ref[0] = y.astype(out_ref.dtype)


def _proj(x, w, *, name, tm, tn, x2=None, w2=None, silu=False, bias=None, mod=None, mod_row=None,
          gate_row=None, resid=None, ln=None, out_dtype=F32):
    b, t, k = x.shape
    n = w.shape[1]
    assert t % tm == 0 and n % tn == 0
    resid_ln = resid is not None
    in_specs = [pl.BlockSpec((1, tm, k), lambda bi, i, j: (bi, i, 0)),
                pl.BlockSpec((k, tn), lambda bi, i, j: (0, j))]
    args = [x, w]
    if x2 is not None:
        k2 = x2.shape[2]
        in_specs += [pl.BlockSpec((1, tm, k2), lambda bi, i, j: (bi, i, 0)),
                     pl.BlockSpec((k2, tn), lambda bi, i, j: (0, j))]
        args += [x2, w2]
    if bias is not None:
        in_specs.append(pl.BlockSpec((1, tn), lambda bi, i, j: (0, j)))
        args.append(bias.reshape(1, n).astype(F32))
    if resid_ln:
        mod_map = (lambda bi, i, j: (bi, 0, 0)) if mod_row is None else (lambda bi, i, j: (mod_row, 0, 0))
        in_specs += [pl.BlockSpec((1, N_MOD, D_MODEL), mod_map),
                     pl.BlockSpec((1, tm, n), lambda bi, i, j: (bi, i, 0)),
                     pl.BlockSpec((1, n), lambda bi, i, j: (0, 0)),
                     pl.BlockSpec((1, n), lambda bi, i, j: (0, 0))]
        args += [mod, resid, ln[0].reshape(1, n).astype(F32), ln[1].reshape(1, n).astype(F32)]
    body = functools.partial(_proj_body, silu=silu, bias=bias is not None, resid_ln=resid_ln, gate_row=gate_row,
                             two=x2 is not None)
    return pl.pallas_call(
        body, grid=(b, t // tm, n // tn), in_specs=in_specs,
        out_specs=pl.BlockSpec((1, tm, tn), lambda bi, i, j: (bi, i, j)),
        out_shape=jax.ShapeDtypeStruct((b, t, n), out_dtype), name=name,
        compiler_params=_cparams(("parallel", "parallel", "arbitrary")),
    )(*args)


def _modvec(c, c_ctx, w_ada, b_ada):
    cond = jnp.zeros((1, MOD_ROWS, D_MODEL), F32).at[0, :BATCH].set(c).at[0, BATCH].set(c_ctx)
    m = _proj(cond, w_ada, name="adaln", tm=MOD_ROWS, tn=1024, silu=True, bias=b_ada)
    return m.reshape(MOD_ROWS, N_MOD, D_MODEL)


def _l0_proj_body(x_ref, mod_ref, wqa_ref, cqa_ref, sqa_ref, wka_ref, cka_ref, ska_ref, wqb_ref,
                  qa_ref, ka_ref, kna_ref, va_ref, qb_ref, kb_ref, knb_ref, vb_ref):
    sc = mod_ref[0, SC1:SC1 + 1, :]
    sh = mod_ref[0, SH1:SH1 + 1, :]
    xs = (x_ref[0] * (1.0 + sc) + sh).astype(BF16)

    cq, sq = cqa_ref[...], sqa_ref[...]
    zeros = jnp.zeros((HALF, xs.shape[0]), qa_ref.dtype)
    yt_a = _dot_nt(wqa_ref[...], xs)
    yt = yt_a[:A_HEADS * HEAD_DIM]
    for h in range(A_HEADS):
        seg = yt[h * HEAD_DIM:(h + 1) * HEAD_DIM]
        ms = jnp.mean(seg * seg, axis=0, keepdims=True)
        q = (_rotate(seg, cq, sq, 0) * lax.rsqrt(ms + EPS)).astype(qa_ref.dtype)
        half = h // A_GROUP
        qa_ref[0, h * LANES + half * HALF:h * LANES + (half + 1) * HALF, :] = q
        qa_ref[0, h * LANES + (1 - half) * HALF:h * LANES + (2 - half) * HALF, :] = zeros

    y_k = _dot(xs, wka_ref[...])
    y = y_k[:, :LANES]
    lo = lax.broadcasted_iota(jnp.int32, y.shape, 1) < HALF
    ysq = y * y
    ms_lo = jnp.sum(jnp.where(lo, ysq, 0.0), axis=-1, keepdims=True) * (1.0 / HEAD_DIM)
    ms_hi = jnp.sum(jnp.where(lo, 0.0, ysq), axis=-1, keepdims=True) * (1.0 / HEAD_DIM)
    rn = jnp.where(lo, lax.rsqrt(ms_lo + EPS), lax.rsqrt(ms_hi + EPS))
    ka = (_rotate(y, cka_ref[...], ska_ref[...], 1) * rn).astype(ka_ref.dtype)
    ka_ref[0] = ka
    kna_ref[0, 0] = jnp.broadcast_to(_slot_norm_rows(ka, 1), kna_ref.shape[2:])

    _store_vt(va_ref, yt_a[A_HEADS * HEAD_DIM:], A_KV_HEADS)

    yt_b = _dot_nt(wqb_ref[...], xs)
    yt = yt_b[:B_HEADS * HEAD_DIM] * (HEAD_SCALE * LOG2E)
    for h in range(B_HEADS):
        half = h % 2
        qb_ref[0, h * LANES + half * HALF:h * LANES + (half + 1) * HALF, :] = (
            yt[h * HEAD_DIM:(h + 1) * HEAD_DIM].astype(qb_ref.dtype))
        qb_ref[0, h * LANES + (1 - half) * HALF:h * LANES + (2 - half) * HALF, :] = zeros
    kb = y_k[:, LANES:].astype(kb_ref.dtype)
    kb_ref[0] = kb
    n_pairs = B_HEADS // 2
    knb_ref[0, 0] = jnp.concatenate([_slot_norm_rows(kb, n_pairs), jnp.zeros((SUBLANES - n_pairs, LANES), F32)], axis=0)
    _store_vt(vb_ref, yt_b[B_HEADS * HEAD_DIM:], B_HEADS)


def _l0_proj(x, mod, mod_row, weights, tables, *, tm, name):
    b, t, d = x.shape
    wqa, wka, wqb = weights
    cqa, sqa, cka, ska = tables
    mod_map = (lambda bi, i: (bi, 0, 0)) if mod_row is None else (lambda bi, i: (mod_row, 0, 0))

    def full(a):
        return pl.BlockSpec(a.shape, lambda bi, i: (0,) * a.ndim)

    in_specs = [pl.BlockSpec((1, tm, d), lambda bi, i: (bi, i, 0)),
                pl.BlockSpec((1, N_MOD, d), mod_map),
                full(wqa), pl.BlockSpec((HEAD_DIM, tm), lambda bi, i: (0, i)), pl.BlockSpec((HEAD_DIM, tm), lambda bi, i: (0, i)),
                full(wka), pl.BlockSpec((tm, LANES), lambda bi, i: (i, 0)), pl.BlockSpec((tm, LANES), lambda bi, i: (i, 0)),
                full(wqb)]
    nb = B_HEADS * HEAD_DIM

    def nat(n):
        return jax.ShapeDtypeStruct((b, t, n), BF16), pl.BlockSpec((1, tm, n), lambda bi, i: (bi, i, 0))

    def tr(n):
        return jax.ShapeDtypeStruct((b, n, t), BF16), pl.BlockSpec((1, n, tm), lambda bi, i: (bi, 0, i))

    kn = (jax.ShapeDtypeStruct((b, t // tm, SUBLANES, LANES), F32), pl.BlockSpec((1, 1, SUBLANES, LANES), lambda bi, i: (bi, i, 0, 0)))
    outs = [tr(A_HEADS * LANES), nat(LANES), kn, tr(A_KV_HEADS * V_ROWS), tr(B_HEADS * LANES), nat(nb), kn, tr(B_HEADS * V_ROWS)]
    return pl.pallas_call(
        _l0_proj_body, grid=(b, t // tm), in_specs=in_specs,
        out_specs=[o[1] for o in outs], out_shape=[o[0] for o in outs], name=name,
        compiler_params=_cparams(("parallel", "parallel")),
    )(x, mod, wqa, cqa, sqa, wka, cka, ska, wqb)


def _l1_proj_body(*refs, with_q):
    it = iter(refs)
    x_ref, mod_ref, wdn_ref, gq_ref, gkv_ref = (next(it) for _ in range(5))
    wq_ref, cq_ref, sq_ref = (next(it) for _ in range(3)) if with_q else (None, None, None)
    wk_ref, ck_ref, sk_ref, wv_ref = (next(it) for _ in range(4))
    q_ref = next(it) if with_q else None
    k_ref, kn_ref, v_ref = next(it), next(it), next(it)

    sc = mod_ref[0, SC1:SC1 + 1, :]
    sh = mod_ref[0, SH1:SH1 + 1, :]
    xs = (x_ref[0] * (1.0 + sc) + sh).astype(BF16)

    def rms(v, g_ref):
        ms = jnp.mean(v * v, axis=-1, keepdims=True)
        return (v * lax.rsqrt(ms + EPS) * g_ref[...]).astype(BF16)

    n_q, n_kv = MLA_Q_LORA, MLA_KV_LORA
    c_kv = rms(_dot(xs, wdn_ref[:, n_q:n_q + n_kv]), gkv_ref)
    k_r = _rotate(_dot(xs, wdn_ref[:, n_q + n_kv:]), ck_ref[...], sk_ref[...], 1)

    if with_q:
        c_q = rms(_dot(xs, wdn_ref[:, :n_q]), gq_ref)
        cq = jnp.concatenate([cq_ref[...]] * (ROW_CHUNK // LANES), axis=0)
        sq = jnp.concatenate([sq_ref[...]] * (ROW_CHUNK // LANES), axis=0)
        for r in range(MLA_HEADS * LANES // ROW_CHUNK):
            rows = slice(r * ROW_CHUNK, (r + 1) * ROW_CHUNK)
            q_ref[0, rows, :] = _rotate(_dot_nt(wq_ref[rows, :], c_q), cq, sq, 0).astype(q_ref.dtype)

    k_r_tiled = jnp.concatenate([k_r] * (ROW_CHUNK // LANES), axis=-1)
    for j in range(MLA_HEADS * LANES // ROW_CHUNK):
        cols = slice(j * ROW_CHUNK, (j + 1) * ROW_CHUNK)
        kb = (_dot(c_kv, wk_ref[:, cols]) + k_r_tiled).astype(k_ref.dtype)
        k_ref[0, :, cols] = kb
        n_sl = ROW_CHUNK // LANES
        kn_ref[0, 0, j * n_sl:(j + 1) * n_sl, :] = _slot_norm_rows(kb, n_sl)

    _store_vt(v_ref, _dot_nt(wv_ref[...], c_kv), MLA_HEADS)


def _l1_proj(x, mod, mod_row, weights, gains, q_tables, k_tables, *, tm, with_q, name):
    b, t, d = x.shape
    wdn, wq, wk, wv = weights
    mod_map = (lambda bi, i: (bi, 0, 0)) if mod_row is None else (lambda bi, i: (mod_row, 0, 0))

    def full(a):
        return pl.BlockSpec(a.shape, lambda bi, i: (0,) * a.ndim)

    in_specs = [pl.BlockSpec((1, tm, d), lambda bi, i: (bi, i, 0)), pl.BlockSpec((1, N_MOD, d), mod_map),
                full(wdn), full(gains[0]), full(gains[1])]
    args = [x, mod, wdn, gains[0], gains[1]]
    if with_q:
        in_specs += [full(wq), pl.BlockSpec((LANES, tm), lambda bi, i: (0, i)), pl.BlockSpec((LANES, tm), lambda bi, i: (0, i))]
        args += [wq, q_tables[0], q_tables[1]]
    in_specs += [full(wk), pl.BlockSpec((tm, LANES), lambda bi, i: (i, 0)), pl.BlockSpec((tm, LANES), lambda bi, i: (i, 0)), full(wv)]
    args += [wk, k_tables[0], k_tables[1], wv]
    out_shape, out_specs = [], []
    if with_q:
        out_shape.append(jax.ShapeDtypeStruct((b, MLA_HEADS * LANES, t), BF16))
        out_specs.append(pl.BlockSpec((1, MLA_HEADS * LANES, tm), lambda bi, i: (bi, 0, i)))
    out_shape += [jax.ShapeDtypeStruct((b, t, MLA_HEADS * LANES), BF16),
                  jax.ShapeDtypeStruct((b, t // tm, MLA_HEADS, LANES), F32),
                  jax.ShapeDtypeStruct((b, MLA_HEADS * V_ROWS, t), BF16)]
    out_specs += [pl.BlockSpec((1, tm, MLA_HEADS * LANES), lambda bi, i: (bi, i, 0)),
                  pl.BlockSpec((1, 1, MLA_HEADS, LANES), lambda bi, i: (bi, i, 0, 0)),
                  pl.BlockSpec((1, MLA_HEADS * V_ROWS, tm), lambda bi, i: (bi, 0, i))]
    return pl.pallas_call(
        functools.partial(_l1_proj_body, with_q=with_q), grid=(b, t // tm), in_specs=in_specs,
        out_specs=out_specs, out_shape=out_shape, name=name,
        compiler_params=_cparams(("parallel", "parallel")),
    )(*args)


def _dense_body(*refs, heads, k_slot, v_row, nk, has_ctx):
    if has_ctx:
        qt_ref, k_ref, vt_ref, kc_ref, vct_ref, out_ref, m_ref, l_ref, acc_ref = refs
    else:
        qt_ref, k_ref, vt_ref, out_ref, m_ref, l_ref, acc_ref = refs
    ki = pl.program_id(2)
    last = nk if has_ctx else nk - 1

    @pl.when(ki == 0)
    def _():
        m_ref[...] = jnp.full(m_ref.shape, MASK_VALUE, F32)
        l_ref[...] = jnp.zeros(l_ref.shape, F32)
        acc_ref[...] = jnp.zeros(acc_ref.shape, F32)

    def step(kr, vr):
        def scores(h):
            ks = k_slot[h]
            return _dot(kr[0, :, ks * LANES:(ks + 1) * LANES], qt_ref[0, h * LANES:(h + 1) * LANES, :])

        pending = [scores(h) for h in range(min(AHEAD, heads))]
        for h in range(heads):
            s = pending.pop(0)
            if h + AHEAD < heads:
                pending.append(scores(h + AHEAD))
            rows = slice(h * HEAD_DIM, (h + 1) * HEAD_DIM)
            m_prev = m_ref[h:h + 1, :]
            m_new = jnp.maximum(m_prev, jnp.max(s, axis=0, keepdims=True))
            alpha = jnp.exp2(m_prev - m_new)
            p = jnp.exp2(s - m_new).astype(BF16)
            pv = _dot(vr[0, v_row[h]:v_row[h] + V_ROWS, :], p)
            m_ref[h:h + 1, :] = m_new
            l_ref[h:h + 1, :] = alpha * l_ref[h:h + 1, :] + pv[HEAD_DIM:HEAD_DIM + 1]
            acc_ref[rows, :] = alpha * acc_ref[rows, :] + pv[:HEAD_DIM]

    if has_ctx:
        @pl.when(ki < nk)
        def _():
            step(k_ref, vt_ref)

        @pl.when(ki == nk)
        def _():
            step(kc_ref, vct_ref)
    else:
        step(k_ref, vt_ref)

    @pl.when(ki == last)
    def _():
        for h in range(heads):
            rows = slice(h * HEAD_DIM, (h + 1) * HEAD_DIM)
            acc_ref[rows, :] = acc_ref[rows, :] * (1.0 / l_ref[h:h + 1, :])
        out_ref[0] = acc_ref[...].T.astype(out_ref.dtype)


def _dense_attn(qt, k, vt, kc=None, vct=None, *, heads, k_slot, v_row, tq, tk, name):
    b, nq, lq = qt.shape
    _, lk, nkw = k.shape
    nv = vt.shape[1]
    assert nq == heads * LANES and lq % tq == 0 and lk % tk == 0
    nk = lk // tk
    has_ctx = kc is not None
    steps = nk + (1 if has_ctx else 0)
    in_specs = [
        pl.BlockSpec((1, nq, tq), lambda bi, qi, ki: (bi, 0, qi)),
        pl.BlockSpec((1, tk, nkw), lambda bi, qi, ki: (bi, jnp.minimum(ki, nk - 1), 0)),
        pl.BlockSpec((1, nv, tk), lambda bi, qi, ki: (bi, 0, jnp.minimum(ki, nk - 1))),
    ]
    args = [qt, k, vt]
    if has_ctx:
        lc = kc.shape[1]
        in_specs += [pl.BlockSpec((1, lc, nkw), lambda bi, qi, ki: (bi, 0, 0)),
                     pl.BlockSpec((1, nv, lc), lambda bi, qi, ki: (bi, 0, 0))]
        args += [kc, vct]
    body = functools.partial(_dense_body, heads=heads, k_slot=tuple(k_slot), v_row=tuple(v_row),
                             nk=nk, has_ctx=has_ctx)
    return pl.pallas_call(
        body, grid=(b, lq // tq, steps), in_specs=in_specs,
        out_specs=pl.BlockSpec((1, tq, heads * HEAD_DIM), lambda bi, qi, ki: (bi, qi, 0)),
        out_shape=jax.ShapeDtypeStruct((b, lq, heads * HEAD_DIM), BF16),
        scratch_shapes=[pltpu.VMEM((heads, tq), F32), pltpu.VMEM((heads, tq), F32),
                        pltpu.VMEM((heads * HEAD_DIM, tq), F32)],
        name=name, compiler_params=_cparams(("parallel", "parallel", "arbitrary")),
    )(*args)


def _dense_fast_body(qt_ref, k_ref, vt_ref, kc_ref, vct_ref, knl_ref, knc_ref, out_ref, lmin_ref,
                     b_ref, acc_ref, o_ref, *, heads, k_slot, v_row, nk):
    ki = pl.program_id(2)
    tq = qt_ref.shape[2]

    @pl.when(ki == 0)
    def _():
        kn = jnp.sqrt(jnp.maximum(jnp.max(knl_ref[0], axis=0), knc_ref[0, 0])) * BOUND_SLACK
        for h in range(heads):
            qf = qt_ref[0, h * LANES:(h + 1) * LANES, :].astype(F32)
            qn = jnp.sqrt(jnp.sum(qf * qf, axis=0, keepdims=True))
            krow = kn[k_slot[h]:k_slot[h] + 1, :]
            b_ref[h:h + 1, :] = qn * jnp.concatenate([krow] * (tq // LANES), axis=1)
        acc_ref[...] = jnp.zeros(acc_ref.shape, F32)

    def step(kr, vr):
        def scores(h):
            ks = k_slot[h]
            return _dot(kr[0, :, ks * LANES:(ks + 1) * LANES], qt_ref[0, h * LANES:(h + 1) * LANES, :])

        pending = [scores(h) for h in range(min(AHEAD, heads))]
        for h in range(heads):
            s = pending.pop(0)
            if h + AHEAD < heads:
                pending.append(scores(h + AHEAD))
            p = jnp.exp2(s - b_ref[h:h + 1, :]).astype(BF16)
            rows = slice(h * V_ROWS, (h + 1) * V_ROWS)
            acc_ref[rows, :] += _dot(vr[0, v_row[h]:v_row[h] + V_ROWS, :], p)

    @pl.when(ki < nk)
    def _():
        step(k_ref, vt_ref)

    @pl.when(ki == nk)
    def _():
        step(kc_ref, vct_ref)
        lmin = None
        for h in range(heads):
            l = acc_ref[h * V_ROWS + HEAD_DIM:h * V_ROWS + HEAD_DIM + 1, :]
            lmin = l if lmin is None else jnp.minimum(lmin, l)
            o_ref[h * HEAD_DIM:(h + 1) * HEAD_DIM, :] = acc_ref[h * V_ROWS:h * V_ROWS + HEAD_DIM, :] * (1.0 / l)
        out_ref[0] = o_ref[...].T.astype(out_ref.dtype)
        lmin_ref[0, 0] = jnp.broadcast_to(jnp.min(lmin, axis=1, keepdims=True), lmin_ref.shape[2:])


def _dense_attn_fast(qt, k, vt, kc, vct, knl, knc, *, heads, k_slot, v_row, tq, tk, name):
    b, nq, lq = qt.shape
    _, lk, nkw = k.shape
    nv = vt.shape[1]
    lc = kc.shape[1]
    nk = lk // tk
    assert nq == heads * LANES and lq % tq == 0 and lk % tk == 0
    n_tiles, n_slots = knl.shape[1:3]
    in_specs = [
        pl.BlockSpec((1, nq, tq), lambda bi, qi, ki: (bi, 0, qi)),
        pl.BlockSpec((1, tk, nkw), lambda bi, qi, ki: (bi, jnp.minimum(ki, nk - 1), 0)),
        pl.BlockSpec((1, nv, tk), lambda bi, qi, ki: (bi, 0, jnp.minimum(ki, nk - 1))),
        pl.BlockSpec((1, lc, nkw), lambda bi, qi, ki: (bi, 0, 0)),
        pl.BlockSpec((1, nv, lc), lambda bi, qi, ki: (bi, 0, 0)),
        pl.BlockSpec((1, n_tiles, n_slots, LANES), lambda bi, qi, ki: (bi, 0, 0, 0)),
        pl.BlockSpec((1, 1, n_slots, LANES), lambda bi, qi, ki: (bi, 0, 0, 0)),
    ]
    body = functools.partial(_dense_fast_body, heads=heads, k_slot=tuple(k_slot), v_row=tuple(v_row), nk=nk)
    return pl.pallas_call(
        body, grid=(b, lq // tq, nk + 1), in_specs=in_specs,
        out_specs=[pl.BlockSpec((1, tq, heads * HEAD_DIM), lambda bi, qi, ki: (bi, qi, 0)),
                   pl.BlockSpec((1, 1, SUBLANES, LANES), lambda bi, qi, ki: (bi, qi, 0, 0))],
        out_shape=[jax.ShapeDtypeStruct((b, lq, heads * HEAD_DIM), BF16),
                   jax.ShapeDtypeStruct((b, lq // tq, SUBLANES, LANES), F32)],
        scratch_shapes=[pltpu.VMEM((heads, tq), F32), pltpu.VMEM((heads * V_ROWS, tq), F32),
                        pltpu.VMEM((heads * HEAD_DIM, tq), F32)],
        name=name, compiler_params=_cparams(("parallel", "parallel", "arbitrary")),
    )(qt, k, vt, kc, vct, knl, knc)


def _attend(qt, k, vt, kc, vct, knl, knc, *, name, **kw):
    y, lmin = _dense_attn_fast(qt, k, vt, kc, vct, knl, knc, name=name + "_fast", **kw)
    return lax.cond(jnp.min(lmin) > L_MIN_OK, lambda: y,
                    lambda: _dense_attn(qt, k, vt, kc, vct, name=name, **kw))


NA_QROWS = 4
NA_TQ = NA_QROWS * GRID_W
NA_WROWS = 12
NA_WBLOCKS = NA_WROWS // NA_QROWS
NA_STEPS = GRID_H // NA_QROWS


def _natten_body(qt_ref, k0_ref, k1_ref, k2_ref, v0_ref, v1_ref, v2_ref, kc_ref, vct_ref, bias_ref,
                 out_ref, o_ref):
    k_refs = (k0_ref, k1_ref, k2_ref)
    v_refs = (v0_ref, v1_ref, v2_ref)

    def scores(h):
        lanes = slice((h // 2) * LANES, (h // 2 + 1) * LANES)
        qq = qt_ref[0, h * LANES:(h + 1) * LANES, :]
        return [_dot(kc_ref[0, :, lanes], qq)] + [_dot(k_refs[blk][0, :, lanes], qq) for blk in range(NA_WBLOCKS)]

    pending = [scores(h) for h in range(AHEAD)]
    for h in range(B_HEADS):
        s_all = pending.pop(0)
        if h + AHEAD < B_HEADS:
            pending.append(scores(h + AHEAD))
        vrows = slice(h * V_ROWS, (h + 1) * V_ROWS)
        s_all = [s_all[0]] + [s_all[1 + blk] + bias_ref[0, h, blk * NA_TQ:(blk + 1) * NA_TQ, :]
                              for blk in range(NA_WBLOCKS)]
        m = s_all[0].max(axis=0, keepdims=True)
        for s in s_all[1:]:
            m = jnp.maximum(m, s.max(axis=0, keepdims=True))
        pv = _dot(vct_ref[0, vrows, :], jnp.exp2(s_all[0] - m).astype(BF16))
        for blk in range(NA_WBLOCKS):
            pv = pv + _dot(v_refs[blk][0, vrows, :], jnp.exp2(s_all[1 + blk] - m).astype(BF16))
        o_ref[h * HEAD_DIM:(h + 1) * HEAD_DIM, :] = pv[:HEAD_DIM] * (1.0 / pv[HEAD_DIM:HEAD_DIM + 1])
    out_ref[0] = o_ref[...].T.astype(out_ref.dtype)


def _natten_fast_body(qt_ref, k0_ref, k1_ref, k2_ref, v0_ref, v1_ref, v2_ref, kc_ref, vct_ref, bias_ref,
                      knl_ref, knc_ref, bmax_ref, out_ref, lmin_ref, o_ref, kall_ref, vall_ref):
    kall_ref[0:CTX_LEN, :] = kc_ref[0]
    vall_ref[:, 0:CTX_LEN] = vct_ref[0]
    for blk, (kr, vr) in enumerate(((k0_ref, v0_ref), (k1_ref, v1_ref), (k2_ref, v2_ref))):
        kall_ref[CTX_LEN + blk * NA_TQ:CTX_LEN + (blk + 1) * NA_TQ, :] = kr[0]
        vall_ref[:, CTX_LEN + blk * NA_TQ:CTX_LEN + (blk + 1) * NA_TQ] = vr[0]
    kn = jnp.sqrt(jnp.maximum(jnp.max(knl_ref[0], axis=0), knc_ref[0, 0])) * BOUND_SLACK

    def scores(h):
        lanes = slice((h // 2) * LANES, (h // 2 + 1) * LANES)
        return _dot(kall_ref[:, lanes], qt_ref[0, h * LANES:(h + 1) * LANES, :])

    pending = [scores(h) for h in range(AHEAD)]
    lmin = None
    for h in range(B_HEADS):
        s = pending.pop(0)
        if h + AHEAD < B_HEADS:
            pending.append(scores(h + AHEAD))
        qf = qt_ref[0, h * LANES:(h + 1) * LANES, :].astype(F32)
        qn = jnp.sqrt(jnp.sum(qf * qf, axis=0, keepdims=True))
        krow = kn[h // 2:h // 2 + 1, :]
        shift = qn * jnp.concatenate([krow] * (NA_TQ // LANES), axis=1) + bmax_ref[h:h + 1, :]
        p = jnp.concatenate([jnp.exp2(s[:CTX_LEN] - shift).astype(BF16),
                             jnp.exp2(s[CTX_LEN:] + (bias_ref[0, h] - shift)).astype(BF16)], axis=0)
        pv = _dot(vall_ref[h * V_ROWS:(h + 1) * V_ROWS, :], p)
        l = pv[HEAD_DIM:HEAD_DIM + 1]
        lmin = l if lmin is None else jnp.minimum(lmin, l)
        o_ref[h * HEAD_DIM:(h + 1) * HEAD_DIM, :] = pv[:HEAD_DIM] * (1.0 / l)
    out_ref[0] = o_ref[...].T.astype(out_ref.dtype)
    lmin_ref[0, 0] = jnp.broadcast_to(jnp.min(lmin, axis=1, keepdims=True), lmin_ref.shape[2:])


def _natten_call(body, name, qt, k, vt, kc, vct, bias, extra=(), with_lmin=False, gather=False):
    b = qt.shape[0]
    nw = B_HEADS * HEAD_DIM
    nv = B_HEADS * V_ROWS

    def wstart(i):
        return jnp.clip(i - 1, 0, NA_STEPS - NA_WBLOCKS)

    def cls(i):
        return jnp.where(i == 0, 0, jnp.where(i == NA_STEPS - 1, 2, 1))

    in_specs = [pl.BlockSpec((1, B_HEADS * LANES, NA_TQ), lambda bi, i: (bi, 0, i))]
    in_specs += [pl.BlockSpec((1, NA_TQ, nw), functools.partial(lambda bi, i, j: (bi, wstart(i) + j, 0), j=j))
                 for j in range(NA_WBLOCKS)]
    in_specs += [pl.BlockSpec((1, nv, NA_TQ), functools.partial(lambda bi, i, j: (bi, 0, wstart(i) + j), j=j))
                 for j in range(NA_WBLOCKS)]
    in_specs += [pl.BlockSpec((1, CTX_LEN, nw), lambda bi, i: (bi, 0, 0)),
                 pl.BlockSpec((1, nv, CTX_LEN), lambda bi, i: (bi, 0, 0)),
                 pl.BlockSpec((1, B_HEADS, NA_WROWS * GRID_W, NA_TQ), lambda bi, i: (cls(i), 0, 0, 0))]
    for a, per_batch in extra:
        blk = (1,) + a.shape[1:] if per_batch else a.shape
        nz = len(blk) - 1
        in_specs.append(pl.BlockSpec(blk, (lambda bi, i, nz=nz: (bi,) + (0,) * nz) if per_batch
                                     else (lambda bi, i, nz=nz: (0,) * (nz + 1))))
    out_specs = [pl.BlockSpec((1, NA_TQ, nw), lambda bi, i: (bi, i, 0))]
    out_shape = [jax.ShapeDtypeStruct((b, SEQ, nw), BF16)]
    if with_lmin:
        out_specs.append(pl.BlockSpec((1, 1, SUBLANES, LANES), lambda bi, i: (bi, i, 0, 0)))
        out_shape.append(jax.ShapeDtypeStruct((b, NA_STEPS, SUBLANES, LANES), F32))
    return pl.pallas_call(
        body, grid=(b, NA_STEPS), in_specs=in_specs, out_specs=out_specs, out_shape=out_shape,
        scratch_shapes=[pltpu.VMEM((nw, NA_TQ), F32)] + ([pltpu.VMEM((CTX_LEN + NA_WROWS * GRID_W, nw), BF16),
                                                          pltpu.VMEM((nv, CTX_LEN + NA_WROWS * GRID_W), BF16)] if gather else []),
        name=name, compiler_params=_cparams(("parallel", "arbitrary")),
    )(qt, k, k, k, vt, vt, vt, kc, vct, bias, *[a for a, _ in extra])


def _natten(qt, k, vt, kc, vct, rpb, knl, knc):
    bias = _natten_bias(rpb)
    bmax = jnp.maximum(jnp.max(rpb.astype(F32), axis=(1, 2)), 0.0) * LOG2E
    bmax = jnp.broadcast_to(bmax[:, None], (B_HEADS, NA_TQ))
    y, lmin = _natten_call(_natten_fast_body, "natten_fast", qt, k, vt, kc, vct, bias,
                           extra=((knl, True), (knc, True), (bmax, False)), with_lmin=True, gather=True)
    return lax.cond(jnp.min(lmin) > L_MIN_OK, lambda: y,
                    lambda: _natten_call(_natten_body, "natten", qt, k, vt, kc, vct, bias)[0])


def _natten_bias(rpb):
    v = rpb.astype(F32) * LOG2E
    h, nr, _ = v.shape
    period = 2 * GRID_W
    row = jnp.zeros((h, nr, period), F32)
    row = row.at[..., :NA_COLS].set(v[..., NA_COLS - 1::-1])
    row = row.at[..., period - (NA_COLS - 1):].set(v[..., :NA_COLS - 1:-1])
    toep = jnp.tile(row, (1, 1, GRID_W))[..., :GRID_W * (period - 1)]
    toep = toep.reshape(h, nr, GRID_W, period - 1)[..., :GRID_W]
    col = np.arange(GRID_W)
    cs = np.clip(col - NA_COLS // 2, 0, GRID_W - NA_COLS)
    col_ok = (col[:, None] >= cs[None, :]) & (col[:, None] < cs[None, :] + NA_COLS)
    masked = 2.0 * MASK_VALUE
    toep = jnp.where(jnp.asarray(col_ok), toep, masked)
    dead = jnp.full((h, GRID_W, GRID_W), masked, F32)
    tables = []
    for r0 in (0, 2 * NA_QROWS, GRID_H - NA_QROWS):
        ws = min(max(r0 - NA_QROWS, 0), GRID_H - NA_WROWS)
        key_rows = []
        for ki in range(NA_WROWS):
            blocks = []
            for qi in range(NA_QROWS):
                k_row, q_row = ws + ki, r0 + qi
                rs = min(max(q_row - NA_ROWS // 2, 0), GRID_H - NA_ROWS)
                blocks.append(toep[:, k_row - q_row + NA_ROWS - 1] if rs <= k_row < rs + NA_ROWS else dead)
            key_rows.append(jnp.concatenate(blocks, axis=-1))
        tables.append(jnp.concatenate(key_rows, axis=-2))
    return jnp.stack(tables)


def _ffn_body(x_ref, xp_ref, xn_ref, mod_ref, wg_ref, wv_ref, wd_ref, cw_ref, cb_ref, lng_ref, lnb_ref,
              out_ref, hs_ref, g_ref, *, tm, rc, n_tiles):
    i = pl.program_id(1)
    sc = mod_ref[0, SC2:SC2 + 1, :]
    sh = mod_ref[0, SH2:SH2 + 1, :]
    gate_mod = mod_ref[0, G2:G2 + 1, :]

    def modulate(v):
        return v * (1.0 + sc) + sh

    hp = jnp.where(i > 0, modulate(xp_ref[0]), 0.0)
    hn = jnp.where(i < n_tiles - 1, modulate(xn_ref[0]), 0.0)
    hs_ref[0:FFN_HALO, :] = hp.astype(BF16)
    hs_ref[FFN_HALO:FFN_HALO + tm, :] = modulate(x_ref[0]).astype(BF16)
    hs_ref[FFN_HALO + tm:, :] = hn.astype(BF16)

    def up(c):
        lo = c * rc
        g_ref[c] = _dot(hs_ref[lo:lo + rc + 2 * FFN_HALO, :], wg_ref[...])
        return _dot(hs_ref[FFN_HALO + lo:FFN_HALO + lo + rc, :], wv_ref[...])

    n_chunks = tm // rc
    val_next = up(0)
    for c in range(n_chunks):
        val = val_next
        if c + 1 < n_chunks:
            val_next = up(c + 1)
        gate = (cw_ref[0:1, :] * g_ref[c, FFN_HALO - 1:FFN_HALO - 1 + rc, :]
                + cw_ref[1:2, :] * g_ref[c, FFN_HALO:FFN_HALO + rc, :]
                + cw_ref[2:3, :] * g_ref[c, FFN_HALO + 1:FFN_HALO + 1 + rc, :]
                + cb_ref[...])
        act = gate * jax.nn.sigmoid(gate) * val
        y = _dot(act.astype(BF16), wd_ref[...])
        rows = slice(c * rc, (c + 1) * rc)
        out_ref[0, rows, :] = _layer_norm(DEEPNORM_ALPHA * x_ref[0, rows, :] + gate_mod * y, lng_ref[...], lnb_ref[...])


def _ffn(x, mod, mod_row, wup, wd, conv_w, conv_b, ln_g, ln_b, *, tm, name):
    b, t, d = x.shape
    n_tiles = t // tm
    rc = min(FFN_ROWS, tm)
    hb = tm // FFN_HALO
    n_hblocks = t // FFN_HALO
    mod_map = (lambda bi, i: (bi, 0, 0)) if mod_row is None else (lambda bi, i: (mod_row, 0, 0))

    def resident(shape, col=0):
        return pl.BlockSpec(shape, lambda bi, i: (0, col), pipeline_mode=pl.Buffered(1))

    in_specs = [
        pl.BlockSpec((1, tm, d), lambda bi, i: (bi, i, 0)),
        pl.BlockSpec((1, FFN_HALO, d), lambda bi, i: (bi, jnp.maximum(i * hb - 1, 0), 0)),
        pl.BlockSpec((1, FFN_HALO, d), lambda bi, i: (bi, jnp.minimum((i + 1) * hb, n_hblocks - 1), 0)),
        pl.BlockSpec((1, N_MOD, d), mod_map),
        resident((d, D_FF), 0), resident((d, D_FF), 1), resident((D_FF, d)),
        resident((3, D_FF)), resident((1, D_FF)), resident((1, d)), resident((1, d)),
    ]
    body = functools.partial(_ffn_body, tm=tm, rc=rc, n_tiles=n_tiles)
    return pl.pallas_call(
        body, grid=(b, n_tiles), in_specs=in_specs,
        out_specs=pl.BlockSpec((1, tm, d), lambda bi, i: (bi, i, 0)),
        out_shape=jax.ShapeDtypeStruct((b, t, d), F32),
        scratch_shapes=[pltpu.VMEM((tm + 2 * FFN_HALO, d), BF16),
                        pltpu.VMEM((tm // rc, rc + 2 * FFN_HALO, D_FF), F32)],
        name=name, compiler_params=_cparams(("parallel", "parallel")),
    )(x, x, x, mod, wup, wup, wd, conv_w.astype(F32), conv_b.reshape(1, D_FF).astype(F32),
      ln_g.reshape(1, d).astype(F32), ln_b.reshape(1, d).astype(F32))


def _rope_cos_sin(rot_dim):
    pos = jnp.arange(SEQ, dtype=jnp.int32)
    rows = (pos // GRID_W).astype(F32)
    cols = (pos % GRID_W).astype(F32)
    axis_dim = rot_dim // 2
    inv = ROPE_THETA ** (-jnp.arange(0, axis_dim, 2, dtype=F32) / axis_dim)
    ang = jnp.concatenate([rows[:, None] * inv, cols[:, None] * inv], axis=-1)
    cos = jnp.repeat(jnp.cos(ang), 2, axis=-1)
    sin = jnp.repeat(jnp.sin(ang), 2, axis=-1) * jnp.tile(jnp.array([-1.0, 1.0], F32), rot_dim // 2)
    return cos, sin


def _swap_pairs(g):
    return g.reshape(-1, 2)[:, ::-1].reshape(-1)


def _gqa_tables(gain, scale, length, rope):
    if rope:
        cos, sin = _rope_cos_sin(HEAD_DIM)
    else:
        cos, sin = jnp.ones((length, HEAD_DIM), F32), jnp.zeros((length, HEAD_DIM), F32)
    c = cos * (gain * scale)
    s = sin * (_swap_pairs(gain) * scale)
    return jnp.concatenate([c, c], axis=-1), jnp.concatenate([s, s], axis=-1)


def _mla_tables(scale, length, rope):
    c = jnp.full((length, LANES), scale, F32)
    s = jnp.zeros((length, LANES), F32)
    if rope:
        cos, sin = _rope_cos_sin(MLA_ROPE_DIM)
        lo, hi = MLA_NOPE_DIM, MLA_NOPE_DIM + MLA_ROPE_DIM
        c = c.at[:, lo:hi].set(cos * scale)
        s = s.at[:, lo:hi].set(sin * scale)
    return c, s


def _slot_rows(w, heads, width, half_of):
    k = w.shape[0]
    out = jnp.zeros((heads * LANES, k), w.dtype)
    for h in range(heads):
        off = h * LANES + half_of(h) * HALF
        out = out.at[off:off + width].set(w[:, h * width:(h + 1) * width].T)
    return out


def _layer0_mixer(x, xc, mod, w_in, q_gain, k_gain, rpb):
    w = w_in.astype(BF16)
    aq, akv, bw = A_HEADS * HEAD_DIM, A_KV_HEADS * HEAD_DIM, B_HEADS * HEAD_DIM
    o = 0
    w_qa = w[:, o:o + aq]; o += aq
    w_ka = w[:, o:o + akv]; o += akv
    w_va = w[:, o:o + akv]; o += akv
    w_qb = w[:, o:o + bw]; o += bw
    w_kb = w[:, o:o + bw]; o += bw
    w_vb = w[:, o:o + bw]
    wqa = jnp.concatenate([w_qa, w_va], axis=1).T
    wqb = jnp.concatenate([w_qb, w_vb], axis=1).T
    wka = jnp.concatenate([w_ka, w_kb], axis=1)
    weights = (wqa, wka, wqb)
    qscale = HEAD_SCALE * LOG2E

    def tables(length, rope):
        cq, sq = _gqa_tables(q_gain, qscale, length, rope)
        ck, sk = _gqa_tables(k_gain, 1.0, length, rope)
        return cq[:, :HEAD_DIM].T, sq[:, :HEAD_DIM].T, ck, sk

    qa_t, ka, kna, va_t, qb_t, kb, knb, vb_t = _l0_proj(x, mod, None, weights, tables(SEQ, True), tm=TM_WIDE, name="l0_proj")
    qac_t, kac, knac, vac_t, qbc_t, kbc, knbc, vbc_t = _l0_proj(xc, mod, CTX_MOD_ROW, weights, tables(CTX_LEN, False),
                                                    tm=TMC, name="l0_proj_ctx")

    a_kslot = [0] * A_HEADS
    a_vrow = [(h // A_GROUP) * V_ROWS for h in range(A_HEADS)]
    b_kslot = [h // 2 for h in range(B_HEADS)]
    b_vrow = [h * V_ROWS for h in range(B_HEADS)]
    ya = _attend(qa_t, ka, va_t, kac, vac_t, kna, knac, heads=A_HEADS, k_slot=a_kslot, v_row=a_vrow,
                 tq=TQ, tk=2 * TK, name="gqa")
    yb = _natten(qb_t, kb, vb_t, kbc, vbc_t, rpb, knb, knbc)
    yac = _dense_attn(qac_t, kac, vac_t, heads=A_HEADS, k_slot=a_kslot, v_row=a_vrow,
                      tq=CTX_LEN, tk=CTX_LEN, name="gqa_ctx")
    ybc = _dense_attn(qbc_t, kbc, vbc_t, heads=B_HEADS, k_slot=b_kslot, v_row=b_vrow,
                      tq=CTX_LEN, tk=CTX_LEN, name="nbr_ctx")
    return (ya, yb), (yac, ybc)


def _layer1_mixer(x, xc, mod, w_in, cq_gain, ckv_gain, w_uq, w_ukv):
    qk_dim = MLA_NOPE_DIM + MLA_ROPE_DIM
    kv_dim = MLA_NOPE_DIM + MLA_V_DIM
    n_in = MLA_Q_LORA + MLA_KV_LORA
    wdn = jnp.zeros((D_MODEL, n_in + LANES), BF16)
    wdn = wdn.at[:, :n_in].set(w_in[:, :n_in].astype(BF16))
    wdn = wdn.at[:, n_in + MLA_NOPE_DIM:n_in + qk_dim].set(w_in[:, n_in:].astype(BF16))
    wq = _slot_rows(w_uq.astype(BF16), MLA_HEADS, qk_dim, lambda h: 0)
    wkv = w_ukv.astype(BF16).reshape(MLA_KV_LORA, MLA_HEADS, kv_dim)
    wk = _slot_rows(wkv[:, :, :MLA_NOPE_DIM].reshape(MLA_KV_LORA, -1), MLA_HEADS, MLA_NOPE_DIM, lambda h: 0).T
    wv = wkv[:, :, MLA_NOPE_DIM:].reshape(MLA_KV_LORA, MLA_HEADS * MLA_V_DIM).T
    weights = (wdn, wq, wk, wv)
    gains = (cq_gain.reshape(1, -1).astype(F32), ckv_gain.reshape(1, -1).astype(F32))

    cq, sq = _mla_tables(MLA_SCALE * LOG2E, SEQ, True)
    qt, kk, kn, vt = _l1_proj(x, mod, None, weights, gains, (cq.T, sq.T), _mla_tables(1.0, SEQ, True),
                          tm=TM, with_q=True, name="l1_proj")
    kkc, knc, vtc = _l1_proj(xc, mod, CTX_MOD_ROW, weights, gains, None, _mla_tables(1.0, CTX_LEN, False),
                        tm=TMC, with_q=False, name="l1_proj_ctx")
    return _attend(qt, kk, vt, kkc, vtc, kn, knc, heads=MLA_HEADS, k_slot=list(range(MLA_HEADS)),
                   v_row=[h * V_ROWS for h in range(MLA_HEADS)], tq=TQ, tk=TK, name="mla")


def _post_mixer(x, y, mod, mod_row, w_out, ln1, w_up, conv_w, conv_b, w_down, ln2, tm, tag):
    wo = w_out.astype(BF16)
    if isinstance(y, tuple):
        n0 = y[0].shape[2]
        extra = dict(x2=y[1], w2=wo[n0:])
        y, wo = y[0], wo[:n0]
    else:
        extra = {}
    x = _proj(y, wo, name="out_ln" + tag, tm=min(TM_WIDE, x.shape[1]), tn=D_MODEL, mod=mod, mod_row=mod_row,
              gate_row=G1, resid=x, ln=ln1, **extra)
    return _ffn(x, mod, mod_row, w_up.astype(BF16), w_down.astype(BF16), conv_w, conv_b, ln2[0], ln2[1],
                tm=tm, name="ffn" + tag)


def kernel(x, c, ctx, c_ctx, l0_w_ada, l0_b_ada, l0_w_in, l0_q_gain, l0_k_gain, l0_rpb, l0_w_out, l0_ln1_g, l0_ln1_b, l0_w_up, l0_conv_w, l0_conv_b, l0_w_down, l0_ln2_g, l0_ln2_b, l1_w_ada, l1_b_ada, l1_w_in, l1_cq_gain, l1_ckv_gain, l1_w_uq, l1_w_ukv, l1_w_out, l1_ln1_g, l1_ln1_b, l1_w_up, l1_conv_w, l1_conv_b, l1_w_down, l1_ln2_g, l1_ln2_b):
    xc = ctx
    mod = _modvec(c, c_ctx, l0_w_ada, l0_b_ada)
    y, yc = _layer0_mixer(x, xc, mod, l0_w_in, l0_q_gain, l0_k_gain, l0_rpb)
    post0 = (l0_w_out, (l0_ln1_g, l0_ln1_b), l0_w_up, l0_conv_w, l0_conv_b, l0_w_down, (l0_ln2_g, l0_ln2_b))
    x = _post_mixer(x, y, mod, None, *post0, tm=TM, tag="0")
    xc = _post_mixer(xc, yc, mod, CTX_MOD_ROW, *post0, tm=TMC, tag="0_ctx")
    mod = _modvec(c, c_ctx, l1_w_ada, l1_b_ada)
    y = _layer1_mixer(x, xc, mod, l1_w_in, l1_cq_gain, l1_ckv_gain, l1_w_uq, l1_w_ukv)
    post1 = (l1_w_out, (l1_ln1_g, l1_ln1_b), l1_w_up, l1_conv_w, l1_conv_b, l1_w_down, (l1_ln2_g, l1_ln2_b))
    return _post_mixer(x, y, mod, None, *post1, tm=TM, tag="1")
```

```python
import functools
import math

import numpy as np
import jax
import jax.numpy as jnp
from jax import lax
from jax.experimental import pallas as pl
from jax.experimental.pallas import tpu as pltpu

D_MODEL = 1024
BATCH = 4
SEQ = 4096
DEPTH = 2
GRID_W = 64
GRID_H = SEQ // GRID_W
CTX_LEN = 256
HEAD_DIM = 64
A_HEADS = 8
A_KV_HEADS = 2
A_GROUP = A_HEADS // A_KV_HEADS
B_HEADS = 8
NA_ROWS = 8
NA_COLS = 16
ROPE_THETA = 10000.0
MLA_HEADS = 16
MLA_Q_LORA = 768
MLA_KV_LORA = 256
MLA_NOPE_DIM = 64
MLA_ROPE_DIM = 32
MLA_V_DIM = 64
D_FF = 2816
N_MOD = 6
EPS = 1e-6
DEEPNORM_ALPHA = (2 * DEPTH) ** 0.25
HEAD_SCALE = HEAD_DIM ** -0.5
MLA_SCALE = (MLA_NOPE_DIM + MLA_ROPE_DIM) ** -0.5
LOG2E = math.log2(math.e)

LANES = 128
SUBLANES = 8
HALF = LANES // 2
BF16_ROWS = 16
V_ROWS = HEAD_DIM + BF16_ROWS
MASK_VALUE = -1e30
VMEM_LIMIT = 56 * 1024 * 1024

SH1, SC1, G1, SH2, SC2, G2 = range(N_MOD)

F32 = jnp.float32
BF16 = jnp.bfloat16

MOD_ROWS = 16
CTX_MOD_ROW = BATCH
TM = 512
TM_WIDE = 1024
TMC = CTX_LEN
TQ = 1024
TK = 512
AHEAD = 2
BOUND_SLACK = 1.0 + 1e-3
L_MIN_OK = 2.0 ** -80
FFN_ROWS = 256
FFN_HALO = BF16_ROWS
ROW_CHUNK = 256


def _cparams(sem):
    return pltpu.CompilerParams(dimension_semantics=sem, vmem_limit_bytes=VMEM_LIMIT)


def _dot(a, b):
    return jnp.dot(a, b, preferred_element_type=F32)


def _dot_nt(a, b):
    return lax.dot_general(a, b, (((1,), (1,)), ((), ())), preferred_element_type=F32)


def _pair_swap(y, axis):
    n = y.shape[axis]
    idx = lax.broadcasted_iota(jnp.int32, y.shape, axis)
    nxt = pltpu.roll(y, n - 1, axis=axis)
    prv = pltpu.roll(y, 1, axis=axis)
    return jnp.where(idx % 2 == 0, nxt, prv)


def _rotate(y, c, s, axis):
    return y * c + _pair_swap(y, axis) * s


def _layer_norm(z, g, b):
    mu = jnp.mean(z, axis=-1, keepdims=True)
    zc = z - mu
    var = jnp.mean(zc * zc, axis=-1, keepdims=True)
    return zc * lax.rsqrt(var + EPS) * g + b


def _slot_norm_rows(kb, n_slots):
    rows = []
    for sl in range(n_slots):
        kf = kb[:, sl * LANES:(sl + 1) * LANES].astype(F32)
        n2 = jnp.sum(kf * kf, axis=-1, keepdims=True)
        rows.append(jnp.broadcast_to(jnp.max(n2, axis=0, keepdims=True), (1, LANES)))
    return rows[0] if n_slots == 1 else jnp.concatenate(rows, axis=0)


def _store_vt(out_ref, yt, heads):
    ones = jnp.ones((BF16_ROWS, yt.shape[1]), out_ref.dtype)
    for h in range(heads):
        out_ref[0, h * V_ROWS:h * V_ROWS + HEAD_DIM, :] = yt[h * HEAD_DIM:(h + 1) * HEAD_DIM].astype(out_ref.dtype)
        out_ref[0, h * V_ROWS + HEAD_DIM:(h + 1) * V_ROWS, :] = ones


def _proj_body(*refs, silu, bias, resid_ln, gate_row, two):
    it = iter(refs)
    x_ref = next(it)
    w_ref = next(it)
    x2_ref = next(it) if two else None
    w2_ref = next(it) if two else None
    bias_ref = next(it) if bias else None
    mod_ref = next(it) if resid_ln else None
    xres_ref = next(it) if resid_ln else None
    lng_ref = next(it) if resid_ln else None
    lnb_ref = next(it) if resid_ln else None
    out_ref = next(it)
    x = x_ref[0]
    if silu:
        x = x * jax.nn.sigmoid(x)
    y = _dot(x.astype(BF16), w_ref[...].astype(BF16))
    if two:
        y = y + _dot(x2_ref[0], w2_ref[...])
    if bias:
        y = y + bias_ref[...]
    if resid_ln:
        g = mod_ref[0, gate_row:gate_row + 1, :]
        y = _layer_norm(DEEPNORM_ALPHA * xres_ref[0] + g * y, lng_ref[...], lnb_ref[...])
    out_ref[0] = y.astype(out_ref.dtype)


def _proj(x, w, *, name, tm, tn, x2=None, w2=None, silu=False, bias=None, mod=None, mod_row=None,
          gate_row=None, resid=None, ln=None, out_dtype=F32):
    b, t, k = x.shape
    n = w.shape[1]
    assert t % tm == 0 and n % tn == 0
    resid_ln = resid is not None
    in_specs = [pl.BlockSpec((1, tm, k), lambda bi, i, j: (bi, i, 0)),
                pl.BlockSpec((k, tn), lambda bi, i, j: (0, j))]
    args = [x, w]
    if x2 is not None:
        k2 = x2.shape[2]
        in_specs += [pl.BlockSpec((1, tm, k2), lambda bi, i, j: (bi, i, 0)),
                     pl.BlockSpec((k2, tn), lambda bi, i, j: (0, j))]
        args += [x2, w2]
    if bias is not None:
        in_specs.append(pl.BlockSpec((1, tn), lambda bi, i, j: (0, j)))
        args.append(bias.reshape(1, n).astype(F32))
    if resid_ln:
        mod_map = (lambda bi, i, j: (bi, 0, 0)) if mod_row is None else (lambda bi, i, j: (mod_row, 0, 0))
        in_specs += [pl.BlockSpec((1, N_MOD, D_MODEL), mod_map),
                     pl.BlockSpec((1, tm, n), lambda bi, i, j: (bi, i, 0)),
                     pl.BlockSpec((1, n), lambda bi, i, j: (0, 0)),
                     pl.BlockSpec((1, n), lambda bi, i, j: (0, 0))]
        args += [mod, resid, ln[0].reshape(1, n).astype(F32), ln[1].reshape(1, n).astype(F32)]
    body = functools.partial(_proj_body, silu=silu, bias=bias is not None, resid_ln=resid_ln, gate_row=gate_row,
                             two=x2 is not None)
    return pl.pallas_call(
        body, grid=(b, t // tm, n // tn), in_specs=in_specs,
        out_specs=pl.BlockSpec((1, tm, tn), lambda bi, i, j: (bi, i, j)),
        out_shape=jax.ShapeDtypeStruct((b, t, n), out_dtype), name=name,
        compiler_params=_cparams(("parallel", "parallel", "arbitrary")),
    )(*args)


def _modvec(c, c_ctx, w_ada, b_ada):
    cond = jnp.zeros((1, MOD_ROWS, D_MODEL), F32).at[0, :BATCH].set(c).at[0, BATCH].set(c_ctx)
    m = _proj(cond, w_ada, name="adaln", tm=MOD_ROWS, tn=1024, silu=True, bias=b_ada)
    return m.reshape(MOD_ROWS, N_MOD, D_MODEL)


def _l0_proj_body(x_ref, mod_ref, wqa_ref, cqa_ref, sqa_ref, wka_ref, cka_ref, ska_ref, wqb_ref,
                  qa_ref, ka_ref, kna_ref, va_ref, qb_ref, kb_ref, knb_ref, vb_ref):
    sc = mod_ref[0, SC1:SC1 + 1, :]
    sh = mod_ref[0, SH1:SH1 + 1, :]
    xs = (x_ref[0] * (1.0 + sc) + sh).astype(BF16)

    cq, sq = cqa_ref[...], sqa_ref[...]
    zeros = jnp.zeros((HALF, xs.shape[0]), qa_ref.dtype)
    yt_a = _dot_nt(wqa_ref[...], xs)
    yt = yt_a[:A_HEADS * HEAD_DIM]
    for h in range(A_HEADS):
        seg = yt[h * HEAD_DIM:(h + 1) * HEAD_DIM]
        ms = jnp.mean(seg * seg, axis=0, keepdims=True)
        q = (_rotate(seg, cq, sq, 0) * lax.rsqrt(ms + EPS)).astype(qa_ref.dtype)
        half = h // A_GROUP
        qa_ref[0, h * LANES + half * HALF:h * LANES + (half + 1) * HALF, :] = q
        qa_ref[0, h * LANES + (1 - half) * HALF:h * LANES + (2 - half) * HALF, :] = zeros

    y_k = _dot(xs, wka_ref[...])
    y = y_k[:, :LANES]
    lo = lax.broadcasted_iota(jnp.int32, y.shape, 1) < HALF
    ysq = y * y
    ms_lo = jnp.sum(jnp.where(lo, ysq, 0.0), axis=-1, keepdims=True) * (1.0 / HEAD_DIM)
    ms_hi = jnp.sum(jnp.where(lo, 0.0, ysq), axis=-1, keepdims=True) * (1.0 / HEAD_DIM)
    rn = jnp.where(lo, lax.rsqrt(ms_lo + EPS), lax.rsqrt(ms_hi + EPS))
    ka = (_rotate(y, cka_ref[...], ska_ref[...], 1) * rn).astype(ka_ref.dtype)
    ka_ref[0] = ka
    kna_ref[0, 0] = jnp.broadcast_to(_slot_norm_rows(ka, 1), kna_ref.shape[2:])

    _store_vt(va_ref, yt_a[A_HEADS * HEAD_DIM:], A_KV_HEADS)

    yt_b = _dot_nt(wqb_ref[...], xs)
    yt = yt_b[:B_HEADS * HEAD_DIM] * (HEAD_SCALE * LOG2E)
    for h in range(B_HEADS):
        half = h % 2
        qb_ref[0, h * LANES + half * HALF:h * LANES + (half + 1) * HALF, :] = (
            yt[h * HEAD_DIM:(h + 1) * HEAD_DIM].astype(qb_ref.dtype))
        qb_ref[0, h * LANES + (1 - half) * HALF:h * LANES + (2 - half) * HALF, :] = zeros
    kb = y_k[:, LANES:].astype(kb_ref.dtype)
    kb_ref[0] = kb
    n_pairs = B_HEADS // 2
    knb_ref[0, 0] = jnp.concatenate([_slot_norm_rows(kb, n_pairs), jnp.zeros((SUBLANES - n_pairs, LANES), F32)], axis=0)
    _store_vt(vb_ref, yt_b[B_HEADS * HEAD_DIM:], B_HEADS)


def _l0_proj(x, mod, mod_row, weights, tables, *, tm, name):
    b, t, d = x.shape
    wqa, wka, wqb = weights
    cqa, sqa, cka, ska = tables
    mod_map = (lambda bi, i: (bi, 0, 0)) if mod_row is None else (lambda bi, i: (mod_row, 0, 0))

    def full(a):
        return pl.BlockSpec(a.shape, lambda bi, i: (0,) * a.ndim)

    in_specs = [pl.BlockSpec((1, tm, d), lambda bi, i: (bi, i, 0)),
                pl.BlockSpec((1, N_MOD, d), mod_map),
                full(wqa), pl.BlockSpec((HEAD_DIM, tm), lambda bi, i: (0, i)), pl.BlockSpec((HEAD_DIM, tm), lambda bi, i: (0, i)),
                full(wka), pl.BlockSpec((tm, LANES), lambda bi, i: (i, 0)), pl.BlockSpec((tm, LANES), lambda bi, i: (i, 0)),
                full(wqb)]
    nb = B_HEADS * HEAD_DIM

    def nat(n):
        return jax.ShapeDtypeStruct((b, t, n), BF16), pl.BlockSpec((1, tm, n), lambda bi, i: (bi, i, 0))

    def tr(n):
        return jax.ShapeDtypeStruct((b, n, t), BF16), pl.BlockSpec((1, n, tm), lambda bi, i: (bi, 0, i))

    kn = (jax.ShapeDtypeStruct((b, t // tm, SUBLANES, LANES), F32), pl.BlockSpec((1, 1, SUBLANES, LANES), lambda bi, i: (bi, i, 0, 0)))
    outs = [tr(A_HEADS * LANES), nat(LANES), kn, tr(A_KV_HEADS * V_ROWS), tr(B_HEADS * LANES), nat(nb), kn, tr(B_HEADS * V_ROWS)]
    return pl.pallas_call(
        _l0_proj_body, grid=(b, t // tm), in_specs=in_specs,
        out_specs=[o[1] for o in outs], out_shape=[o[0] for o in outs], name=name,
        compiler_params=_cparams(("parallel", "parallel")),
    )(x, mod, wqa, cqa, sqa, wka, cka, ska, wqb)


def _l1_proj_body(*refs, with_q):
    it = iter(refs)
    x_ref, mod_ref, wdn_ref, gq_ref, gkv_ref = (next(it) for _ in range(5))
    wq_ref, cq_ref, sq_ref = (next(it) for _ in range(3)) if with_q else (None, None, None)
    wk_ref, ck_ref, sk_ref, wv_ref = (next(it) for _ in range(4))
    q_ref = next(it) if with_q else None
    k_ref, kn_ref, v_ref = next(it), next(it), next(it)

    sc = mod_ref[0, SC1:SC1 + 1, :]
    sh = mod_ref[0, SH1:SH1 + 1, :]
    xs = (x_ref[0] * (1.0 + sc) + sh).astype(BF16)

    def rms(v, g_ref):
        ms = jnp.mean(v * v, axis=-1, keepdims=True)
        return (v * lax.rsqrt(ms + EPS) * g_ref[...]).astype(BF16)

    n_q, n_kv = MLA_Q_LORA, MLA_KV_LORA
    c_kv = rms(_dot(xs, wdn_ref[:, n_q:n_q + n_kv]), gkv_ref)
    k_r = _rotate(_dot(xs, wdn_ref[:, n_q + n_kv:]), ck_ref[...], sk_ref[...], 1)

    if with_q:
        c_q = rms(_dot(xs, wdn_ref[:, :n_q]), gq_ref)
        cq = jnp.concatenate([cq_ref[...]] * (ROW_CHUNK // LANES), axis=0)
        sq = jnp.concatenate([sq_ref[...]] * (ROW_CHUNK // LANES), axis=0)
        for r in range(MLA_HEADS * LANES // ROW_CHUNK):
            rows = slice(r * ROW_CHUNK, (r + 1) * ROW_CHUNK)
            q_ref[0, rows, :] = _rotate(_dot_nt(wq_ref[rows, :], c_q), cq, sq, 0).astype(q_ref.dtype)

    k_r_tiled = jnp.concatenate([k_r] * (ROW_CHUNK // LANES), axis=-1)
    for j in range(MLA_HEADS * LANES // ROW_CHUNK):
        cols = slice(j * ROW_CHUNK, (j + 1) * ROW_CHUNK)
        kb = (_dot(c_kv, wk_ref[:, cols]) + k_r_tiled).astype(k_ref.dtype)
        k_ref[0, :, cols] = kb
        n_sl = ROW_CHUNK // LANES
        kn_ref[0, 0, j * n_sl:(j + 1) * n_sl, :] = _slot_norm_rows(kb, n_sl)

    _store_vt(v_ref, _dot_nt(wv_ref[...], c_kv), MLA_HEADS)


def _l1_proj(x, mod, mod_row, weights, gains, q_tables, k_tables, *, tm, with_q, name):
    b, t, d = x.shape
    wdn, wq, wk, wv = weights
    mod_map = (lambda bi, i: (bi, 0, 0)) if mod_row is None else (lambda bi, i: (mod_row, 0, 0))

    def full(a):
        return pl.BlockSpec(a.shape, lambda bi, i: (0,) * a.ndim)

    in_specs = [pl.BlockSpec((1, tm, d), lambda bi, i: (bi, i, 0)), pl.BlockSpec((1, N_MOD, d), mod_map),
                full(wdn), full(gains[0]), full(gains[1])]
    args = [x, mod, wdn, gains[0], gains[1]]
    if with_q:
        in_specs += [full(wq), pl.BlockSpec((LANES, tm), lambda bi, i: (0, i)), pl.BlockSpec((LANES, tm), lambda bi, i: (0, i))]
        args += [wq, q_tables[0], q_tables[1]]
    in_specs += [full(wk), pl.BlockSpec((tm, LANES), lambda bi, i: (i, 0)), pl.BlockSpec((tm, LANES), lambda bi, i: (i, 0)), full(wv)]
    args += [wk, k_tables[0], k_tables[1], wv]
    out_shape, out_specs = [], []
    if with_q:
        out_shape.append(jax.ShapeDtypeStruct((b, MLA_HEADS * LANES, t), BF16))
        out_specs.append(pl.BlockSpec((1, MLA_HEADS * LANES, tm), lambda bi, i: (bi, 0, i)))
    out_shape += [jax.ShapeDtypeStruct((b, t, MLA_HEADS * LANES), BF16),
                  jax.ShapeDtypeStruct((b, t // tm, MLA_HEADS, LANES), F32),
                  jax.ShapeDtypeStruct((b, MLA_HEADS * V_ROWS, t), BF16)]
    out_specs += [pl.BlockSpec((1, tm, MLA_HEADS * LANES), lambda bi, i: (bi, i, 0)),
                  pl.BlockSpec((1, 1, MLA_HEADS, LANES), lambda bi, i: (bi, i, 0, 0)),
                  pl.BlockSpec((1, MLA_HEADS * V_ROWS, tm), lambda bi, i: (bi, 0, i))]
    return pl.pallas_call(
        functools.partial(_l1_proj_body, with_q=with_q), grid=(b, t // tm), in_specs=in_specs,
        out_specs=out_specs, out_shape=out_shape, name=name,
        compiler_params=_cparams(("parallel", "parallel")),
    )(*args)


def _dense_body(*refs, heads, k_slot, v_row, nk, has_ctx):
    if has_ctx:
        qt_ref, k_ref, vt_ref, kc_ref, vct_ref, out_ref, m_ref, l_ref, acc_ref = refs
    else:
        qt_ref, k_ref, vt_ref, out_ref, m_ref, l_ref, acc_ref = refs
    ki = pl.program_id(2)
    last = nk if has_ctx else nk - 1

    @pl.when(ki == 0)
    def _():
        m_ref[...] = jnp.full(m_ref.shape, MASK_VALUE, F32)
        l_ref[...] = jnp.zeros(l_ref.shape, F32)
        acc_ref[...] = jnp.zeros(acc_ref.shape, F32)

    def step(kr, vr):
        def scores(h):
            ks = k_slot[h]
            return _dot(kr[0, :, ks * LANES:(ks + 1) * LANES], qt_ref[0, h * LANES:(h + 1) * LANES, :])

        pending = [scores(h) for h in range(min(AHEAD, heads))]
        for h in range(heads):
            s = pending.pop(0)
            if h + AHEAD < heads:
                pending.append(scores(h + AHEAD))
            rows = slice(h * HEAD_DIM, (h + 1) * HEAD_DIM)
            m_prev = m_ref[h:h + 1, :]
            m_new = jnp.maximum(m_prev, jnp.max(s, axis=0, keepdims=True))
            alpha = jnp.exp2(m_prev - m_new)
            p = jnp.exp2(s - m_new).astype(BF16)
            pv = _dot(vr[0, v_row[h]:v_row[h] + V_ROWS, :], p)
            m_ref[h:h + 1, :] = m_new
            l_ref[h:h + 1, :] = alpha * l_ref[h:h + 1, :] + pv[HEAD_DIM:HEAD_DIM + 1]
            acc_ref[rows, :] = alpha * acc_ref[rows, :] + pv[:HEAD_DIM]

    if has_ctx:
        @pl.when(ki < nk)
        def _():
            step(k_ref, vt_ref)

        @pl.when(ki == nk)
        def _():
            step(kc_ref, vct_ref)
    else:
        step(k_ref, vt_ref)

    @pl.when(ki == last)
    def _():
        for h in range(heads):
            rows = slice(h * HEAD_DIM, (h + 1) * HEAD_DIM)
            acc_ref[rows, :] = acc_ref[rows, :] * (1.0 / l_ref[h:h + 1, :])
        out_ref[0] = acc_ref[...].T.astype(out_ref.dtype)


def _dense_attn(qt, k, vt, kc=None, vct=None, *, heads, k_slot, v_row, tq, tk, name):
    b, nq, lq = qt.shape
    _, lk, nkw = k.shape
    nv = vt.shape[1]
    assert nq == heads * LANES and lq % tq == 0 and lk % tk == 0
    nk = lk // tk
    has_ctx = kc is not None
    steps = nk + (1 if has_ctx else 0)
    in_specs = [
        pl.BlockSpec((1, nq, tq), lambda bi, qi, ki: (bi, 0, qi)),
        pl.BlockSpec((1, tk, nkw), lambda bi, qi, ki: (bi, jnp.minimum(ki, nk - 1), 0)),
        pl.BlockSpec((1, nv, tk), lambda bi, qi, ki: (bi, 0, jnp.minimum(ki, nk - 1))),
    ]
    args = [qt, k, vt]
    if has_ctx:
        lc = kc.shape[1]
        in_specs += [pl.BlockSpec((1, lc, nkw), lambda bi, qi, ki: (bi, 0, 0)),
                     pl.BlockSpec((1, nv, lc), lambda bi, qi, ki: (bi, 0, 0))]
        args += [kc, vct]
    body = functools.partial(_dense_body, heads=heads, k_slot=tuple(k_slot), v_row=tuple(v_row),
                             nk=nk, has_ctx=has_ctx)
    return pl.pallas_call(
        body, grid=(b, lq // tq, steps), in_specs=in_specs,
        out_specs=pl.BlockSpec((1, tq, heads * HEAD_DIM), lambda bi, qi, ki: (bi, qi, 0)),
        out_shape=jax.ShapeDtypeStruct((b, lq, heads * HEAD_DIM), BF16),
        scratch_shapes=[pltpu.VMEM((heads, tq), F32), pltpu.VMEM((heads, tq), F32),
                        pltpu.VMEM((heads * HEAD_DIM, tq), F32)],
        name=name, compiler_params=_cparams(("parallel", "parallel", "arbitrary")),
    )(*args)


def _dense_fast_body(qt_ref, k_ref, vt_ref, kc_ref, vct_ref, knl_ref, knc_ref, out_ref, lmin_ref,
                     b_ref, acc_ref, o_ref, *, heads, k_slot, v_row, nk):
    ki = pl.program_id(2)
    tq = qt_ref.shape[2]

    @pl.when(ki == 0)
    def _():
        kn = jnp.sqrt(jnp.maximum(jnp.max(knl_ref[0], axis=0), knc_ref[0, 0])) * BOUND_SLACK
        for h in range(heads):
            qf = qt_ref[0, h * LANES:(h + 1) * LANES, :].astype(F32)
            qn = jnp.sqrt(jnp.sum(qf * qf, axis=0, keepdims=True))
            krow = kn[k_slot[h]:k_slot[h] + 1, :]
            b_ref[h:h + 1, :] = qn * jnp.concatenate([krow] * (tq // LANES), axis=1)
        acc_ref[...] = jnp.zeros(acc_ref.shape, F32)

    def step(kr, vr):
        def scores(h):
            ks = k_slot[h]
            return _dot(kr[0, :, ks * LANES:(ks + 1) * LANES], qt_ref[0, h * LANES:(h + 1) * LANES, :])

        pending = [scores(h) for h in range(min(AHEAD, heads))]
        for h in range(heads):
            s = pending.pop(0)
            if h + AHEAD < heads:
                pending.append(scores(h + AHEAD))
            p = jnp.exp2(s - b_ref[h:h + 1, :]).astype(BF16)
            rows = slice(h * V_ROWS, (h + 1) * V_ROWS)
            acc_ref[rows, :] += _dot(vr[0, v_row[h]:v_row[h] + V_ROWS, :], p)

    @pl.when(ki < nk)
    def _():
        step(k_ref, vt_ref)

    @pl.when(ki == nk)
    def _():
        step(kc_ref, vct_ref)
        lmin = None
        for h in range(heads):
            l = acc_ref[h * V_ROWS + HEAD_DIM:h * V_ROWS + HEAD_DIM + 1, :]
            lmin = l if lmin is None else jnp.minimum(lmin, l)
            o_ref[h * HEAD_DIM:(h + 1) * HEAD_DIM, :] = acc_ref[h * V_ROWS:h * V_ROWS + HEAD_DIM, :] * (1.0 / l)
        out_ref[0] = o_ref[...].T.astype(out_ref.dtype)
        lmin_ref[0, 0] = jnp.broadcast_to(jnp.min(lmin, axis=1, keepdims=True), lmin_ref.shape[2:])


def _dense_attn_fast(qt, k, vt, kc, vct, knl, knc, *, heads, k_slot, v_row, tq, tk, name):
    b, nq, lq = qt.shape
    _, lk, nkw = k.shape
    nv = vt.shape[1]
    lc = kc.shape[1]
    nk = lk // tk
    assert nq == heads * LANES and lq % tq == 0 and lk % tk == 0
    n_tiles, n_slots = knl.shape[1:3]
    in_specs = [
        pl.BlockSpec((1, nq, tq), lambda bi, qi, ki: (bi, 0, qi)),
        pl.BlockSpec((1, tk, nkw), lambda bi, qi, ki: (bi, jnp.minimum(ki, nk - 1), 0)),
        pl.BlockSpec((1, nv, tk), lambda bi, qi, ki: (bi, 0, jnp.minimum(ki, nk - 1))),
        pl.BlockSpec((1, lc, nkw), lambda bi, qi, ki: (bi, 0, 0)),
        pl.BlockSpec((1, nv, lc), lambda bi, qi, ki: (bi, 0, 0)),
        pl.BlockSpec((1, n_tiles, n_slots, LANES), lambda bi, qi, ki: (bi, 0, 0, 0)),
        pl.BlockSpec((1, 1, n_slots, LANES), lambda bi, qi, ki: (bi, 0, 0, 0)),
    ]
    body = functools.partial(_dense_fast_body, heads=heads, k_slot=tuple(k_slot), v_row=tuple(v_row), nk=nk)
    return pl.pallas_call(
        body, grid=(b, lq // tq, nk + 1), in_specs=in_specs,
        out_specs=[pl.BlockSpec((1, tq, heads * HEAD_DIM), lambda bi, qi, ki: (bi, qi, 0)),
                   pl.BlockSpec((1, 1, SUBLANES, LANES), lambda bi, qi, ki: (bi, qi, 0, 0))],
        out_shape=[jax.ShapeDtypeStruct((b, lq, heads * HEAD_DIM), BF16),
                   jax.ShapeDtypeStruct((b, lq // tq, SUBLANES, LANES), F32)],
        scratch_shapes=[pltpu.VMEM((heads, tq), F32), pltpu.VMEM((heads * V_ROWS, tq), F32),
                        pltpu.VMEM((heads * HEAD_DIM, tq), F32)],
        name=name, compiler_params=_cparams(("parallel", "parallel", "arbitrary")),
    )(qt, k, vt, kc, vct, knl, knc)


def _attend(qt, k, vt, kc, vct, knl, knc, *, name, **kw):
    y, lmin = _dense_attn_fast(qt, k, vt, kc, vct, knl, knc, name=name + "_fast", **kw)
    return lax.cond(jnp.min(lmin) > L_MIN_OK, lambda: y,
                    lambda: _dense_attn(qt, k, vt, kc, vct, name=name, **kw))


NA_QROWS = 4
NA_TQ = NA_QROWS * GRID_W
NA_WROWS = 12
NA_WBLOCKS = NA_WROWS // NA_QROWS
NA_STEPS = GRID_H // NA_QROWS


def _natten_body(qt_ref, k0_ref, k1_ref, k2_ref, v0_ref, v1_ref, v2_ref, kc_ref, vct_ref, bias_ref,
                 out_ref, o_ref):
    k_refs = (k0_ref, k1_ref, k2_ref)
    v_refs = (v0_ref, v1_ref, v2_ref)

    def scores(h):
        lanes = slice((h // 2) * LANES, (h // 2 + 1) * LANES)
        qq = qt_ref[0, h * LANES:(h + 1) * LANES, :]
        return [_dot(kc_ref[0, :, lanes], qq)] + [_dot(k_refs[blk][0, :, lanes], qq) for blk in range(NA_WBLOCKS)]

    pending = [scores(h) for h in range(AHEAD)]
    for h in range(B_HEADS):
        s_all = pending.pop(0)
        if h + AHEAD < B_HEADS:
            pending.append(scores(h + AHEAD))
        vrows = slice(h * V_ROWS, (h + 1) * V_ROWS)
        s_all = [s_all[0]] + [s_all[1 + blk] + bias_ref[0, h, blk * NA_TQ:(blk + 1) * NA_TQ, :]
                              for blk in range(NA_WBLOCKS)]
        m = s_all[0].max(axis=0, keepdims=True)
        for s in s_all[1:]:
            m = jnp.maximum(m, s.max(axis=0, keepdims=True))
        pv = _dot(vct_ref[0, vrows, :], jnp.exp2(s_all[0] - m).astype(BF16))
        for blk in range(NA_WBLOCKS):
            pv = pv + _dot(v_refs[blk][0, vrows, :], jnp.exp2(s_all[1 + blk] - m).astype(BF16))
        o_ref[h * HEAD_DIM:(h + 1) * HEAD_DIM, :] = pv[:HEAD_DIM] * (1.0 / pv[HEAD_DIM:HEAD_DIM + 1])
    out_ref[0] = o_ref[...].T.astype(out_ref.dtype)


def _natten_fast_body(qt_ref, k0_ref, k1_ref, k2_ref, v0_ref, v1_ref, v2_ref, kc_ref, vct_ref, bias_ref,
                      knl_ref, knc_ref, bmax_ref, out_ref, lmin_ref, o_ref, kall_ref, vall_ref):
    kall_ref[0:CTX_LEN, :] = kc_ref[0]
    vall_ref[:, 0:CTX_LEN] = vct_ref[0]
    for blk, (kr, vr) in enumerate(((k0_ref, v0_ref), (k1_ref, v1_ref), (k2_ref, v2_ref))):
        kall_ref[CTX_LEN + blk * NA_TQ:CTX_LEN + (blk + 1) * NA_TQ, :] = kr[0]
        vall_ref[:, CTX_LEN + blk * NA_TQ:CTX_LEN + (blk + 1) * NA_TQ] = vr[0]
    kn = jnp.sqrt(jnp.maximum(jnp.max(knl_ref[0], axis=0), knc_ref[0, 0])) * BOUND_SLACK

    def scores(h):
        lanes = slice((h // 2) * LANES, (h // 2 + 1) * LANES)
        return _dot(kall_ref[:, lanes], qt_ref[0, h * LANES:(h + 1) * LANES, :])

    pending = [scores(h) for h in range(AHEAD)]
    lmin = None
    for h in range(B_HEADS):
        s = pending.pop(0)
        if h + AHEAD < B_HEADS:
            pending.append(scores(h + AHEAD))
        qf = qt_ref[0, h * LANES:(h + 1) * LANES, :].astype(F32)
        qn = jnp.sqrt(jnp.sum(qf * qf, axis=0, keepdims=True))
        krow = kn[h // 2:h // 2 + 1, :]
        shift = qn * jnp.concatenate([krow] * (NA_TQ // LANES), axis=1) + bmax_ref[h:h + 1, :]
        p = jnp.concatenate([jnp.exp2(s[:CTX_LEN] - shift).astype(BF16),
                             jnp.exp2(s[CTX_LEN:] + (bias_ref[0, h] - shift)).astype(BF16)], axis=0)
        pv = _dot(vall_ref[h * V_ROWS:(h + 1) * V_ROWS, :], p)
        l = pv[HEAD_DIM:HEAD_DIM + 1]
        lmin = l if lmin is None else jnp.minimum(lmin, l)
        o_ref[h * HEAD_DIM:(h + 1) * HEAD_DIM, :] = pv[:HEAD_DIM] * (1.0 / l)
    out_ref[0] = o_ref[...].T.astype(out_ref.dtype)
    lmin_ref[0, 0] = jnp.broadcast_to(jnp.min(lmin, axis=1, keepdims=True), lmin_ref.shape[2:])


def _natten_call(body, name, qt, k, vt, kc, vct, bias, extra=(), with_lmin=False, gather=False):
    b = qt.shape[0]
    nw = B_HEADS * HEAD_DIM
    nv = B_HEADS * V_ROWS

    def wstart(i):
        return jnp.clip(i - 1, 0, NA_STEPS - NA_WBLOCKS)

    def cls(i):
        return jnp.where(i == 0, 0, jnp.where(i == NA_STEPS - 1, 2, 1))

    in_specs = [pl.BlockSpec((1, B_HEADS * LANES, NA_TQ), lambda bi, i: (bi, 0, i))]
    in_specs += [pl.BlockSpec((1, NA_TQ, nw), functools.partial(lambda bi, i, j: (bi, wstart(i) + j, 0), j=j))
                 for j in range(NA_WBLOCKS)]
    in_specs += [pl.BlockSpec((1, nv, NA_TQ), functools.partial(lambda bi, i, j: (bi, 0, wstart(i) + j), j=j))
                 for j in range(NA_WBLOCKS)]
    in_specs += [pl.BlockSpec((1, CTX_LEN, nw), lambda bi, i: (bi, 0, 0)),
                 pl.BlockSpec((1, nv, CTX_LEN), lambda bi, i: (bi, 0, 0)),
                 pl.BlockSpec((1, B_HEADS, NA_WROWS * GRID_W, NA_TQ), lambda bi, i: (cls(i), 0, 0, 0))]
    for a, per_batch in extra:
        blk = (1,) + a.shape[1:] if per_batch else a.shape
        nz = len(blk) - 1
        in_specs.append(pl.BlockSpec(blk, (lambda bi, i, nz=nz: (bi,) + (0,) * nz) if per_batch
                                     else (lambda bi, i, nz=nz: (0,) * (nz + 1))))
    out_specs = [pl.BlockSpec((1, NA_TQ, nw), lambda bi, i: (bi, i, 0))]
    out_shape = [jax.ShapeDtypeStruct((b, SEQ, nw), BF16)]
    if with_lmin:
        out_specs.append(pl.BlockSpec((1, 1, SUBLANES, LANES), lambda bi, i: (bi, i, 0, 0)))
        out_shape.append(jax.ShapeDtypeStruct((b, NA_STEPS, SUBLANES, LANES), F32))
    return pl.pallas_call(
        body, grid=(b, NA_STEPS), in_specs=in_specs, out_specs=out_specs, out_shape=out_shape,
        scratch_shapes=[pltpu.VMEM((nw, NA_TQ), F32)] + ([pltpu.VMEM((CTX_LEN + NA_WROWS * GRID_W, nw), BF16),
                                                          pltpu.VMEM((nv, CTX_LEN + NA_WROWS * GRID_W), BF16)] if gather else []),
        name=name, compiler_params=_cparams(("parallel", "arbitrary")),
    )(qt, k, k, k, vt, vt, vt, kc, vct, bias, *[a for a, _ in extra])


def _natten(qt, k, vt, kc, vct, rpb, knl, knc):
    bias = _natten_bias(rpb)
    bmax = jnp.maximum(jnp.max(rpb.astype(F32), axis=(1, 2)), 0.0) * LOG2E
    bmax = jnp.broadcast_to(bmax[:, None], (B_HEADS, NA_TQ))
    y, lmin = _natten_call(_natten_fast_body, "natten_fast", qt, k, vt, kc, vct, bias,
                           extra=((knl, True), (knc, True), (bmax, False)), with_lmin=True, gather=True)
    return lax.cond(jnp.min(lmin) > L_MIN_OK, lambda: y,
                    lambda: _natten_call(_natten_body, "natten", qt, k, vt, kc, vct, bias)[0])


def _natten_bias(rpb):
    v = rpb.astype(F32) * LOG2E
    h, nr, _ = v.shape
    period = 2 * GRID_W
    row = jnp.zeros((h, nr, period), F32)
    row = row.at[..., :NA_COLS].set(v[..., NA_COLS - 1::-1])
    row = row.at[..., period - (NA_COLS - 1):].set(v[..., :NA_COLS - 1:-1])
    toep = jnp.tile(row, (1, 1, GRID_W))[..., :GRID_W * (period - 1)]
    toep = toep.reshape(h, nr, GRID_W, period - 1)[..., :GRID_W]
    col = np.arange(GRID_W)
    cs = np.clip(col - NA_COLS // 2, 0, GRID_W - NA_COLS)
    col_ok = (col[:, None] >= cs[None, :]) & (col[:, None] < cs[None, :] + NA_COLS)
    masked = 2.0 * MASK_VALUE
    toep = jnp.where(jnp.asarray(col_ok), toep, masked)
    dead = jnp.full((h, GRID_W, GRID_W), masked, F32)
    tables = []
    for r0 in (0, 2 * NA_QROWS, GRID_H - NA_QROWS):
        ws = min(max(r0 - NA_QROWS, 0), GRID_H - NA_WROWS)
        key_rows = []
        for ki in range(NA_WROWS):
            blocks = []
            for qi in range(NA_QROWS):
                k_row, q_row = ws + ki, r0 + qi
                rs = min(max(q_row - NA_ROWS // 2, 0), GRID_H - NA_ROWS)
                blocks.append(toep[:, k_row - q_row + NA_ROWS - 1] if rs <= k_row < rs + NA_ROWS else dead)
            key_rows.append(jnp.concatenate(blocks, axis=-1))
        tables.append(jnp.concatenate(key_rows, axis=-2))
    return jnp.stack(tables)


def _ffn_body(x_ref, xp_ref, xn_ref, mod_ref, wg_ref, wv_ref, wd_ref, cw_ref, cb_ref, lng_ref, lnb_ref,
              out_ref, hs_ref, g_ref, *, tm, rc, n_tiles):
    i = pl.program_id(1)
    sc = mod_ref[0, SC2:SC2 + 1, :]
    sh = mod_ref[0, SH2:SH2 + 1, :]
    gate_mod = mod_ref[0, G2:G2 + 1, :]

    def modulate(v):
        return v * (1.0 + sc) + sh

    hp = jnp.where(i > 0, modulate(xp_ref[0]), 0.0)
    hn = jnp.where(i < n_tiles - 1, modulate(xn_ref[0]), 0.0)
    hs_ref[0:FFN_HALO, :] = hp.astype(BF16)
    hs_ref[FFN_HALO:FFN_HALO + tm, :] = modulate(x_ref[0]).astype(BF16)
    hs_ref[FFN_HALO + tm:, :] = hn.astype(BF16)

    def up(c):
        lo = c * rc
        g_ref[c] = _dot(hs_ref[lo:lo + rc + 2 * FFN_HALO, :], wg_ref[...])
        return _dot(hs_ref[FFN_HALO + lo:FFN_HALO + lo + rc, :], wv_ref[...])

    n_chunks = tm // rc
    val_next = up(0)
    for c in range(n_chunks):
        val = val_next
        if c + 1 < n_chunks:
            val_next = up(c + 1)
        gate = (cw_ref[0:1, :] * g_ref[c, FFN_HALO - 1:FFN_HALO - 1 + rc, :]
                + cw_ref[1:2, :] * g_ref[c, FFN_HALO:FFN_HALO + rc, :]
                + cw_ref[2:3, :] * g_ref[c, FFN_HALO + 1:FFN_HALO + 1 + rc, :]
                + cb_ref[...])
        act = gate * jax.nn.sigmoid(gate) * val
        y = _dot(act.astype(BF16), wd_ref[...])
        rows = slice(c * rc, (c + 1) * rc)
        out_ref[0, rows, :] = _layer_norm(DEEPNORM_ALPHA * x_ref[0, rows, :] + gate_mod * y, lng_ref[...], lnb_ref[...])


def _ffn(x, mod, mod_row, wup, wd, conv_w, conv_b, ln_g, ln_b, *, tm, name):
    b, t, d = x.shape
    n_tiles = t // tm
    rc = min(FFN_ROWS, tm)
    hb = tm // FFN_HALO
    n_hblocks = t // FFN_HALO
    mod_map = (lambda bi, i: (bi, 0, 0)) if mod_row is None else (lambda bi, i: (mod_row, 0, 0))

    def resident(shape, col=0):
        return pl.BlockSpec(shape, lambda bi, i: (0, col), pipeline_mode=pl.Buffered(1))

    in_specs = [
        pl.BlockSpec((1, tm, d), lambda bi, i: (bi, i, 0)),
        pl.BlockSpec((1, FFN_HALO, d), lambda bi, i: (bi, jnp.maximum(i * hb - 1, 0), 0)),
        pl.BlockSpec((1, FFN_HALO, d), lambda bi, i: (bi, jnp.minimum((i + 1) * hb, n_hblocks - 1), 0)),
        pl.BlockSpec((1, N_MOD, d), mod_map),
        resident((d, D_FF), 0), resident((d, D_FF), 1), resident((D_FF, d)),
        resident((3, D_FF)), resident((1, D_FF)), resident((1, d)), resident((1, d)),
    ]
    body = functools.partial(_ffn_body, tm=tm, rc=rc, n_tiles=n_tiles)
    return pl.pallas_call(
        body, grid=(b, n_tiles), in_specs=in_specs,
        out_specs=pl.BlockSpec((1, tm, d), lambda bi, i: (bi, i, 0)),
        out_shape=jax.ShapeDtypeStruct((b, t, d), F32),
        scratch_shapes=[pltpu.VMEM((tm + 2 * FFN_HALO, d), BF16),
                        pltpu.VMEM((tm // rc, rc + 2 * FFN_HALO, D_FF), F32)],
        name=name, compiler_params=_cparams(("parallel", "parallel")),
    )(x, x, x, mod, wup, wup, wd, conv_w.astype(F32), conv_b.reshape(1, D_FF).astype(F32),
      ln_g.reshape(1, d).astype(F32), ln_b.reshape(1, d).astype(F32))


def _rope_cos_sin(rot_dim):
    pos = jnp.arange(SEQ, dtype=jnp.int32)
    rows = (pos // GRID_W).astype(F32)
    cols = (pos % GRID_W).astype(F32)
    axis_dim = rot_dim // 2
    inv = ROPE_THETA ** (-jnp.arange(0, axis_dim, 2, dtype=F32) / axis_dim)
    ang = jnp.concatenate([rows[:, None] * inv, cols[:, None] * inv], axis=-1)
    cos = jnp.repeat(jnp.cos(ang), 2, axis=-1)
    sin = jnp.repeat(jnp.sin(ang), 2, axis=-1) * jnp.tile(jnp.array([-1.0, 1.0], F32), rot_dim // 2)
    return cos, sin


def _swap_pairs(g):
    return g.reshape(-1, 2)[:, ::-1].reshape(-1)


def _gqa_tables(gain, scale, length, rope):
    if rope:
        cos, sin = _rope_cos_sin(HEAD_DIM)
    else:
        cos, sin = jnp.ones((length, HEAD_DIM), F32), jnp.zeros((length, HEAD_DIM), F32)
    c = cos * (gain * scale)
    s = sin * (_swap_pairs(gain) * scale)
    return jnp.concatenate([c, c], axis=-1), jnp.concatenate([s, s], axis=-1)


def _mla_tables(scale, length, rope):
    c = jnp.full((length, LANES), scale, F32)
    s = jnp.zeros((length, LANES), F32)
    if rope:
        cos, sin = _rope_cos_sin(MLA_ROPE_DIM)
        lo, hi = MLA_NOPE_DIM, MLA_NOPE_DIM + MLA_ROPE_DIM
        c = c.at[:, lo:hi].set(cos * scale)
        s = s.at[:, lo:hi].set(sin * scale)
    return c, s


def _slot_rows(w, heads, width, half_of):
    k = w.shape[0]
    out = jnp.zeros((heads * LANES, k), w.dtype)
    for h in range(heads):
        off = h * LANES + half_of(h) * HALF
        out = out.at[off:off + width].set(w[:, h * width:(h + 1) * width].T)
    return out


def _layer0_mixer(x, xc, mod, w_in, q_gain, k_gain, rpb):
    w = w_in.astype(BF16)
    aq, akv, bw = A_HEADS * HEAD_DIM, A_KV_HEADS * HEAD_DIM, B_HEADS * HEAD_DIM
    o = 0
    w_qa = w[:, o:o + aq]; o += aq
    w_ka = w[:, o:o + akv]; o += akv
    w_va = w[:, o:o + akv]; o += akv
    w_qb = w[:, o:o + bw]; o += bw
    w_kb = w[:, o:o + bw]; o += bw
    w_vb = w[:, o:o + bw]
    wqa = jnp.concatenate([w_qa, w_va], axis=1).T
    wqb = jnp.concatenate([w_qb, w_vb], axis=1).T
    wka = jnp.concatenate([w_ka, w_kb], axis=1)
    weights = (wqa, wka, wqb)
    qscale = HEAD_SCALE * LOG2E

    def tables(length, rope):
        cq, sq = _gqa_tables(q_gain, qscale, length, rope)
        ck, sk = _gqa_tables(k_gain, 1.0, length, rope)
        return cq[:, :HEAD_DIM].T, sq[:, :HEAD_DIM].T, ck, sk

    qa_t, ka, kna, va_t, qb_t, kb, knb, vb_t = _l0_proj(x, mod, None, weights, tables(SEQ, True), tm=TM_WIDE, name="l0_proj")
    qac_t, kac, knac, vac_t, qbc_t, kbc, knbc, vbc_t = _l0_proj(xc, mod, CTX_MOD_ROW, weights, tables(CTX_LEN, False),
                                                    tm=TMC, name="l0_proj_ctx")

    a_kslot = [0] * A_HEADS
    a_vrow = [(h // A_GROUP) * V_ROWS for h in range(A_HEADS)]
    b_kslot = [h // 2 for h in range(B_HEADS)]
    b_vrow = [h * V_ROWS for h in range(B_HEADS)]
    ya = _attend(qa_t, ka, va_t, kac, vac_t, kna, knac, heads=A_HEADS, k_slot=a_kslot, v_row=a_vrow,
                 tq=TQ, tk=4 * TK, name="gqa")
    yb = _natten(qb_t, kb, vb_t, kbc, vbc_t, rpb, knb, knbc)
    yac = _dense_attn(qac_t, kac, vac_t, heads=A_HEADS, k_slot=a_kslot, v_row=a_vrow,
                      tq=CTX_LEN, tk=CTX_LEN, name="gqa_ctx")
    ybc = _dense_attn(qbc_t, kbc, vbc_t, heads=B_HEADS, k_slot=b_kslot, v_row=b_vrow,
                      tq=CTX_LEN, tk=CTX_LEN, name="nbr_ctx")
    return (ya, yb), (yac, ybc)


def _layer1_mixer(x, xc, mod, w_in, cq_gain, ckv_gain, w_uq, w_ukv):
    qk_dim = MLA_NOPE_DIM + MLA_ROPE_DIM
    kv_dim = MLA_NOPE_DIM + MLA_V_DIM
    n_in = MLA_Q_LORA + MLA_KV_LORA
    wdn = jnp.zeros((D_MODEL, n_in + LANES), BF16)
    wdn = wdn.at[:, :n_in].set(w_in[:, :n_in].astype(BF16))
    wdn = wdn.at[:, n_in + MLA_NOPE_DIM:n_in + qk_dim].set(w_in[:, n_in:].astype(BF16))
    wq = _slot_rows(w_uq.astype(BF16), MLA_HEADS, qk_dim, lambda h: 0)
    wkv = w_ukv.astype(BF16).reshape(MLA_KV_LORA, MLA_HEADS, kv_dim)
    wk = _slot_rows(wkv[:, :, :MLA_NOPE_DIM].reshape(MLA_KV_LORA, -1), MLA_HEADS, MLA_NOPE_DIM, lambda h: 0).T
    wv = wkv[:, :, MLA_NOPE_DIM:].reshape(MLA_KV_LORA, MLA_HEADS * MLA_V_DIM).T
    weights = (wdn, wq, wk, wv)
    gains = (cq_gain.reshape(1, -1).astype(F32), ckv_gain.reshape(1, -1).astype(F32))

    cq, sq = _mla_tables(MLA_SCALE * LOG2E, SEQ, True)
    qt, kk, kn, vt = _l1_proj(x, mod, None, weights, gains, (cq.T, sq.T), _mla_tables(1.0, SEQ, True),
                          tm=TM, with_q=True, name="l1_proj")
    kkc, knc, vtc = _l1_proj(xc, mod, CTX_MOD_ROW, weights, gains, None, _mla_tables(1.0, CTX_LEN, False),
                        tm=TMC, with_q=False, name="l1_proj_ctx")
    return _attend(qt, kk, vt, kkc, vtc, kn, knc, heads=MLA_HEADS, k_slot=list(range(MLA_HEADS)),
                   v_row=[h * V_ROWS for h in range(MLA_HEADS)], tq=TQ, tk=TK, name="mla")


def _post_mixer(x, y, mod, mod_row, w_out, ln1, w_up, conv_w, conv_b, w_down, ln2, tm, tag):
    wo = w_out.astype(BF16)
    if isinstance(y, tuple):
        n0 = y[0].shape[2]
        extra = dict(x2=y[1], w2=wo[n0:])
        y, wo = y[0], wo[:n0]
    else:
        extra = {}
    x = _proj(y, wo, name="out_ln" + tag, tm=min(TM_WIDE, x.shape[1]), tn=D_MODEL, mod=mod, mod_row=mod_row,
              gate_row=G1, resid=x, ln=ln1, **extra)
    return _ffn(x, mod, mod_row, w_up.astype(BF16), w_down.astype(BF16), conv_w, conv_b, ln2[0], ln2[1],
                tm=tm, name="ffn" + tag)


def kernel(x, c, ctx, c_ctx, l0_w_ada, l0_b_ada, l0_w_in, l0_q_gain, l0_k_gain, l0_rpb, l0_w_out, l0_ln1_g, l0_ln1_b, l0_w_up, l0_conv_w, l0_conv_b, l0_w_down, l0_ln2_g, l0_ln2_b, l1_w_ada, l1_b_ada, l1_w_in, l1_cq_gain, l1_ckv_gain, l1_w_uq, l1_w_ukv, l1_w_out, l1_ln1_g, l1_ln1_b, l1_w_up, l1_conv_w, l1_conv_b, l1_w_down, l1_ln2_g, l1_ln2_b):
    xc = ctx
    mod = _modvec(c, c_ctx, l0_w_ada, l0_b_ada)
    y, yc = _layer0_mixer(x, xc, mod, l0_w_in, l0_q_gain, l0_k_gain, l0_rpb)
    post0 = (l0_w_out, (l0_ln1_g, l0_ln1_b), l0_w_up, l0_conv_w, l0_conv_b, l0_w_down, (l0_ln2_g, l0_ln2_b))
    x = _post_mixer(x, y, mod, None, *post0, tm=TM, tag="0")
    xc = _post_mixer(xc, yc, mod, CTX_MOD_ROW, *post0, tm=TMC, tag="0_ctx")
    mod = _modvec(c, c_ctx, l1_w_ada, l1_b_ada)
    y = _layer1_mixer(x, xc, mod, l1_w_in, l1_cq_gain, l1_ckv_gain, l1_w_uq, l1_w_ukv)
    post1 = (l1_w_out, (l1_ln1_g, l1_ln1_b), l1_w_up, l1_conv_w, l1_conv_b, l1_w_down, (l1_ln2_g, l1_ln2_b))
    return _post_mixer(x, y, mod, None, *post1, tm=TM, tag="1")
```
